```python
import math
import jax, jax.numpy as jnp
from jax import lax
import numpy as np

D_MODEL = 1024
BATCH = 4
SEQ = 8192
DEPTH = 1

CTX_LEN = 256
GRID_W = 64
GDN_HEADS = 8
GDN_DK = 128
GDN_DV = 128
GDN_CONV = 3
ML_HEADS = 4
ML_DK = 128
ML_DV = 256
CHUNK = 64
GATE_CAP = 15.0
D_FF = 2816
FFN_CONV = 3
N_MOD = 6
EPS = 1e-6

GDN_QK = GDN_HEADS * GDN_DK
GDN_V = GDN_HEADS * GDN_DV
ML_QK = ML_HEADS * ML_DK
ML_V = ML_HEADS * ML_DV
STATE_SIZES = (2 * GDN_QK + GDN_V, 2 * GDN_HEADS, 2 * GDN_HEADS, ML_QK, ML_QK, ML_V, 2 * ML_HEADS, 2 * ML_HEADS)
OUT_SIZES = (GDN_V, ML_V, D_MODEL, D_MODEL)
N_STATE_COLS = sum(STATE_SIZES)
N_IN_COLS = N_STATE_COLS + sum(OUT_SIZES)

kernel_name = 'hybrid_gdn_mlstm_convffn_ctx_prefix'


def rms_norm(x, w):
    xf = x.astype(jnp.float32)
    y = xf * lax.rsqrt(jnp.mean(xf * xf, axis=-1, keepdims=True) + EPS)
    return (y * w.astype(jnp.float32)).astype(x.dtype)


def l2_normalize(x):
    xf = x.astype(jnp.float32)
    return xf * lax.rsqrt(jnp.sum(xf * xf, axis=-1, keepdims=True) + EPS)


def soft_cap(x):
    return GATE_CAP * jnp.tanh(x / GATE_CAP)


def split_cols(z, sizes):
    return jnp.split(z, [int(s) for s in np.cumsum(sizes)[:-1]], axis=-1)


def to_heads(a, n_heads):
    b, t, _ = a.shape
    return a.reshape(b, t, n_heads, -1).transpose(0, 2, 1, 3)


def dir_heads(a, n_heads):
    b, t, _ = a.shape
    return a.reshape(b, t, 2, n_heads).transpose(2, 0, 3, 1)


def flip_t(a):
    return jnp.flip(a, axis=2)


def dwconv_seq(x, w):
    k = w.shape[0]
    return lax.conv_general_dilated(x, w[:, None, :], window_strides=(1,), padding=[(k // 2, k // 2)],
                                    dimension_numbers=('NWC', 'WIO', 'NWC'), feature_group_count=x.shape[-1])


def dwconv_grid(x, w):
    b, t, ch = x.shape
    rows = t // GRID_W
    kh, kw = w.shape[0], w.shape[1]
    y = lax.conv_general_dilated(x.reshape(b, rows, GRID_W, ch), w[:, :, None, :], window_strides=(1, 1),
                                 padding=[(kh // 2, kh // 2), (kw // 2, kw // 2)],
                                 dimension_numbers=('NHWC', 'HWIO', 'NHWC'), feature_group_count=ch)
    return y.reshape(b, t, ch)


def gdn_chunked(q, k, v, g, beta, s0, with_output):
    f32 = jnp.float32
    b, h, t, dk = q.shape
    dv = v.shape[-1]
    n = t // CHUNK
    k = k.astype(f32).reshape(b, h, n, CHUNK, dk)
    v = v.astype(f32).reshape(b, h, n, CHUNK, dv)
    beta = beta.astype(f32).reshape(b, h, n, CHUNK)
    gc = jnp.cumsum(g.astype(f32).reshape(b, h, n, CHUNK), axis=-1)
    lower = jnp.tril(jnp.ones((CHUNK, CHUNK), dtype=bool))
    decay = jnp.exp(jnp.where(lower, gc[..., :, None] - gc[..., None, :], -jnp.inf))
    kb = k * beta[..., None]
    a_strict = jnp.einsum('bhnld,bhnsd->bhnls', kb, k) * decay
    rhs = jnp.concatenate([v * beta[..., None], kb * jnp.exp(gc)[..., None]], axis=-1)
    sol = lax.linalg.triangular_solve(a_strict, rhs, left_side=True, lower=True, unit_diagonal=True)
    u, w = sol[..., :dv], sol[..., dv:]
    k_tail = k * jnp.exp(gc[..., -1:] - gc)[..., None]
    g_tot = jnp.exp(gc[..., -1])
    xs = [u, w, k_tail, g_tot]
    if with_output:
        q = q.astype(f32).reshape(b, h, n, CHUNK, dk) * (dk ** -0.5)
        attn = jnp.einsum('bhnld,bhnsd->bhnls', q, k) * decay
        xs += [q * jnp.exp(gc)[..., None], attn]
    xs = tuple(jnp.moveaxis(a, 2, 0) for a in xs)

    def step(s, xc):
        u_n, w_n, kt_n, gt_n = xc[:4]
        v_new = u_n - jnp.einsum('bhld,bhde->bhle', w_n, s)
        s_next = s * gt_n[..., None, None] + jnp.einsum('bhld,bhle->bhde', kt_n, v_new)
        if not with_output:
            return s_next, None
        qd_n, at_n = xc[4:]
        o = jnp.einsum('bhld,bhde->bhle', qd_n, s) + jnp.einsum('bhls,bhse->bhle', at_n, v_new)
        return s_next, o

    s_fin, o = lax.scan(step, s0, xs)
    if with_output:
        o = jnp.moveaxis(o, 0, 2).reshape(b, h, t, dv)
    return o, s_fin


def mlstm_chunked(q, k, v, ig, lf, state, with_output):
    f32 = jnp.float32
    b, h, t, dk = q.shape
    dv = v.shape[-1]
    n = t // CHUNK
    k = k.astype(f32).reshape(b, h, n, CHUNK, dk)
    v = v.astype(f32).reshape(b, h, n, CHUNK, dv)
    ig = ig.astype(f32).reshape(b, h, n, CHUNK)
    bc = jnp.cumsum(lf.astype(f32).reshape(b, h, n, CHUNK), axis=-1)
    b_last = bc[..., -1]
    tail = b_last[..., None] - bc + ig
    tail_max = jnp.max(tail, axis=-1)
    xs = [k, v, b_last, tail, tail_max]
    if with_output:
        q = q.astype(f32).reshape(b, h, n, CHUNK, dk) * (dk ** -0.5)
        lower = jnp.tril(jnp.ones((CHUNK, CHUNK), dtype=bool))
        dmat = jnp.where(lower, bc[..., :, None] - bc[..., None, :] + ig[..., None, :], -jnp.inf)
        qk = jnp.einsum('bhnld,bhnsd->bhnls', q, k)
        xs += [q, bc, dmat, jnp.max(dmat, axis=-1), qk]
    xs = tuple(jnp.moveaxis(a, 2, 0) for a in xs)

    def step(carry, xc):
        c_st, n_st, m = carry
        k_n, v_n, bl_n, tl_n, tm_n = xc[:5]
        m_new = jnp.maximum(bl_n + m, tm_n)
        wt = jnp.exp(tl_n - m_new[..., None])
        dec = jnp.exp(bl_n + m - m_new)
        c_next = dec[..., None, None] * c_st + jnp.einsum('bhld,bhle->bhde', k_n * wt[..., None], v_n)
        n_next = dec[..., None] * n_st + jnp.einsum('bhl,bhld->bhd', wt, k_n)
        if not with_output:
            return (c_next, n_next, m_new), None
        q_n, b_n, d_n, dm_n, qk_n = xc[5:]
        m_t = jnp.maximum(b_n + m[..., None], dm_n)
        inter = jnp.exp(b_n + m[..., None] - m_t)
        p = qk_n * jnp.exp(d_n - m_t[..., None])
        num = inter[..., None] * jnp.einsum('bhld,bhde->bhle', q_n, c_st) + jnp.einsum('bhls,bhse->bhle', p, v_n)
        den = inter * jnp.einsum('bhld,bhd->bhl', q_n, n_st) + jnp.sum(p, axis=-1)
        out = num / jnp.maximum(jnp.abs(den), jnp.exp(-m_t))[..., None]
        return (c_next, n_next, m_new), out

    fin, out = lax.scan(step, state, xs)
    if with_output:
        out = jnp.moveaxis(out, 0, 2).reshape(b, h, t, dv)
    return out, fin


def token_mixer(hn, states, with_output, w_in, gdn_conv, gdn_a_log, gdn_dt_bias, gdn_norm_w,
                ml_igate_b, ml_fgate_b, ml_norm_w, w_branch_gdn, w_branch_ml, w_out):
    bsz, t, _ = hn.shape
    s_gdn_f, s_gdn_b, s_ml_f, s_ml_b = states
    zs = hn @ w_in[:, :N_STATE_COLS]
    gdn_qkv, gdn_a, gdn_b, ml_q, ml_k, ml_v, ml_i, ml_f = split_cols(zs, STATE_SIZES)
    gdn_qkv = jax.nn.silu(dwconv_seq(gdn_qkv, gdn_conv))
    gq, gk, gv = split_cols(gdn_qkv, (GDN_QK, GDN_QK, GDN_V))
    gq = l2_normalize(to_heads(gq, GDN_HEADS))
    gk = l2_normalize(to_heads(gk, GDN_HEADS))
    gv = to_heads(gv, GDN_HEADS)
    a = dir_heads(gdn_a, GDN_HEADS).astype(jnp.float32)
    log_decay = -jnp.exp(gdn_a_log.astype(jnp.float32))[:, None, :, None] * jax.nn.softplus(
        a + gdn_dt_bias.astype(jnp.float32)[:, None, :, None])
    beta = jax.nn.sigmoid(dir_heads(gdn_b, GDN_HEADS))
    o_f, s_gdn_f = gdn_chunked(gq, gk, gv, log_decay[0], beta[0], s_gdn_f, with_output)
    o_b, s_gdn_b = gdn_chunked(flip_t(gq), flip_t(gk), flip_t(gv), flip_t(log_decay[1]), flip_t(beta[1]),
                               s_gdn_b, with_output)
    mq = to_heads(ml_q, ML_HEADS)
    mk = to_heads(ml_k, ML_HEADS)
    mv = to_heads(ml_v, ML_HEADS)
    ig = soft_cap(dir_heads(ml_i, ML_HEADS).astype(jnp.float32) + ml_igate_b.astype(jnp.float32)[:, None, :, None])
    lf = jax.nn.log_sigmoid(soft_cap(dir_heads(ml_f, ML_HEADS).astype(jnp.float32)
                                     + ml_fgate_b.astype(jnp.float32)[:, None, :, None]))
    h_f, s_ml_f = mlstm_chunked(mq, mk, mv, ig[0], lf[0], s_ml_f, with_output)
    h_b, s_ml_b = mlstm_chunked(flip_t(mq), flip_t(mk), flip_t(mv), flip_t(ig[1]), flip_t(lf[1]),
                                s_ml_b, with_output)
    new_states = (s_gdn_f, s_gdn_b, s_ml_f, s_ml_b)
    if not with_output:
        return None, new_states
    zo = hn @ w_in[:, N_STATE_COLS:]
    gdn_z, ml_o, gate_gdn, gate_ml = split_cols(zo, OUT_SIZES)
    o = (o_f + flip_t(o_b)).astype(hn.dtype).transpose(0, 2, 1, 3)
    o = rms_norm(o, gdn_norm_w) * jax.nn.silu(gdn_z.reshape(bsz, t, GDN_HEADS, GDN_DV))
    y_gdn = o.reshape(bsz, t, GDN_V) @ w_branch_gdn
    hm = (h_f + flip_t(h_b)).astype(hn.dtype).transpose(0, 2, 1, 3)
    hm = rms_norm(hm, ml_norm_w) * jax.nn.sigmoid(ml_o.reshape(bsz, t, ML_HEADS, ML_DV))
    y_ml = hm.reshape(bsz, t, ML_V) @ w_branch_ml
    merged = jax.nn.sigmoid(gate_gdn) * y_gdn + jax.nn.sigmoid(gate_ml) * y_ml
    return merged @ w_out, new_states


def conv_ffn(hn, w_up, conv_w, w_down, on_grid):
    u = hn @ w_up
    u = dwconv_grid(u, conv_w) if on_grid else dwconv_seq(u, conv_w[FFN_CONV // 2])
    gate, val = u[..., :D_FF], u[..., D_FF:]
    return (jax.nn.silu(gate) * val) @ w_down


def setup_inputs(seed: int = 0) -> dict:
    key = jax.random.key(seed)
    ks = jax.random.split(key, 32)
    f32 = jnp.float32

    def nrm(k, shape, scale):
        return jax.random.normal(k, shape, f32) * scale

    L, D = DEPTH, D_MODEL
    dt = jnp.exp(jax.random.uniform(ks[10], (L, 2, GDN_HEADS), f32, math.log(1e-3), math.log(0.1)))
    return {
        'x': nrm(ks[0], (BATCH, SEQ, D), 1.0),
        'c': nrm(ks[1], (BATCH, D), 1.0),
        'ctx': nrm(ks[2], (BATCH, CTX_LEN, D), 1.0),
        'c_ctx': nrm(ks[3], (D,), 1.0),
        'w_ada': nrm(ks[4], (L, D, N_MOD * D), 0.5 * D ** -0.5),
        'b_ada': nrm(ks[5], (L, N_MOD * D), 0.02),
        'norm1_w': 1.0 + nrm(ks[6], (L, D), 0.02),
        'w_in': nrm(ks[7], (L, D, N_IN_COLS), D ** -0.5),
        'gdn_conv': nrm(ks[8], (L, GDN_CONV, 2 * GDN_QK + GDN_V), GDN_CONV ** -0.5),
        'gdn_a_log': jnp.log(jax.random.uniform(ks[9], (L, 2, GDN_HEADS), f32, 1.0, 16.0)),
        'gdn_dt_bias': dt + jnp.log(-jnp.expm1(-dt)),
        'gdn_norm_w': 1.0 + nrm(ks[11], (L, GDN_DV), 0.02),
        'ml_igate_b': nrm(ks[12], (L, 2, ML_HEADS), 0.1),
        'ml_fgate_b': 3.0 + nrm(ks[13], (L, 2, ML_HEADS), 0.5),
        'ml_norm_w': 1.0 + nrm(ks[14], (L, ML_HEADS, ML_DV), 0.02),
        'w_branch_gdn': nrm(ks[15], (L, GDN_V, D), GDN_V ** -0.5),
        'w_branch_ml': nrm(ks[16], (L, ML_V, D), ML_V ** -0.5),
        'w_out': nrm(ks[17], (L, D, D), D ** -0.5),
        'norm2_w': 1.0 + nrm(ks[18], (L, D), 0.02),
        'w_up': nrm(ks[19], (L, D, 2 * D_FF), D ** -0.5),
        'ffn_conv': nrm(ks[20], (L, FFN_CONV, FFN_CONV, 2 * D_FF), 1.0 / FFN_CONV),
        'w_down': nrm(ks[21], (L, D_FF, D), D_FF ** -0.5),
        'norm_out_w': 1.0 + nrm(ks[22], (D,), 0.02),
    }


def reference(x, c, ctx, c_ctx, w_ada, b_ada, norm1_w, w_in, gdn_conv, gdn_a_log, gdn_dt_bias, gdn_norm_w,
              ml_igate_b, ml_fgate_b, ml_norm_w, w_branch_gdn, w_branch_ml, w_out, norm2_w, w_up, ffn_conv,
              w_down, norm_out_w):
    f32 = jnp.float32
    bsz = x.shape[0]
    s_gdn0 = jnp.zeros((bsz, GDN_HEADS, GDN_DK, GDN_DV), f32)
    s_ml0 = (jnp.zeros((bsz, ML_HEADS, ML_DK, ML_DV), f32), jnp.zeros((bsz, ML_HEADS, ML_DK), f32),
             jnp.zeros((bsz, ML_HEADS), f32))
    zero_states = (s_gdn0, s_gdn0, s_ml0, s_ml0)
    for l in range(DEPTH):
        last = l == DEPTH - 1
        mix_w = (w_in[l], gdn_conv[l], gdn_a_log[l], gdn_dt_bias[l], gdn_norm_w[l], ml_igate_b[l],
                 ml_fgate_b[l], ml_norm_w[l], w_branch_gdn[l], w_branch_ml[l], w_out[l])
        mod_x = (jax.nn.silu(c) @ w_ada[l] + b_ada[l]).reshape(bsz, N_MOD, 1, D_MODEL)
        mod_c = (jax.nn.silu(c_ctx) @ w_ada[l] + b_ada[l]).reshape(N_MOD, D_MODEL)
        hc = rms_norm(ctx, norm1_w[l]) * (1.0 + mod_c[1]) + mod_c[0]
        ctx_mix, ctx_states = token_mixer(hc, zero_states, not last, *mix_w)
        hx = rms_norm(x, norm1_w[l]) * (1.0 + mod_x[:, 1]) + mod_x[:, 0]
        x_mix, _ = token_mixer(hx, ctx_states, True, *mix_w)
        x = x + mod_x[:, 2] * x_mix
        hx = rms_norm(x, norm2_w[l]) * (1.0 + mod_x[:, 4]) + mod_x[:, 3]
        x = x + mod_x[:, 5] * conv_ffn(hx, w_up[l], ffn_conv[l], w_down[l], True)
        if not last:
            ctx = ctx + mod_c[2] * ctx_mix
            hc = rms_norm(ctx, norm2_w[l]) * (1.0 + mod_c[4]) + mod_c[3]
            ctx = ctx + mod_c[5] * conv_ffn(hc, w_up[l], ffn_conv[l], w_down[l], False)
    return rms_norm(x, norm_out_w)
```

```python
import functools

import jax
import jax.numpy as jnp
from jax import lax
from jax.experimental import pallas as pl
from jax.experimental.pallas import tpu as pltpu

F32 = jnp.float32
BF16 = jnp.bfloat16

GDN_HEADS = 8
GDN_DK = 128
GDN_DV = 128
ML_HEADS = 4
ML_DK = 128
ML_DV = 256
CHUNK = 64
GATE_CAP = 15.0
GRID_W = 64
N_MOD = 6
EPS = 1e-6
LANES = 128
NEG_BIG = -1e30

GDN_QK = GDN_HEADS * GDN_DK
GDN_V = GDN_HEADS * GDN_DV
ML_QK = ML_HEADS * ML_DK
ML_V = ML_HEADS * ML_DV
STATE_COLS = 2 * GDN_QK + GDN_V + 2 * ML_QK + ML_V
OUT_COLS = GDN_V + ML_V + 2 * 1024

VMEM_LIMIT = 48 * 1024 * 1024


def _cparams(*sem):
    return pltpu.CompilerParams(dimension_semantics=sem, vmem_limit_bytes=VMEM_LIMIT)


def _dot(a, b):
    return jnp.dot(a, b, preferred_element_type=F32)


def _dot_nt(a, b):
    return lax.dot_general(a, b, (((1,), (1,)), ((), ())), preferred_element_type=F32)


def _dot_tn(a, b):
    return lax.dot_general(a, b, (((0,), (0,)), ((), ())), preferred_element_type=F32)


def _sigmoid(x):
    return 1.0 / (1.0 + jnp.exp(-x))


def _softplus(x):
    return jnp.maximum(x, 0.0) + jnp.log1p(jnp.exp(-jnp.abs(x)))


def _pick_tile(n, pref):
    t = min(n, pref)
    while n % t:
        t //= 2
    return t


def _mod_kernel(c_ref, w_ref, b_ref, o_ref):
    c = c_ref[...]
    s = c * _sigmoid(c)
    o_ref[...] = _dot(s.astype(BF16), w_ref[...]) + b_ref[...]


def _mod_call(c_all, w_ada, b_ada):
    rows, d = c_all.shape
    n = w_ada.shape[1]
    tn = 1024
    return pl.pallas_call(
        _mod_kernel,
        grid=(n // tn,),
        in_specs=[
            pl.BlockSpec((rows, d), lambda j: (0, 0)),
            pl.BlockSpec((d, tn), lambda j: (0, j)),
            pl.BlockSpec((1, tn), lambda j: (0, j)),
        ],
        out_specs=pl.BlockSpec((rows, tn), lambda j: (0, j)),
        out_shape=jax.ShapeDtypeStruct((rows, n), F32),
        compiler_params=_cparams("parallel"),
        name="mod",
    )(c_all, w_ada, b_ada)


def _nmm_kernel(x_ref, nw_ref, sh_ref, sc_ref, w_ref, *rest, has_aux):
    if has_aux:
        wa_ref, o_ref, oa_ref, hn_ref = rest
    else:
        o_ref, hn_ref = rest

    @pl.when(pl.program_id(2) == 0)
    def _():
        x = x_ref[0]
        ms = jnp.mean(x * x, axis=-1, keepdims=True)
        y = x * lax.rsqrt(ms + EPS) * nw_ref[...]
        hb = (y * (1.0 + sc_ref[0]) + sh_ref[0]).astype(BF16)
        hn_ref[...] = hb
        if has_aux:
            oa_ref[0] = _dot(hb, wa_ref[...])

    o_ref[0] = _dot(hn_ref[...], w_ref[...]).astype(o_ref.dtype)


def _nmm_call(x, nw, shift, scale, w, w_aux=None, out_dtype=F32, name="nmm"):
    b, t, d = x.shape
    c = w.shape[1]
    tm = _pick_tile(t, 1024)
    tn = _pick_tile(c, 512)
    has_aux = w_aux is not None
    in_specs = [
        pl.BlockSpec((1, tm, d), lambda bi, i, j: (bi, i, 0)),
        pl.BlockSpec((1, d), lambda bi, i, j: (0, 0)),
        pl.BlockSpec((1, 1, d), lambda bi, i, j: (bi, 0, 0)),
        pl.BlockSpec((1, 1, d), lambda bi, i, j: (bi, 0, 0)),
        pl.BlockSpec((d, tn), lambda bi, i, j: (0, j)),
    ]
    out_specs = [pl.BlockSpec((1, tm, tn), lambda bi, i, j: (bi, i, j))]
    out_shape = [jax.ShapeDtypeStruct((b, t, c), out_dtype)]
    args = [x, nw, shift, scale, w]
    if has_aux:
        in_specs.append(pl.BlockSpec((d, LANES), lambda bi, i, j: (0, 0)))
        out_specs.append(pl.BlockSpec((1, tm, LANES), lambda bi, i, j: (bi, i, 0)))
        out_shape.append(jax.ShapeDtypeStruct((b, t, LANES), F32))
        args.append(w_aux)
    res = pl.pallas_call(
        functools.partial(_nmm_kernel, has_aux=has_aux),
        grid=(b, t // tm, c // tn),
        in_specs=in_specs,
        out_specs=out_specs,
        out_shape=out_shape,
        scratch_shapes=[pltpu.VMEM((tm, d), BF16)],
        compiler_params=_cparams("parallel", "parallel", "arbitrary"),
        name=name,
    )(*args)
    return res if has_aux else res[0]


G_A, G_B, G_I, G_F = 0, 16, 32, 40


def _chunk_scan(x, pos, backward, op, ident):
    rows = x.shape[0]
    yf, yb = x, x
    s = 1
    while s < CHUNK:
        yf = op(yf, jnp.where(pos >= s, pltpu.roll(yf, s, 0), ident))
        yb = op(yb, jnp.where(pos + s < CHUNK, pltpu.roll(yb, rows - s, 0), ident))
        s *= 2
    return jnp.where(backward, yb, yf)


def _gates_kernel(z_ref, alog_ref, bias_ref, o_ref):
    z = z_ref[0] + bias_ref[...]
    lane = lax.broadcasted_iota(jnp.int32, z.shape, 1)
    pos = lax.broadcasted_iota(jnp.int32, z.shape, 0) & (CHUNK - 1)
    backward = ((lane < G_I) & ((lane & 15) >= 8)) | ((lane >= G_I) & ((lane & 7) >= 4))

    log_decay = -jnp.exp(alog_ref[...]) * _softplus(z)
    beta = _sigmoid(z)
    capped = GATE_CAP * jnp.tanh(z * (1.0 / GATE_CAP))
    log_forget = -_softplus(-capped)

    summand = jnp.where(lane < G_B, log_decay, jnp.where(lane >= G_F, log_forget, 0.0))
    csum = _chunk_scan(summand, pos, backward, jnp.add, 0.0)
    o_ref[0, :, 0:LANES] = jnp.where(lane < G_B, csum, beta)

    bc = pltpu.roll(csum, LANES - (G_F - G_I), 1)
    backward_i = (lane & 7) >= 4
    u = capped - bc
    umax = _chunk_scan(u, pos, backward_i, jnp.maximum, NEG_BIG)
    o_ref[0, :, LANES:2 * LANES] = bc
    o_ref[0, :, 2 * LANES:3 * LANES] = u
    o_ref[0, :, 3 * LANES:4 * LANES] = bc + umax


def _gates_call(z_aux, alog_row, bias_row):
    b, t, _ = z_aux.shape
    tm = _pick_tile(t, 512)
    return pl.pallas_call(
        _gates_kernel,
        grid=(b, t // tm),
        in_specs=[
            pl.BlockSpec((1, tm, LANES), lambda bi, i: (bi, i, 0)),
            pl.BlockSpec((1, LANES), lambda bi, i: (0, 0)),
            pl.BlockSpec((1, LANES), lambda bi, i: (0, 0)),
        ],
        out_specs=pl.BlockSpec((1, tm, 4 * LANES), lambda bi, i: (bi, i, 0)),
        out_shape=jax.ShapeDtypeStruct((b, t, 4 * LANES), F32),
        compiler_params=_cparams("parallel", "parallel"),
        name="gates",
    )(z_aux, alog_row, bias_row)


def _conv_kernel(z_ref, zp_ref, zn_ref, w_ref, o_ref):
    i = pl.program_id(1)
    j = pl.program_id(2)
    z = z_ref[0]
    rows = z.shape[0]
    row = lax.broadcasted_iota(jnp.int32, z.shape, 0)
    prev_row = jnp.where(i == 0, 0.0, zp_ref[0, 7:8, :])
    next_row = jnp.where(i == pl.num_programs(1) - 1, 0.0, zn_ref[0, 0:1, :])
    z_prev = jnp.where(row == 0, prev_row, pltpu.roll(z, 1, 0))
    z_next = jnp.where(row == rows - 1, next_row, pltpu.roll(z, rows - 1, 0))
    y = z_prev * w_ref[0:1, :] + z * w_ref[1:2, :] + z_next * w_ref[2:3, :]
    y = y * _sigmoid(y)
    is_qk = j < 2
    for h in range(GDN_HEADS):
        cols = slice(h * GDN_DK, (h + 1) * GDN_DK)
        yh = y[:, cols]
        inv = lax.rsqrt(jnp.sum(yh * yh, axis=-1, keepdims=True) + EPS)
        o_ref[0, :, cols] = (yh * jnp.where(is_qk, inv, 1.0)).astype(o_ref.dtype)


def _conv_call(z, conv_w):
    b, t, _ = z.shape
    tt = _pick_tile(t, 512)
    g = GDN_QK
    nb8 = tt // 8
    return pl.pallas_call(
        _conv_kernel,
        grid=(b, t // tt, 3),
        in_specs=[
            pl.BlockSpec((1, tt, g), lambda bi, i, j: (bi, i, j)),
            pl.BlockSpec((1, 8, g), lambda bi, i, j: (bi, jnp.maximum(i * nb8 - 1, 0), j)),
            pl.BlockSpec((1, 8, g), lambda bi, i, j: (bi, jnp.minimum((i + 1) * nb8, t // 8 - 1), j)),
            pl.BlockSpec((3, g), lambda bi, i, j: (0, j)),
        ],
        out_specs=pl.BlockSpec((1, tt, g), lambda bi, i, j: (bi, i, j)),
        out_shape=jax.ShapeDtypeStruct((b, t, 3 * g), BF16),
        compiler_params=_cparams("parallel", "parallel", "parallel"),
        name="gdn_conv",
    )(z, z, z, conv_w)


def _split_bf16(a):
    hi = a.astype(BF16)
    lo = (a - hi.astype(F32)).astype(BF16)
    return hi, lo


def _dot_hp(a, b):
    ah, al = _split_bf16(a)
    bh, bl = _split_bf16(b)
    return _dot(ah, bh) + (_dot(ah, bl) + _dot(al, bh))


def _unit_tri_inverse(a, eye):
    x = -a
    p = eye + x
    n = 2
    while n < CHUNK:
        x = _dot_hp(x, x)
        p = p + _dot_hp(p, x)
        n *= 2
    return p


def _tri_masks(reverse):
    ii = lax.broadcasted_iota(jnp.int32, (CHUNK, CHUNK), 0)
    jj = lax.broadcasted_iota(jnp.int32, (CHUNK, CHUNK), 1)
    if reverse:
        return jj > ii, jj >= ii, jj == ii
    return jj < ii, jj <= ii, jj == ii


def _gdn_chunk(q, k, v, gcol, bcol, grow, brow, s, reverse, with_out):
    strict, incl, diag = _tri_masks(reverse)
    decay = jnp.exp(jnp.where(incl, gcol - grow, NEG_BIG))
    kk = _dot_nt(k, k)
    a = jnp.where(strict, kk * bcol * decay, 0.0)
    t = _unit_tri_inverse(a, diag.astype(F32))
    u = _dot((t * brow).astype(BF16), v)
    w = _dot((t * (brow * jnp.exp(grow))).astype(BF16), k)
    sb = s.astype(BF16)
    v_new = u - _dot(w.astype(BF16), sb)
    g_end = gcol[0:1] if reverse else gcol[CHUNK - 1:CHUNK]
    s_next = s * jnp.exp(g_end) + _dot_tn(k, (v_new * jnp.exp(g_end - gcol)).astype(BF16))
    if not with_out:
        return s_next, None
    scale = GDN_DK ** -0.5
    attn = _dot_nt(q, k) * decay * scale
    o = (jnp.exp(gcol) * scale) * _dot(q, sb) + _dot(attn.astype(BF16), v_new.astype(BF16))
    return s_next, o


def _gdn_kernel(*refs, hg, cs, with_out):
    (qf, kf, vf, qb, kb, vb, gcf, gcb, grf, grb, s0_ref) = refs[:11]
    if with_out:
        of_ref, ob_ref, sfin_ref = refs[11:]
    else:
        (sfin_ref,) = refs[11:]
        of_ref = ob_ref = None

    @pl.when(pl.program_id(2) == 0)
    def _():
        sfin_ref[...] = s0_ref[...]

    for d, (q_ref, k_ref, v_ref, gc_ref, gr_ref, o_ref) in enumerate(
            ((qf, kf, vf, gcf, grf, of_ref), (qb, kb, vb, gcb, grb, ob_ref))):
        for hh in range(hg):
            cols = slice(hh * GDN_DK, (hh + 1) * GDN_DK)
            lg = d * hg + hh
            lb = 2 * hg + d * hg + hh
            s = sfin_ref[0, d, hh]
            for ci in range(cs):
                c = cs - 1 - ci if d else ci
                rows = slice(c * CHUNK, (c + 1) * CHUNK)
                s, o = _gdn_chunk(
                    q_ref[0, rows, cols], k_ref[0, rows, cols], v_ref[0, rows, cols],
                    gc_ref[0, 0, rows, lg:lg + 1], gc_ref[0, 0, rows, lb:lb + 1],
                    gr_ref[0, 0, c, lg:lg + 1, :], gr_ref[0, 0, c, lb:lb + 1, :],
                    s, bool(d), with_out)
                if with_out:
                    o_ref[0, rows, cols] = o
            sfin_ref[0, d, hh] = s


def _gdn_call(qkv, gcol, grow, s0, hg, cs, with_out):
    b, t, _ = qkv.shape
    ng = GDN_HEADS // hg
    blk = cs * CHUNK
    nb = t // blk
    r = grow.shape[3]
    w = hg * GDN_DK

    def fwd(off):
        return lambda bi, gi, n: (bi, n, off * ng + gi)

    def bwd(off):
        return lambda bi, gi, n: (bi, nb - 1 - n, off * ng + gi)

    in_specs = (
        [pl.BlockSpec((1, blk, w), fwd(o)) for o in range(3)]
        + [pl.BlockSpec((1, blk, w), bwd(o)) for o in range(3)]
        + [pl.BlockSpec((1, 1, blk, LANES), lambda bi, gi, n: (bi, gi, n, 0)),
           pl.BlockSpec((1, 1, blk, LANES), lambda bi, gi, n: (bi, gi, nb - 1 - n, 0)),
           pl.BlockSpec((1, 1, cs, r, CHUNK), lambda bi, gi, n: (bi, gi, n, 0, 0)),
           pl.BlockSpec((1, 1, cs, r, CHUNK), lambda bi, gi, n: (bi, gi, nb - 1 - n, 0, 0)),
           pl.BlockSpec((1, 2, hg, GDN_DK, GDN_DV), lambda bi, gi, n: (bi, 0, gi, 0, 0))])
    s_spec = pl.BlockSpec((1, 2, hg, GDN_DK, GDN_DV), lambda bi, gi, n: (bi, 0, gi, 0, 0))
    s_shape = jax.ShapeDtypeStruct((b, 2, GDN_HEADS, GDN_DK, GDN_DV), F32)
    if with_out:
        out_specs = [pl.BlockSpec((1, blk, w), lambda bi, gi, n: (bi, n, gi)),
                     pl.BlockSpec((1, blk, w), lambda bi, gi, n: (bi, nb - 1 - n, gi)), s_spec]
        o_shape = jax.ShapeDtypeStruct((b, t, GDN_V), F32)
        out_shape = [o_shape, o_shape, s_shape]
    else:
        out_specs = [s_spec]
        out_shape = [s_shape]
    return pl.pallas_call(
        functools.partial(_gdn_kernel, hg=hg, cs=cs, with_out=with_out),
        grid=(b, ng, nb),
        in_specs=in_specs,
        out_specs=out_specs,
        out_shape=out_shape,
        compiler_params=_cparams("parallel", "parallel", "arbitrary"),
        name="gdn_scan_out" if with_out else "gdn_scan_state",
    )(qkv, qkv, qkv, qkv, qkv, qkv, gcol, gcol, grow, grow, s0)


def _ml_chunk(q, k, v, bcol, ucol, dmcol, urow, c_st, n_st, m, reverse, with_out):
    end = 0 if reverse else CHUNK - 1
    total = bcol[end:end + 1]
    umax = dmcol[end:end + 1] - total
    m_new = jnp.maximum(total + m, total + umax)
    wt = jnp.exp(total + ucol - m_new)
    dec = jnp.exp(total + m - m_new)
    c_next = dec * c_st + _dot_tn(k, (v * wt).astype(BF16))
    n_next = dec * n_st + jnp.sum(k.astype(F32) * wt, axis=0, keepdims=True)
    if not with_out:
        return c_next, n_next, m_new, None
    _, incl, _ = _tri_masks(reverse)
    scale = ML_DK ** -0.5
    m_t = jnp.maximum(bcol + m, dmcol)
    inter = jnp.exp(bcol + m - m_t)
    p = _dot_nt(q, k) * scale * jnp.exp(jnp.where(incl, bcol + urow - m_t, NEG_BIG))
    num = (inter * scale) * _dot(q, c_st.astype(BF16)) + _dot(p.astype(BF16), v.astype(BF16))
    qn = jnp.sum(q.astype(F32) * n_st, axis=1, keepdims=True) * scale
    den = inter * qn + jnp.sum(p, axis=1, keepdims=True)
    out = num / jnp.maximum(jnp.abs(den), jnp.exp(-m_t))
    return c_next, n_next, m_new, out


def _ml_kernel(*refs, hg, cs, with_out):
    (qf, kf, vf, qb, kb, vb, gcf, gcb, grf, grb, c0_ref, n0_ref, m0_ref) = refs[:13]
    if with_out:
        of_ref, ob_ref, cfin_ref, nfin_ref, mfin_ref = refs[13:]
    else:
        cfin_ref, nfin_ref, mfin_ref = refs[13:]
        of_ref = ob_ref = None

    @pl.when(pl.program_id(2) == 0)
    def _():
        cfin_ref[...] = c0_ref[...]
        nfin_ref[...] = n0_ref[...]
        mfin_ref[...] = m0_ref[...]

    for d, (q_ref, k_ref, v_ref, gc_ref, gr_ref, o_ref) in enumerate(
            ((qf, kf, vf, gcf, grf, of_ref), (qb, kb, vb, gcb, grb, ob_ref))):
        for hh in range(hg):
            qcols = slice(hh * ML_DK, (hh + 1) * ML_DK)
            vcols = slice(hh * ML_DV, (hh + 1) * ML_DV)
            lb = d * hg + hh
            lu = 2 * hg + d * hg + hh
            ld = 4 * hg + d * hg + hh
            c_st = cfin_ref[0, d, hh]
            n_st = nfin_ref[0, d, hh, 0:1, :]
            m = mfin_ref[0, d, hh, 0:1, 0:1]
            for ci in range(cs):
                c = cs - 1 - ci if d else ci
                rows = slice(c * CHUNK, (c + 1) * CHUNK)
                c_st, n_st, m, o = _ml_chunk(
                    q_ref[0, rows, qcols].astype(BF16), k_ref[0, rows, qcols].astype(BF16),
                    v_ref[0, rows, vcols],
                    gc_ref[0, 0, rows, lb:lb + 1], gc_ref[0, 0, rows, lu:lu + 1],
                    gc_ref[0, 0, rows, ld:ld + 1], gr_ref[0, 0, c, lu:lu + 1, :],
                    c_st, n_st, m, bool(d), with_out)
                if with_out:
                    o_ref[0, rows, vcols] = o
            cfin_ref[0, d, hh] = c_st
            nfin_ref[0, d, hh] = jnp.broadcast_to(n_st, (8, ML_DK))
            mfin_ref[0, d, hh] = jnp.broadcast_to(m, (8, LANES))


def _ml_call(z, col0, gcol, grow, c0, n0, m0, hg, cs, with_out):
    b, t, _ = z.shape
    ng = ML_HEADS // hg
    blk = cs * CHUNK
    nb = t // blk
    r = grow.shape[3]
    wq = hg * ML_DK
    wv = hg * ML_DV
    qoff = col0 * LANES // wq
    koff = (col0 * LANES + ML_QK) // wq
    voff = (col0 * LANES + 2 * ML_QK) // wv

    def spec(width, off, rev):
        if rev:
            return pl.BlockSpec((1, blk, width), lambda bi, gi, n: (bi, nb - 1 - n, off + gi))
        return pl.BlockSpec((1, blk, width), lambda bi, gi, n: (bi, n, off + gi))

    def state_spec(*tail):
        zeros = (0,) * len(tail)
        return pl.BlockSpec((1, 2, hg) + tail, lambda bi, gi, n: (bi, 0, gi) + zeros)

    in_specs = (
        [spec(wq, qoff, False), spec(wq, koff, False), spec(wv, voff, False),
         spec(wq, qoff, True), spec(wq, koff, True), spec(wv, voff, True),
         pl.BlockSpec((1, 1, blk, LANES), lambda bi, gi, n: (bi, gi, n, 0)),
         pl.BlockSpec((1, 1, blk, LANES), lambda bi, gi, n: (bi, gi, nb - 1 - n, 0)),
         pl.BlockSpec((1, 1, cs, r, CHUNK), lambda bi, gi, n: (bi, gi, n, 0, 0)),
         pl.BlockSpec((1, 1, cs, r, CHUNK), lambda bi, gi, n: (bi, gi, nb - 1 - n, 0, 0)),
         state_spec(ML_DK, ML_DV), state_spec(8, ML_DK), state_spec(8, LANES)])
    st_specs = [state_spec(ML_DK, ML_DV), state_spec(8, ML_DK), state_spec(8, LANES)]
    st_shapes = [jax.ShapeDtypeStruct((b, 2, ML_HEADS, ML_DK, ML_DV), F32),
                 jax.ShapeDtypeStruct((b, 2, ML_HEADS, 8, ML_DK), F32),
                 jax.ShapeDtypeStruct((b, 2, ML_HEADS, 8, LANES), F32)]
    if with_out:
        out_specs = [pl.BlockSpec((1, blk, wv), lambda bi, gi, n: (bi, n, gi)),
                     pl.BlockSpec((1, blk, wv), lambda bi, gi, n: (bi, nb - 1 - n, gi))] + st_specs
        o_shape = jax.ShapeDtypeStruct((b, t, ML_V), F32)
        out_shape = [o_shape, o_shape] + st_shapes
    else:
        out_specs = st_specs
        out_shape = st_shapes
    return pl.pallas_call(
        functools.partial(_ml_kernel, hg=hg, cs=cs, with_out=with_out),
        grid=(b, ng, nb),
        in_specs=in_specs,
        out_specs=out_specs,
        out_shape=out_shape,
        compiler_params=_cparams("parallel", "parallel", "arbitrary"),
        name="mlstm_scan_out" if with_out else "mlstm_scan_state",
    )(z, z, z, z, z, z, gcol, gcol, grow, grow, c0, n0, m0)


def _head_rms(x, width):
    outs = []
    for h in range(x.shape[1] // width):
        xh = x[:, h * width:(h + 1) * width]
        ms = jnp.sum(xh * xh, axis=-1, keepdims=True) * (1.0 / width)
        outs.append(xh * lax.rsqrt(ms + EPS))
    return jnp.concatenate(outs, axis=-1)


def _merge_kernel(x_ref, of_ref, ob_ref, hf_ref, hb_ref, gz_ref, mo_ref, gg_ref, gm_ref,
                  gnw_ref, mnw_ref, wg_ref, wm_ref, wo_ref, gate_ref, o_ref):
    og = _head_rms(of_ref[0] + ob_ref[0], GDN_DV) * gnw_ref[...]
    gz = gz_ref[0]
    og = og * (gz * _sigmoid(gz))
    y_gdn = _dot(og.astype(BF16), wg_ref[...])
    hm = _head_rms(hf_ref[0] + hb_ref[0], ML_DV) * mnw_ref[...]
    hm = hm * _sigmoid(mo_ref[0])
    y_ml = _dot(hm.astype(BF16), wm_ref[...])
    merged = _sigmoid(gg_ref[0]) * y_gdn + _sigmoid(gm_ref[0]) * y_ml
    x_mix = _dot(merged.astype(BF16), wo_ref[...])
    o_ref[0] = x_ref[0] + gate_ref[0] * x_mix


def _merge_call(x, o_f, o_b, h_f, h_b, z, zo_col0, gnw, mnw, wg, wm, wo, gate):
    b, t, d = x.shape
    tm = _pick_tile(t, 256)
    nz = zo_col0 * LANES // d

    def tok(bi, i):
        return (bi, i, 0)

    def zcol(k):
        return lambda bi, i: (bi, i, nz + k)

    tile = lambda imap: pl.BlockSpec((1, tm, d), imap)
    full = lambda shape: pl.BlockSpec(shape, lambda bi, i: (0,) * len(shape))
    return pl.pallas_call(
        _merge_kernel,
        grid=(b, t // tm),
        in_specs=[tile(tok), tile(tok), tile(tok), tile(tok), tile(tok),
                  tile(zcol(0)), tile(zcol(1)), tile(zcol(2)), tile(zcol(3)),
                  full((1, d)), full((1, d)), full((d, d)), full((d, d)), full((d, d)),
                  pl.BlockSpec((1, 1, d), lambda bi, i: (bi, 0, 0))],
        out_specs=tile(tok),
        out_shape=jax.ShapeDtypeStruct((b, t, d), F32),
        compiler_params=_cparams("parallel", "parallel"),
        name="merge",
    )(x, o_f, o_b, h_f, h_b, z, z, z, z, gnw, mnw, wg, wm, wo, gate)


def _ffn2_kernel(ug_ref, ugp_ref, ugn_ref, uv_ref, uvp_ref, uvn_ref, cwg_ref, cwv_ref, wd_ref,
                 x_ref, gate_ref, nw_ref, o_ref, acc_ref):
    i = pl.program_id(1)
    j = pl.program_id(2)
    first = i == 0
    last = i == pl.num_programs(1) - 1

    def grid_conv(u_ref, up_ref, un_ref, cw_ref):
        u = u_ref[0]
        rows = u.shape[0]
        ext = jnp.concatenate([jnp.where(first, 0.0, up_ref[0]), u, jnp.where(last, 0.0, un_ref[0])], axis=0)
        col = lax.broadcasted_iota(jnp.int32, u.shape, 0) & (GRID_W - 1)
        taps = [ext[r * GRID_W:r * GRID_W + rows] for r in range(3)]
        side = []
        for dc in range(3):
            side.append(sum(taps[r] * cw_ref[3 * r + dc:3 * r + dc + 1, :] for r in range(3)))
        left = jnp.where(col == 0, 0.0, pltpu.roll(side[0], 1, 0))
        right = jnp.where(col == GRID_W - 1, 0.0, pltpu.roll(side[2], rows - 1, 0))
        return side[1] + left + right

    g = grid_conv(ug_ref, ugp_ref, ugn_ref, cwg_ref)
    v = grid_conv(uv_ref, uvp_ref, uvn_ref, cwv_ref)
    act = (g * _sigmoid(g) * v).astype(BF16)
    part = _dot(act, wd_ref[...])

    @pl.when(j == 0)
    def _():
        acc_ref[...] = part

    @pl.when(j > 0)
    def _():
        acc_ref[...] += part

    @pl.when(j == pl.num_programs(2) - 1)
    def _():
        x = x_ref[0] + gate_ref[0] * acc_ref[...]
        ms = jnp.mean(x * x, axis=-1, keepdims=True)
        o_ref[0] = x * lax.rsqrt(ms + EPS) * nw_ref[...]


def _ffn2_call(u, conv_w, w_down, x, gate, nw):
    b, t, d = x.shape
    f = w_down.shape[0]
    tm = _pick_tile(t, 512)
    tc = 256
    nj = f // tc
    rpt = tm // GRID_W
    nrows = t // GRID_W

    def main(off):
        return pl.BlockSpec((1, tm, tc), lambda bi, i, j: (bi, i, off + j))

    def prev(off):
        return pl.BlockSpec((1, GRID_W, tc), lambda bi, i, j: (bi, jnp.maximum(i * rpt - 1, 0), off + j))

    def nxt(off):
        return pl.BlockSpec((1, GRID_W, tc), lambda bi, i, j: (bi, jnp.minimum((i + 1) * rpt, nrows - 1), off + j))

    return pl.pallas_call(
        _ffn2_kernel,
        grid=(b, t // tm, nj),
        in_specs=[main(0), prev(0), nxt(0), main(nj), prev(nj), nxt(nj),
                  pl.BlockSpec((9, tc), lambda bi, i, j: (0, j)),
                  pl.BlockSpec((9, tc), lambda bi, i, j: (0, nj + j)),
                  pl.BlockSpec((tc, d), lambda bi, i, j: (j, 0)),
                  pl.BlockSpec((1, tm, d), lambda bi, i, j: (bi, i, 0)),
                  pl.BlockSpec((1, 1, d), lambda bi, i, j: (bi, 0, 0)),
                  pl.BlockSpec((1, d), lambda bi, i, j: (0, 0))],
        out_specs=pl.BlockSpec((1, tm, d), lambda bi, i, j: (bi, i, 0)),
        out_shape=jax.ShapeDtypeStruct((b, t, d), F32),
        scratch_shapes=[pltpu.VMEM((tm, d), F32)],
        compiler_params=_cparams("parallel", "parallel", "arbitrary"),
        name="ffn2",
    )(u, u, u, u, u, u, conv_w, conv_w, w_down, x, gate, nw)


def _gate_layouts(gates, hg_gdn, hg_ml):
    b, t, _ = gates.shape
    nc = t // CHUNK

    def regroup(cols, heads, hg):
        ng = heads // hg
        parts = []
        for off in cols:
            q = gates[:, :, off:off + 2 * heads].reshape(b, t, 2, ng, hg)
            parts.append(q.transpose(0, 3, 1, 2, 4).reshape(b, ng, t, 2 * hg))
        col = jnp.concatenate(parts, axis=-1)
        used = col.shape[-1]
        r = -(-used // 8) * 8
        colp = jnp.pad(col, ((0, 0), (0, 0), (0, 0), (0, LANES - used)))
        row = jnp.pad(col, ((0, 0), (0, 0), (0, 0), (0, r - used)))
        row = row.reshape(b, ng, nc, CHUNK, r).transpose(0, 1, 2, 4, 3)
        return colp, row

    gdn = regroup([G_A, G_B], GDN_HEADS, hg_gdn)
    ml = regroup([LANES + G_I, 2 * LANES + G_I, 3 * LANES + G_I], ML_HEADS, hg_ml)
    return gdn, ml


def _mixer_states(x_seq, nw, shift, scale, w_state, w_aux, conv_w, alog_row, bias_row, states,
                  with_out, w_full=None):
    hg_gdn, hg_ml, cs = 1, 1, 2
    w = w_full if with_out else w_state
    z, z_aux = _nmm_call(x_seq, nw, shift, scale, w, w_aux, name="in_proj")
    gates = _gates_call(z_aux, alog_row, bias_row)
    (g_col, g_row), (m_col, m_row) = _gate_layouts(gates, hg_gdn, hg_ml)
    qkv = _conv_call(z, conv_w)
    s_gdn, c_ml, n_ml, m_ml = states
    cs = min(cs, x_seq.shape[1] // CHUNK)
    gdn_res = _gdn_call(qkv, g_col, g_row, s_gdn, hg_gdn, cs, with_out)
    ml_col0 = (2 * GDN_QK + GDN_V) // LANES
    ml_res = _ml_call(z, ml_col0, m_col, m_row, c_ml, n_ml, m_ml, hg_ml, cs, with_out)
    return z, gdn_res, ml_res


def kernel(x, c, ctx, c_ctx, w_ada, b_ada, norm1_w, w_in, gdn_conv, gdn_a_log, gdn_dt_bias, gdn_norm_w,
           ml_igate_b, ml_fgate_b, ml_norm_w, w_branch_gdn, w_branch_ml, w_out, norm2_w, w_up, ffn_conv,
           w_down, norm_out_w):
    bsz, _, d = x.shape
    depth = w_ada.shape[0]
    assert depth == 1, "single-layer problem: the context stream is never updated"
    l = 0

    sizes = (2 * GDN_QK + GDN_V, 2 * GDN_HEADS, 2 * GDN_HEADS, ML_QK, ML_QK, ML_V, 2 * ML_HEADS, 2 * ML_HEADS,
             GDN_V, ML_V, d, d)
    offs = [0]
    for s in sizes:
        offs.append(offs[-1] + s)
    wi = w_in[l]
    seg = lambda k: wi[:, offs[k]:offs[k + 1]]
    w_state = jnp.concatenate([seg(0), seg(3), seg(4), seg(5)], axis=1).astype(BF16)
    w_full = jnp.concatenate([seg(0), seg(3), seg(4), seg(5), seg(8), seg(9), seg(10), seg(11)], axis=1).astype(BF16)
    n_gate = 4 * GDN_HEADS + 4 * ML_HEADS
    w_aux = jnp.concatenate([seg(1), seg(2), seg(6), seg(7), jnp.zeros((d, LANES - n_gate), F32)], axis=1).astype(BF16)
    pad = lambda v, n: jnp.pad(v.reshape(1, -1).astype(F32), ((0, 0), (0, n - v.size)))
    alog_row = pad(gdn_a_log[l], LANES)
    bias_row = pad(jnp.concatenate([gdn_dt_bias[l].reshape(-1), jnp.zeros((2 * GDN_HEADS,), F32),
                                    ml_igate_b[l].reshape(-1), ml_fgate_b[l].reshape(-1)]), LANES)
    row = lambda v: v.reshape(1, -1).astype(F32)

    c_all = jnp.concatenate([c, c_ctx[None], jnp.zeros((8 - bsz - 1, d), F32)], axis=0)
    mods = _mod_call(c_all, w_ada[l].astype(BF16), row(b_ada[l]))
    mod_x = mods[:bsz].reshape(bsz, N_MOD, 1, d)
    mod_c = jnp.broadcast_to(mods[bsz].reshape(1, N_MOD, 1, d), (bsz, N_MOD, 1, d))

    zero_states = (jnp.zeros((bsz, 2, GDN_HEADS, GDN_DK, GDN_DV), F32),
                   jnp.zeros((bsz, 2, ML_HEADS, ML_DK, ML_DV), F32),
                   jnp.zeros((bsz, 2, ML_HEADS, 8, ML_DK), F32),
                   jnp.zeros((bsz, 2, ML_HEADS, 8, LANES), F32))
    common = (w_state, w_aux, gdn_conv[l].astype(F32), alog_row, bias_row)

    _, (s_gdn,), (c_ml, n_ml, m_ml) = _mixer_states(
        ctx, row(norm1_w[l]), mod_c[:, 0], mod_c[:, 1], *common, zero_states, False)

    z, (o_f, o_b, _), (h_f, h_b, _, _, _) = _mixer_states(
        x, row(norm1_w[l]), mod_x[:, 0], mod_x[:, 1], *common, (s_gdn, c_ml, n_ml, m_ml), True, w_full)
    x1 = _merge_call(x, o_f, o_b, h_f, h_b, z, STATE_COLS // LANES,
                     row(jnp.tile(gdn_norm_w[l], GDN_HEADS)), row(ml_norm_w[l]),
                     w_branch_gdn[l].astype(BF16), w_branch_ml[l].astype(BF16), w_out[l].astype(BF16),
                     mod_x[:, 2])
    u = _nmm_call(x1, row(norm2_w[l]), mod_x[:, 3], mod_x[:, 4], w_up[l].astype(BF16), name="ffn_up")
    return _ffn2_call(u, ffn_conv[l].reshape(9, -1).astype(F32), w_down[l].astype(BF16), x1,
                      mod_x[:, 5], row(norm_out_w))
```

```python
import functools

import jax
import jax.numpy as jnp
from jax import lax
from jax.experimental import pallas as pl
from jax.experimental.pallas import tpu as pltpu

F32 = jnp.float32
BF16 = jnp.bfloat16

GDN_HEADS = 8
GDN_DK = 128
GDN_DV = 128
ML_HEADS = 4
ML_DK = 128
ML_DV = 256
CHUNK = 64
GATE_CAP = 15.0
GRID_W = 64
N_MOD = 6
EPS = 1e-6
LANES = 128
NEG_BIG = -1e30

GDN_QK = GDN_HEADS * GDN_DK
GDN_V = GDN_HEADS * GDN_DV
ML_QK = ML_HEADS * ML_DK
ML_V = ML_HEADS * ML_DV
STATE_COLS = 2 * GDN_QK + GDN_V + 2 * ML_QK + ML_V
OUT_COLS = GDN_V + ML_V + 2 * 1024

VMEM_LIMIT = 48 * 1024 * 1024


def _cparams(*sem):
    return pltpu.CompilerParams(dimension_semantics=sem, vmem_limit_bytes=VMEM_LIMIT)


def _dot(a, b):
    return jnp.dot(a, b, preferred_element_type=F32)


def _dot_nt(a, b):
    return lax.dot_general(a, b, (((1,), (1,)), ((), ())), preferred_element_type=F32)


def _dot_tn(a, b):
    return lax.dot_general(a, b, (((0,), (0,)), ((), ())), preferred_element_type=F32)


def _sigmoid(x):
    return 1.0 / (1.0 + jnp.exp(-x))


def _softplus(x):
    return jnp.maximum(x, 0.0) + jnp.log1p(jnp.exp(-jnp.abs(x)))


def _pick_tile(n, pref):
    t = min(n, pref)
    while n % t:
        t //= 2
    return t


def _mod_kernel(c_ref, w_ref, b_ref, o_ref):
    c = c_ref[...]
    s = c * _sigmoid(c)
    o_ref[...] = _dot(s.astype(BF16), w_ref[...]) + b_ref[...]


def _mod_call(c_all, w_ada, b_ada):
    rows, d = c_all.shape
    n = w_ada.shape[1]
    tn = 1024
    return pl.pallas_call(
        _mod_kernel,
        grid=(n // tn,),
        in_specs=[
            pl.BlockSpec((rows, d), lambda j: (0, 0)),
            pl.BlockSpec((d, tn), lambda j: (0, j)),
            pl.BlockSpec((1, tn), lambda j: (0, j)),
        ],
        out_specs=pl.BlockSpec((rows, tn), lambda j: (0, j)),
        out_shape=jax.ShapeDtypeStruct((rows, n), F32),
        compiler_params=_cparams("parallel"),
        name="mod",
    )(c_all, w_ada, b_ada)


def _nmm_kernel(x_ref, nw_ref, sh_ref, sc_ref, w_ref, *rest, has_aux):
    if has_aux:
        wa_ref, o_ref, oa_ref, hn_ref = rest
    else:
        o_ref, hn_ref = rest

    @pl.when(pl.program_id(2) == 0)
    def _():
        x = x_ref[0]
        ms = jnp.mean(x * x, axis=-1, keepdims=True)
        y = x * lax.rsqrt(ms + EPS) * nw_ref[...]
        hb = (y * (1.0 + sc_ref[0]) + sh_ref[0]).astype(BF16)
        hn_ref[...] = hb
        if has_aux:
            oa_ref[0] = _dot(hb, wa_ref[...])

    o_ref[0] = _dot(hn_ref[...], w_ref[...]).astype(o_ref.dtype)


def _nmm_call(x, nw, shift, scale, w, w_aux=None, out_dtype=F32, name="nmm"):
    b, t, d = x.shape
    c = w.shape[1]
    tm = _pick_tile(t, 1024)
    tn = _pick_tile(c, 512)
    has_aux = w_aux is not None
    in_specs = [
        pl.BlockSpec((1, tm, d), lambda bi, i, j: (bi, i, 0)),
        pl.BlockSpec((1, d), lambda bi, i, j: (0, 0)),
        pl.BlockSpec((1, 1, d), lambda bi, i, j: (bi, 0, 0)),
        pl.BlockSpec((1, 1, d), lambda bi, i, j: (bi, 0, 0)),
        pl.BlockSpec((d, tn), lambda bi, i, j: (0, j)),
    ]
    out_specs = [pl.BlockSpec((1, tm, tn), lambda bi, i, j: (bi, i, j))]
    out_shape = [jax.ShapeDtypeStruct((b, t, c), out_dtype)]
    args = [x, nw, shift, scale, w]
    if has_aux:
        in_specs.append(pl.BlockSpec((d, LANES), lambda bi, i, j: (0, 0)))
        out_specs.append(pl.BlockSpec((1, tm, LANES), lambda bi, i, j: (bi, i, 0)))
        out_shape.append(jax.ShapeDtypeStruct((b, t, LANES), F32))
        args.append(w_aux)
    res = pl.pallas_call(
        functools.partial(_nmm_kernel, has_aux=has_aux),
        grid=(b, t // tm, c // tn),
        in_specs=in_specs,
        out_specs=out_specs,
        out_shape=out_shape,
        scratch_shapes=[pltpu.VMEM((tm, d), BF16)],
        compiler_params=_cparams("parallel", "parallel", "arbitrary"),
        name=name,
    )(*args)
    return res if has_aux else res[0]


G_A, G_B, G_I, G_F = 0, 16, 32, 40


def _chunk_scan(x, pos, backward, op, ident):
    rows = x.shape[0]
    yf, yb = x, x
    s = 1
    while s < CHUNK:
        yf = op(yf, jnp.where(pos >= s, pltpu.roll(yf, s, 0), ident))
        yb = op(yb, jnp.where(pos + s < CHUNK, pltpu.roll(yb, rows - s, 0), ident))
        s *= 2
    return jnp.where(backward, yb, yf)


def _gates_kernel(z_ref, alog_ref, bias_ref, o_ref):
    z = z_ref[0] + bias_ref[...]
    lane = lax.broadcasted_iota(jnp.int32, z.shape, 1)
    pos = lax.broadcasted_iota(jnp.int32, z.shape, 0) & (CHUNK - 1)
    backward = ((lane < G_I) & ((lane & 15) >= 8)) | ((lane >= G_I) & ((lane & 7) >= 4))

    log_decay = -jnp.exp(alog_ref[...]) * _softplus(z)
    beta = _sigmoid(z)
    capped = GATE_CAP * jnp.tanh(z * (1.0 / GATE_CAP))
    log_forget = -_softplus(-capped)

    summand = jnp.where(lane < G_B, log_decay, jnp.where(lane >= G_F, log_forget, 0.0))
    csum = _chunk_scan(summand, pos, backward, jnp.add, 0.0)
    o_ref[0, :, 0:LANES] = jnp.where(lane < G_B, csum, beta)

    bc = pltpu.roll(csum, LANES - (G_F - G_I), 1)
    backward_i = (lane & 7) >= 4
    u = capped - bc
    umax = _chunk_scan(u, pos, backward_i, jnp.maximum, NEG_BIG)
    o_ref[0, :, LANES:2 * LANES] = bc
    o_ref[0, :, 2 * LANES:3 * LANES] = u
    o_ref[0, :, 3 * LANES:4 * LANES] = bc + umax


def _gates_call(z_aux, alog_row, bias_row):
    b, t, _ = z_aux.shape
    tm = _pick_tile(t, 512)
    return pl.pallas_call(
        _gates_kernel,
        grid=(b, t // tm),
        in_specs=[
            pl.BlockSpec((1, tm, LANES), lambda bi, i: (bi, i, 0)),
            pl.BlockSpec((1, LANES), lambda bi, i: (0, 0)),
            pl.BlockSpec((1, LANES), lambda bi, i: (0, 0)),
        ],
        out_specs=pl.BlockSpec((1, tm, 4 * LANES), lambda bi, i: (bi, i, 0)),
        out_shape=jax.ShapeDtypeStruct((b, t, 4 * LANES), F32),
        compiler_params=_cparams("parallel", "parallel"),
        name="gates",
    )(z_aux, alog_row, bias_row)


def _conv_kernel(z_ref, zp_ref, zn_ref, w_ref, o_ref):
    i = pl.program_id(1)
    j = pl.program_id(2)
    z = z_ref[0]
    rows = z.shape[0]
    row = lax.broadcasted_iota(jnp.int32, z.shape, 0)
    prev_row = jnp.where(i == 0, 0.0, zp_ref[0, 7:8, :])
    next_row = jnp.where(i == pl.num_programs(1) - 1, 0.0, zn_ref[0, 0:1, :])
    z_prev = jnp.where(row == 0, prev_row, pltpu.roll(z, 1, 0))
    z_next = jnp.where(row == rows - 1, next_row, pltpu.roll(z, rows - 1, 0))
    y = z_prev * w_ref[0:1, :] + z * w_ref[1:2, :] + z_next * w_ref[2:3, :]
    y = y * _sigmoid(y)
    is_qk = j < 2
    for h in range(GDN_HEADS):
        cols = slice(h * GDN_DK, (h + 1) * GDN_DK)
        yh = y[:, cols]
        inv = lax.rsqrt(jnp.sum(yh * yh, axis=-1, keepdims=True) + EPS)
        o_ref[0, :, cols] = (yh * jnp.where(is_qk, inv, 1.0)).astype(o_ref.dtype)


def _conv_call(z, conv_w):
    b, t, _ = z.shape
    tt = _pick_tile(t, 512)
    g = GDN_QK
    nb8 = tt // 8
    return pl.pallas_call(
        _conv_kernel,
        grid=(b, t // tt, 3),
        in_specs=[
            pl.BlockSpec((1, tt, g), lambda bi, i, j: (bi, i, j)),
            pl.BlockSpec((1, 8, g), lambda bi, i, j: (bi, jnp.maximum(i * nb8 - 1, 0), j)),
            pl.BlockSpec((1, 8, g), lambda bi, i, j: (bi, jnp.minimum((i + 1) * nb8, t // 8 - 1), j)),
            pl.BlockSpec((3, g), lambda bi, i, j: (0, j)),
        ],
        out_specs=pl.BlockSpec((1, tt, g), lambda bi, i, j: (bi, i, j)),
        out_shape=jax.ShapeDtypeStruct((b, t, 3 * g), BF16),
        compiler_params=_cparams("parallel", "parallel", "parallel"),
        name="gdn_conv",
    )(z, z, z, conv_w)


def _tri_masks(reverse):
    ii = lax.broadcasted_iota(jnp.int32, (CHUNK, CHUNK), 0)
    jj = lax.broadcasted_iota(jnp.int32, (CHUNK, CHUNK), 1)
    if reverse:
        return jj > ii, jj >= ii, jj == ii
    return jj < ii, jj <= ii, jj == ii


QUAD = 4
QW = QUAD * CHUNK
NQUAD = GDN_HEADS // QUAD


def _quad_masks(reverse):
    ii = lax.broadcasted_iota(jnp.int32, (QW, QW), 0)
    jj = lax.broadcasted_iota(jnp.int32, (QW, QW), 1)
    same = (ii ^ jj) < CHUNK
    if reverse:
        return same & (jj > ii), same & (jj >= ii), ii == jj
    return same & (jj < ii), same & (jj <= ii), ii == jj


def _gdn_kernel(*refs, with_out):
    (qf, kf, vf, qb, kb, vb, gcf, gcb, grf, grb, s0_ref) = refs[:11]
    if with_out:
        of_ref, ob_ref, sfin_ref = refs[11:]
    else:
        (sfin_ref,) = refs[11:]
        of_ref = ob_ref = None

    @pl.when(pl.program_id(1) == 0)
    def _():
        sfin_ref[...] = s0_ref[...]

    scale = GDN_DK ** -0.5
    groups = []
    for d, (q_ref, k_ref, v_ref, gc_ref, gr_ref, o_ref) in enumerate(
            ((qf, kf, vf, gcf, grf, of_ref), (qb, kb, vb, gcb, grb, ob_ref))):
        for g in range(NQUAD):
            heads = tuple(range(g * QUAD, (g + 1) * QUAD))

            def stack(ref):
                return jnp.concatenate([ref[0, :, h * GDN_DK:(h + 1) * GDN_DK] for h in heads], axis=0)

            def col(base):
                lanes = [base + d * GDN_HEADS + h for h in heads]
                return jnp.concatenate([gc_ref[0, :, l:l + 1] for l in lanes], axis=0)

            r = d * NQUAD + g
            groups.append(dict(
                d=d, g=g, heads=heads, o_ref=o_ref, k=stack(k_ref), v=stack(v_ref),
                q=stack(q_ref) if with_out else None, gcol=col(G_A), bcol=col(G_B),
                grow=gr_ref[0, 0, r:r + 1, :], brow=gr_ref[0, 0, 2 * NQUAD + r:2 * NQUAD + r + 1, :]))

    for grp in groups:
        strict, incl, diag = _quad_masks(bool(grp["d"]))
        decay = jnp.exp(jnp.where(incl, grp["gcol"] - grp["grow"], NEG_BIG))
        kk = _dot_nt(grp["k"], grp["k"])
        x = jnp.where(strict, kk * decay * (-grp["bcol"]), 0.0)
        grp["x"] = x
        grp["s"] = jnp.where(diag, 1.0, x)
        if with_out:
            grp["attn"] = (_dot_nt(grp["q"], grp["k"]) * decay * scale).astype(BF16)
    for grp in groups:
        xb = grp["x"].astype(BF16)
        grp["xm"] = _dot(xb, xb)
    m = 2
    while 2 * m < CHUNK:
        for grp in groups:
            xb = grp["xm"].astype(BF16)
            both = _dot(jnp.concatenate([grp["s"].astype(BF16), xb], axis=0), xb)
            grp["s"] = grp["s"] + both[:QW]
            grp["xm"] = both[QW:]
        m *= 2
    for grp in groups:
        t = grp["s"] + _dot(grp["s"].astype(BF16), grp["xm"].astype(BF16))
        brow, grow = grp["brow"], grp["grow"]
        grp["u"] = _dot((t * brow).astype(BF16), grp["v"])
        grp["w"] = _dot((t * (brow * jnp.exp(grow))).astype(BF16), grp["k"])

    row = lax.broadcasted_iota(jnp.int32, (QW, GDN_DV), 0)
    for grp in groups:
        d, g, gcol = grp["d"], grp["g"], grp["gcol"]
        s4 = sfin_ref[0, d, g]
        lhs = grp["w"].astype(BF16)
        if with_out:
            lhs = jnp.concatenate([lhs, grp["q"]], axis=0)
        ws = _dot(lhs, s4.astype(BF16))

        def diag_blocks(base):
            return jnp.concatenate(
                [ws[base + c * CHUNK:base + (c + 1) * CHUNK, c * GDN_DV:(c + 1) * GDN_DV] for c in range(QUAD)],
                axis=0)

        v_new = grp["u"] - diag_blocks(0)
        end = 0 if d else CHUNK - 1
        g_end = [gcol[c * CHUNK + end:c * CHUNK + end + 1] for c in range(QUAD)]
        g_end_col = jnp.concatenate([jnp.broadcast_to(ge, (CHUNK, 1)) for ge in g_end], axis=0)
        vt = v_new * jnp.exp(g_end_col - gcol)
        vbd = jnp.concatenate(
            [jnp.where((row >= c * CHUNK) & (row < (c + 1) * CHUNK), vt, 0.0).astype(BF16) for c in range(QUAD)],
            axis=1)
        decay_lane = jnp.concatenate([jnp.broadcast_to(jnp.exp(ge), (1, GDN_DV)) for ge in g_end], axis=1)
        sfin_ref[0, d, g] = s4 * decay_lane + _dot_tn(grp["k"], vbd)
        if with_out:
            o = (jnp.exp(gcol) * scale) * diag_blocks(QW) + _dot(grp["attn"], v_new.astype(BF16))
            for c, h in enumerate(grp["heads"]):
                grp["o_ref"][0, :, h * GDN_DV:(h + 1) * GDN_DV] = o[c * CHUNK:(c + 1) * CHUNK]


def _gdn_call(qkv, gcol, grow, s0, with_out):
    b, t, _ = qkv.shape
    nb = t // CHUNK
    w = GDN_QK

    def fwd(*tail):
        return lambda bi, n: (bi, n) + tail

    def bwd(*tail):
        return lambda bi, n: (bi, nb - 1 - n) + tail

    s_spec = pl.BlockSpec((1, 2, NQUAD, GDN_DK, QUAD * GDN_DV), lambda bi, n: (bi, 0, 0, 0, 0))
    in_specs = (
        [pl.BlockSpec((1, CHUNK, w), fwd(o)) for o in range(3)]
        + [pl.BlockSpec((1, CHUNK, w), bwd(o)) for o in range(3)]
        + [pl.BlockSpec((1, CHUNK, LANES), fwd(0)), pl.BlockSpec((1, CHUNK, LANES), bwd(0)),
           pl.BlockSpec((1, 1, 4 * NQUAD, QW), fwd(0, 0)), pl.BlockSpec((1, 1, 4 * NQUAD, QW), bwd(0, 0)),
           s_spec])
    s_shape = jax.ShapeDtypeStruct(s0.shape, F32)
    if with_out:
        out_specs = [pl.BlockSpec((1, CHUNK, w), fwd(0)), pl.BlockSpec((1, CHUNK, w), bwd(0)), s_spec]
        o_shape = jax.ShapeDtypeStruct((b, t, GDN_V), F32)
        out_shape = [o_shape, o_shape, s_shape]
    else:
        out_specs = [s_spec]
        out_shape = [s_shape]
    return pl.pallas_call(
        functools.partial(_gdn_kernel, with_out=with_out),
        grid=(b, nb),
        in_specs=in_specs,
        out_specs=out_specs,
        out_shape=out_shape,
        compiler_params=_cparams("parallel", "arbitrary"),
        name="gdn_scan_out" if with_out else "gdn_scan_state",
    )(qkv, qkv, qkv, qkv, qkv, qkv, gcol, gcol, grow, grow, s0)


def _ml_chunk(q, k, v, bcol, ucol, dmcol, urow, c_st, n_st, m, reverse, with_out):
    end = 0 if reverse else CHUNK - 1
    total = bcol[end:end + 1]
    umax = dmcol[end:end + 1] - total
    m_new = jnp.maximum(total + m, total + umax)
    wt = jnp.exp(total + ucol - m_new)
    dec = jnp.exp(total + m - m_new)
    c_next = dec * c_st + _dot_tn(k, (v * wt).astype(BF16))
    n_next = dec * n_st + jnp.sum(k.astype(F32) * wt, axis=0, keepdims=True)
    if not with_out:
        return c_next, n_next, m_new, None
    _, incl, _ = _tri_masks(reverse)
    scale = ML_DK ** -0.5
    m_t = jnp.maximum(bcol + m, dmcol)
    inter = jnp.exp(bcol + m - m_t)
    p = _dot_nt(q, k) * scale * jnp.exp(jnp.where(incl, bcol + urow - m_t, NEG_BIG))
    num = (inter * scale) * _dot(q, c_st.astype(BF16)) + _dot(p.astype(BF16), v.astype(BF16))
    qn = jnp.sum(q.astype(F32) * n_st, axis=1, keepdims=True) * scale
    den = inter * qn + jnp.sum(p, axis=1, keepdims=True)
    out = num / jnp.maximum(jnp.abs(den), jnp.exp(-m_t))
    return c_next, n_next, m_new, out


def _ml_kernel(*refs, hg, cs, with_out):
    (qf, kf, vf, qb, kb, vb, gcf, gcb, grf, grb, c0_ref, n0_ref, m0_ref) = refs[:13]
    if with_out:
        of_ref, ob_ref, cfin_ref, nfin_ref, mfin_ref = refs[13:]
    else:
        cfin_ref, nfin_ref, mfin_ref = refs[13:]
        of_ref = ob_ref = None

    @pl.when(pl.program_id(2) == 0)
    def _():
        cfin_ref[...] = c0_ref[...]
        nfin_ref[...] = n0_ref[...]
        mfin_ref[...] = m0_ref[...]

    for d, (q_ref, k_ref, v_ref, gc_ref, gr_ref, o_ref) in enumerate(
            ((qf, kf, vf, gcf, grf, of_ref), (qb, kb, vb, gcb, grb, ob_ref))):
        for hh in range(hg):
            qcols = slice(hh * ML_DK, (hh + 1) * ML_DK)
            vcols = slice(hh * ML_DV, (hh + 1) * ML_DV)
            lb = d * hg + hh
            lu = 2 * hg + d * hg + hh
            ld = 4 * hg + d * hg + hh
            c_st = cfin_ref[0, d, hh]
            n_st = nfin_ref[0, d, hh, 0:1, :]
            m = mfin_ref[0, d, hh, 0:1, 0:1]
            for ci in range(cs):
                c = cs - 1 - ci if d else ci
                rows = slice(c * CHUNK, (c + 1) * CHUNK)
                c_st, n_st, m, o = _ml_chunk(
                    q_ref[0, rows, qcols].astype(BF16), k_ref[0, rows, qcols].astype(BF16),
                    v_ref[0, rows, vcols],
                    gc_ref[0, 0, rows, lb:lb + 1], gc_ref[0, 0, rows, lu:lu + 1],
                    gc_ref[0, 0, rows, ld:ld + 1], gr_ref[0, 0, c, lu:lu + 1, :],
                    c_st, n_st, m, bool(d), with_out)
                if with_out:
                    o_ref[0, rows, vcols] = o
            cfin_ref[0, d, hh] = c_st
            nfin_ref[0, d, hh] = jnp.broadcast_to(n_st, (8, ML_DK))
            mfin_ref[0, d, hh] = jnp.broadcast_to(m, (8, LANES))


def _ml_call(z, col0, gcol, grow, c0, n0, m0, hg, cs, with_out):
    b, t, _ = z.shape
    ng = ML_HEADS // hg
    blk = cs * CHUNK
    nb = t // blk
    r = grow.shape[3]
    wq = hg * ML_DK
    wv = hg * ML_DV
    qoff = col0 * LANES // wq
    koff = (col0 * LANES + ML_QK) // wq
    voff = (col0 * LANES + 2 * ML_QK) // wv

    def spec(width, off, rev):
        if rev:
            return pl.BlockSpec((1, blk, width), lambda bi, gi, n: (bi, nb - 1 - n, off + gi))
        return pl.BlockSpec((1, blk, width), lambda bi, gi, n: (bi, n, off + gi))

    def state_spec(*tail):
        zeros = (0,) * len(tail)
        return pl.BlockSpec((1, 2, hg) + tail, lambda bi, gi, n: (bi, 0, gi) + zeros)

    in_specs = (
        [spec(wq, qoff, False), spec(wq, koff, False), spec(wv, voff, False),
         spec(wq, qoff, True), spec(wq, koff, True), spec(wv, voff, True),
         pl.BlockSpec((1, 1, blk, LANES), lambda bi, gi, n: (bi, gi, n, 0)),
         pl.BlockSpec((1, 1, blk, LANES), lambda bi, gi, n: (bi, gi, nb - 1 - n, 0)),
         pl.BlockSpec((1, 1, cs, r, CHUNK), lambda bi, gi, n: (bi, gi, n, 0, 0)),
         pl.BlockSpec((1, 1, cs, r, CHUNK), lambda bi, gi, n: (bi, gi, nb - 1 - n, 0, 0)),
         state_spec(ML_DK, ML_DV), state_spec(8, ML_DK), state_spec(8, LANES)])
    st_specs = [state_spec(ML_DK, ML_DV), state_spec(8, ML_DK), state_spec(8, LANES)]
    st_shapes = [jax.ShapeDtypeStruct((b, 2, ML_HEADS, ML_DK, ML_DV), F32),
                 jax.ShapeDtypeStruct((b, 2, ML_HEADS, 8, ML_DK), F32),
                 jax.ShapeDtypeStruct((b, 2, ML_HEADS, 8, LANES), F32)]
    if with_out:
        out_specs = [pl.BlockSpec((1, blk, wv), lambda bi, gi, n: (bi, n, gi)),
                     pl.BlockSpec((1, blk, wv), lambda bi, gi, n: (bi, nb - 1 - n, gi))] + st_specs
        o_shape = jax.ShapeDtypeStruct((b, t, ML_V), F32)
        out_shape = [o_shape, o_shape] + st_shapes
    else:
        out_specs = st_specs
        out_shape = st_shapes
    return pl.pallas_call(
        functools.partial(_ml_kernel, hg=hg, cs=cs, with_out=with_out),
        grid=(b, ng, nb),
        in_specs=in_specs,
        out_specs=out_specs,
        out_shape=out_shape,
        compiler_params=_cparams("parallel", "parallel", "arbitrary"),
        name="mlstm_scan_out" if with_out else "mlstm_scan_state",
    )(z, z, z, z, z, z, gcol, gcol, grow, grow, c0, n0, m0)


def _head_rms(x, width):
    outs = []
    for h in range(x.shape[1] // width):
        xh = x[:, h * width:(h + 1) * width]
        ms = jnp.sum(xh * xh, axis=-1, keepdims=True) * (1.0 / width)
        outs.append(xh * lax.rsqrt(ms + EPS))
    return jnp.concatenate(outs, axis=-1)


def _merge_kernel(x_ref, of_ref, ob_ref, hf_ref, hb_ref, gz_ref, mo_ref, gg_ref, gm_ref,
                  gnw_ref, mnw_ref, wg_ref, wm_ref, wo_ref, gate_ref, o_ref):
    og = _head_rms(of_ref[0] + ob_ref[0], GDN_DV) * gnw_ref[...]
    gz = gz_ref[0]
    og = og * (gz * _sigmoid(gz))
    y_gdn = _dot(og.astype(BF16), wg_ref[...])
    hm = _head_rms(hf_ref[0] + hb_ref[0], ML_DV) * mnw_ref[...]
    hm = hm * _sigmoid(mo_ref[0])
    y_ml = _dot(hm.astype(BF16), wm_ref[...])
    merged = _sigmoid(gg_ref[0]) * y_gdn + _sigmoid(gm_ref[0]) * y_ml
    x_mix = _dot(merged.astype(BF16), wo_ref[...])
    o_ref[0] = x_ref[0] + gate_ref[0] * x_mix


def _merge_call(x, o_f, o_b, h_f, h_b, z, zo_col0, gnw, mnw, wg, wm, wo, gate):
    b, t, d = x.shape
    tm = _pick_tile(t, 256)
    nz = zo_col0 * LANES // d

    def tok(bi, i):
        return (bi, i, 0)

    def zcol(k):
        return lambda bi, i: (bi, i, nz + k)

    tile = lambda imap: pl.BlockSpec((1, tm, d), imap)
    full = lambda shape: pl.BlockSpec(shape, lambda bi, i: (0,) * len(shape))
    return pl.pallas_call(
        _merge_kernel,
        grid=(b, t // tm),
        in_specs=[tile(tok), tile(tok), tile(tok), tile(tok), tile(tok),
                  tile(zcol(0)), tile(zcol(1)), tile(zcol(2)), tile(zcol(3)),
                  full((1, d)), full((1, d)), full((d, d)), full((d, d)), full((d, d)),
                  pl.BlockSpec((1, 1, d), lambda bi, i: (bi, 0, 0))],
        out_specs=tile(tok),
        out_shape=jax.ShapeDtypeStruct((b, t, d), F32),
        compiler_params=_cparams("parallel", "parallel"),
        name="merge",
    )(x, o_f, o_b, h_f, h_b, z, z, z, z, gnw, mnw, wg, wm, wo, gate)


def _ffn2_kernel(ug_ref, ugp_ref, ugn_ref, uv_ref, uvp_ref, uvn_ref, cwg_ref, cwv_ref, wd_ref,
                 x_ref, gate_ref, nw_ref, o_ref, acc_ref):
    i = pl.program_id(1)
    j = pl.program_id(2)
    first = i == 0
    last = i == pl.num_programs(1) - 1

    def grid_conv(u_ref, up_ref, un_ref, cw_ref):
        u = u_ref[0]
        rows = u.shape[0]
        ext = jnp.concatenate([jnp.where(first, 0.0, up_ref[0]), u, jnp.where(last, 0.0, un_ref[0])], axis=0)
        col = lax.broadcasted_iota(jnp.int32, u.shape, 0) & (GRID_W - 1)
        taps = [ext[r * GRID_W:r * GRID_W + rows] for r in range(3)]
        side = []
        for dc in range(3):
            side.append(sum(taps[r] * cw_ref[3 * r + dc:3 * r + dc + 1, :] for r in range(3)))
        left = jnp.where(col == 0, 0.0, pltpu.roll(side[0], 1, 0))
        right = jnp.where(col == GRID_W - 1, 0.0, pltpu.roll(side[2], rows - 1, 0))
        return side[1] + left + right

    g = grid_conv(ug_ref, ugp_ref, ugn_ref, cwg_ref)
    v = grid_conv(uv_ref, uvp_ref, uvn_ref, cwv_ref)
    act = (g * _sigmoid(g) * v).astype(BF16)
    part = _dot(act, wd_ref[...])

    @pl.when(j == 0)
    def _():
        acc_ref[...] = part

    @pl.when(j > 0)
    def _():
        acc_ref[...] += part

    @pl.when(j == pl.num_programs(2) - 1)
    def _():
        x = x_ref[0] + gate_ref[0] * acc_ref[...]
        ms = jnp.mean(x * x, axis=-1, keepdims=True)
        o_ref[0] = x * lax.rsqrt(ms + EPS) * nw_ref[...]


def _ffn2_call(u, conv_w, w_down, x, gate, nw):
    b, t, d = x.shape
    f = w_down.shape[0]
    tm = _pick_tile(t, 512)
    tc = 256
    nj = f // tc
    rpt = tm // GRID_W
    nrows = t // GRID_W

    def main(off):
        return pl.BlockSpec((1, tm, tc), lambda bi, i, j: (bi, i, off + j))

    def prev(off):
        return pl.BlockSpec((1, GRID_W, tc), lambda bi, i, j: (bi, jnp.maximum(i * rpt - 1, 0), off + j))

    def nxt(off):
        return pl.BlockSpec((1, GRID_W, tc), lambda bi, i, j: (bi, jnp.minimum((i + 1) * rpt, nrows - 1), off + j))

    return pl.pallas_call(
        _ffn2_kernel,
        grid=(b, t // tm, nj),
        in_specs=[main(0), prev(0), nxt(0), main(nj), prev(nj), nxt(nj),
                  pl.BlockSpec((9, tc), lambda bi, i, j: (0, j)),
                  pl.BlockSpec((9, tc), lambda bi, i, j: (0, nj + j)),
                  pl.BlockSpec((tc, d), lambda bi, i, j: (j, 0)),
                  pl.BlockSpec((1, tm, d), lambda bi, i, j: (bi, i, 0)),
                  pl.BlockSpec((1, 1, d), lambda bi, i, j: (bi, 0, 0)),
                  pl.BlockSpec((1, d), lambda bi, i, j: (0, 0))],
        out_specs=pl.BlockSpec((1, tm, d), lambda bi, i, j: (bi, i, 0)),
        out_shape=jax.ShapeDtypeStruct((b, t, d), F32),
        scratch_shapes=[pltpu.VMEM((tm, d), F32)],
        compiler_params=_cparams("parallel", "parallel", "arbitrary"),
        name="ffn2",
    )(u, u, u, u, u, u, conv_w, conv_w, w_down, x, gate, nw)


def _gdn_gate_rows(gates):
    b, t, _ = gates.shape
    nc = t // CHUNK

    def rows(base):
        q = gates[:, :, base:base + 2 * GDN_HEADS].reshape(b, nc, CHUNK, 2, NQUAD, QUAD)
        return q.transpose(0, 1, 3, 4, 5, 2).reshape(b, nc, 2 * NQUAD, QW)

    return jnp.concatenate([rows(G_A), rows(G_B)], axis=2)


def _gate_layouts(gates, hg_ml):
    b, t, _ = gates.shape
    nc = t // CHUNK

    def regroup(cols, heads, hg):
        ng = heads // hg
        parts = []
        for off in cols:
            q = gates[:, :, off:off + 2 * heads].reshape(b, t, 2, ng, hg)
            parts.append(q.transpose(0, 3, 1, 2, 4).reshape(b, ng, t, 2 * hg))
        col = jnp.concatenate(parts, axis=-1)
        used = col.shape[-1]
        r = -(-used // 8) * 8
        colp = jnp.pad(col, ((0, 0), (0, 0), (0, 0), (0, LANES - used)))
        row = jnp.pad(col, ((0, 0), (0, 0), (0, 0), (0, r - used)))
        row = row.reshape(b, ng, nc, CHUNK, r).transpose(0, 1, 2, 4, 3)
        return colp, row

    return regroup([LANES + G_I, 2 * LANES + G_I, 3 * LANES + G_I], ML_HEADS, hg_ml)


def _mixer_states(x_seq, nw, shift, scale, w_state, w_aux, conv_w, alog_row, bias_row, states,
                  with_out, w_full=None):
    hg_ml, cs = 4, 1
    w = w_full if with_out else w_state
    z, z_aux = _nmm_call(x_seq, nw, shift, scale, w, w_aux, name="in_proj")
    gates = _gates_call(z_aux, alog_row, bias_row)
    m_col, m_row = _gate_layouts(gates, hg_ml)
    qkv = _conv_call(z, conv_w)
    s_gdn, c_ml, n_ml, m_ml = states
    cs = min(cs, x_seq.shape[1] // CHUNK)
    gdn_res = _gdn_call(qkv, gates[:, :, :LANES], _gdn_gate_rows(gates), s_gdn, with_out)
    ml_col0 = (2 * GDN_QK + GDN_V) // LANES
    ml_res = _ml_call(z, ml_col0, m_col, m_row, c_ml, n_ml, m_ml, hg_ml, cs, with_out)
    return z, gdn_res, ml_res


def kernel(x, c, ctx, c_ctx, w_ada, b_ada, norm1_w, w_in, gdn_conv, gdn_a_log, gdn_dt_bias, gdn_norm_w,
           ml_igate_b, ml_fgate_b, ml_norm_w, w_branch_gdn, w_branch_ml, w_out, norm2_w, w_up, ffn_conv,
           w_down, norm_out_w):
    bsz, _, d = x.shape
    depth = w_ada.shape[0]
    assert depth == 1, "single-layer problem: the context stream is never updated"
    l = 0

    sizes = (2 * GDN_QK + GDN_V, 2 * GDN_HEADS, 2 * GDN_HEADS, ML_QK, ML_QK, ML_V, 2 * ML_HEADS, 2 * ML_HEADS,
             GDN_V, ML_V, d, d)
    offs = [0]
    for s in sizes:
        offs.append(offs[-1] + s)
    wi = w_in[l]
    seg = lambda k: wi[:, offs[k]:offs[k + 1]]
    w_state = jnp.concatenate([seg(0), seg(3), seg(4), seg(5)], axis=1).astype(BF16)
    w_full = jnp.concatenate([seg(0), seg(3), seg(4), seg(5), seg(8), seg(9), seg(10), seg(11)], axis=1).astype(BF16)
    n_gate = 4 * GDN_HEADS + 4 * ML_HEADS
    w_aux = jnp.concatenate([seg(1), seg(2), seg(6), seg(7), jnp.zeros((d, LANES - n_gate), F32)], axis=1).astype(BF16)
    pad = lambda v, n: jnp.pad(v.reshape(1, -1).astype(F32), ((0, 0), (0, n - v.size)))
    alog_row = pad(gdn_a_log[l], LANES)
    bias_row = pad(jnp.concatenate([gdn_dt_bias[l].reshape(-1), jnp.zeros((2 * GDN_HEADS,), F32),
                                    ml_igate_b[l].reshape(-1), ml_fgate_b[l].reshape(-1)]), LANES)
    row = lambda v: v.reshape(1, -1).astype(F32)

    c_all = jnp.concatenate([c, c_ctx[None], jnp.zeros((8 - bsz - 1, d), F32)], axis=0)
    mods = _mod_call(c_all, w_ada[l].astype(BF16), row(b_ada[l]))
    mod_x = mods[:bsz].reshape(bsz, N_MOD, 1, d)
    mod_c = jnp.broadcast_to(mods[bsz].reshape(1, N_MOD, 1, d), (bsz, N_MOD, 1, d))

    zero_states = (jnp.zeros((bsz, 2, NQUAD, GDN_DK, QUAD * GDN_DV), F32),
                   jnp.zeros((bsz, 2, ML_HEADS, ML_DK, ML_DV), F32),
                   jnp.zeros((bsz, 2, ML_HEADS, 8, ML_DK), F32),
                   jnp.zeros((bsz, 2, ML_HEADS, 8, LANES), F32))
    common = (w_state, w_aux, gdn_conv[l].astype(F32), alog_row, bias_row)

    _, (s_gdn,), (c_ml, n_ml, m_ml) = _mixer_states(
        ctx, row(norm1_w[l]), mod_c[:, 0], mod_c[:, 1], *common, zero_states, False)

    z, (o_f, o_b, _), (h_f, h_b, _, _, _) = _mixer_states(
        x, row(norm1_w[l]), mod_x[:, 0], mod_x[:, 1], *common, (s_gdn, c_ml, n_ml, m_ml), True, w_full)
    x1 = _merge_call(x, o_f, o_b, h_f, h_b, z, STATE_COLS // LANES,
                     row(jnp.tile(gdn_norm_w[l], GDN_HEADS)), row(ml_norm_w[l]),
                     w_branch_gdn[l].astype(BF16), w_branch_ml[l].astype(BF16), w_out[l].astype(BF16),
                     mod_x[:, 2])
    u = _nmm_call(x1, row(norm2_w[l]), mod_x[:, 3], mod_x[:, 4], w_up[l].astype(BF16), name="ffn_up")
    return _ffn2_call(u, ffn_conv[l].reshape(9, -1).astype(F32), w_down[l].astype(BF16), x1,
                      mod_x[:, 5], row(norm_out_w))
```

```python
import functools

import jax
import jax.numpy as jnp
from jax import lax
from jax.experimental import pallas as pl
from jax.experimental.pallas import tpu as pltpu

F32 = jnp.float32
BF16 = jnp.bfloat16

GDN_HEADS = 8
GDN_DK = 128
GDN_DV = 128
ML_HEADS = 4
ML_DK = 128
ML_DV = 256
CHUNK = 64
GATE_CAP = 15.0
GRID_W = 64
N_MOD = 6
EPS = 1e-6
LANES = 128
HALO = 16
NEG_BIG = -1e30

GDN_QK = GDN_HEADS * GDN_DK
GDN_V = GDN_HEADS * GDN_DV
ML_QK = ML_HEADS * ML_DK
ML_V = ML_HEADS * ML_DV
STATE_COLS = 2 * GDN_QK + GDN_V + 2 * ML_QK + ML_V
OUT_COLS = GDN_V + ML_V + 2 * 1024

VMEM_LIMIT = 48 * 1024 * 1024


def _cparams(*sem):
    return pltpu.CompilerParams(dimension_semantics=sem, vmem_limit_bytes=VMEM_LIMIT)


def _dot(a, b):
    return jnp.dot(a, b, preferred_element_type=F32)


def _dot_nt(a, b):
    return lax.dot_general(a, b, (((1,), (1,)), ((), ())), preferred_element_type=F32)


def _dot_tn(a, b):
    return lax.dot_general(a, b, (((0,), (0,)), ((), ())), preferred_element_type=F32)


def _sigmoid(x):
    return 1.0 / (1.0 + jnp.exp(-x))


def _softplus(x):
    return jnp.maximum(x, 0.0) + jnp.log1p(jnp.exp(-jnp.abs(x)))


def _pick_tile(n, pref):
    t = min(n, pref)
    while n % t:
        t //= 2
    return t


def _mod_kernel(c_ref, w_ref, b_ref, o_ref):
    c = c_ref[...]
    s = c * _sigmoid(c)
    o_ref[...] = _dot(s.astype(BF16), w_ref[...]) + b_ref[...]


def _mod_call(c_all, w_ada, b_ada):
    rows, d = c_all.shape
    n = w_ada.shape[1]
    tn = 1024
    return pl.pallas_call(
        _mod_kernel,
        grid=(n // tn,),
        in_specs=[
            pl.BlockSpec((rows, d), lambda j: (0, 0)),
            pl.BlockSpec((d, tn), lambda j: (0, j)),
            pl.BlockSpec((1, tn), lambda j: (0, j)),
        ],
        out_specs=pl.BlockSpec((rows, tn), lambda j: (0, j)),
        out_shape=jax.ShapeDtypeStruct((rows, n), F32),
        compiler_params=_cparams("parallel"),
        name="mod",
    )(c_all, w_ada, b_ada)


def _nmm_kernel(x_ref, nw_ref, sh_ref, sc_ref, w_ref, *rest, has_aux):
    if has_aux:
        wa_ref, o_ref, oa_ref, hn_ref = rest
    else:
        o_ref, hn_ref = rest

    @pl.when(pl.program_id(2) == 0)
    def _():
        x = x_ref[0]
        ms = jnp.mean(x * x, axis=-1, keepdims=True)
        y = x * lax.rsqrt(ms + EPS) * nw_ref[...]
        hb = (y * (1.0 + sc_ref[0]) + sh_ref[0]).astype(BF16)
        hn_ref[...] = hb
        if has_aux:
            oa_ref[0] = _dot(hb, wa_ref[...])

    o_ref[0] = _dot(hn_ref[...], w_ref[pl.program_id(2)]).astype(o_ref.dtype)


def _nmm_call(x, nw, shift, scale, w, w_aux=None, out_dtype=BF16, name="nmm"):
    b, t, d = x.shape
    c = w.shape[1]
    tm = _pick_tile(t, 1024)
    tn = _pick_tile(c, 512)
    nj = c // tn
    has_aux = w_aux is not None
    w_tiles = w.reshape(d, nj, tn).transpose(1, 0, 2)
    in_specs = [
        pl.BlockSpec((1, tm, d), lambda bi, i, j: (bi, i, 0)),
        pl.BlockSpec((1, d), lambda bi, i, j: (0, 0)),
        pl.BlockSpec((1, 1, d), lambda bi, i, j: (bi, 0, 0)),
        pl.BlockSpec((1, 1, d), lambda bi, i, j: (bi, 0, 0)),
        pl.BlockSpec((nj, d, tn), lambda bi, i, j: (0, 0, 0), pipeline_mode=pl.Buffered(1)),
    ]
    out_specs = [pl.BlockSpec((1, tm, tn), lambda bi, i, j: (bi, i, j))]
    out_shape = [jax.ShapeDtypeStruct((b, t, c), out_dtype)]
    args = [x, nw, shift, scale, w_tiles]
    if has_aux:
        in_specs.append(pl.BlockSpec((d, LANES), lambda bi, i, j: (0, 0)))
        out_specs.append(pl.BlockSpec((1, tm, LANES), lambda bi, i, j: (bi, i, 0)))
        out_shape.append(jax.ShapeDtypeStruct((b, t, LANES), F32))
        args.append(w_aux)
    res = pl.pallas_call(
        functools.partial(_nmm_kernel, has_aux=has_aux),
        grid=(b, t // tm, c // tn),
        in_specs=in_specs,
        out_specs=out_specs,
        out_shape=out_shape,
        scratch_shapes=[pltpu.VMEM((tm, d), BF16)],
        compiler_params=_cparams("parallel", "parallel", "arbitrary"),
        name=name,
    )(*args)
    return res if has_aux else res[0]


G_A, G_B, G_I, G_F = 0, 16, 32, 40


def _chunk_scan(x, pos, backward, op, ident):
    rows = x.shape[0]
    yf, yb = x, x
    s = 1
    while s < CHUNK:
        yf = op(yf, jnp.where(pos >= s, pltpu.roll(yf, s, 0), ident))
        yb = op(yb, jnp.where(pos + s < CHUNK, pltpu.roll(yb, rows - s, 0), ident))
        s *= 2
    return jnp.where(backward, yb, yf)


def _gates_kernel(z_ref, alog_ref, bias_ref, o_ref):
    z = z_ref[0] + bias_ref[...]
    lane = lax.broadcasted_iota(jnp.int32, z.shape, 1)
    pos = lax.broadcasted_iota(jnp.int32, z.shape, 0) & (CHUNK - 1)
    backward = ((lane < G_I) & ((lane & 15) >= 8)) | ((lane >= G_I) & ((lane & 7) >= 4))

    log_decay = -jnp.exp(alog_ref[...]) * _softplus(z)
    beta = _sigmoid(z)
    capped = GATE_CAP * jnp.tanh(z * (1.0 / GATE_CAP))
    log_forget = -_softplus(-capped)

    summand = jnp.where(lane < G_B, log_decay, jnp.where(lane >= G_F, log_forget, 0.0))
    csum = _chunk_scan(summand, pos, backward, jnp.add, 0.0)
    o_ref[0, :, 0:LANES] = jnp.where(lane < G_B, csum, beta)

    bc = pltpu.roll(csum, LANES - (G_F - G_I), 1)
    backward_i = (lane & 7) >= 4
    u = capped - bc
    umax = _chunk_scan(u, pos, backward_i, jnp.maximum, NEG_BIG)
    o_ref[0, :, LANES:2 * LANES] = bc
    o_ref[0, :, 2 * LANES:3 * LANES] = u
    o_ref[0, :, 3 * LANES:4 * LANES] = bc + umax


def _gates_call(z_aux, alog_row, bias_row):
    b, t, _ = z_aux.shape
    tm = _pick_tile(t, 512)
    return pl.pallas_call(
        _gates_kernel,
        grid=(b, t // tm),
        in_specs=[
            pl.BlockSpec((1, tm, LANES), lambda bi, i: (bi, i, 0)),
            pl.BlockSpec((1, LANES), lambda bi, i: (0, 0)),
            pl.BlockSpec((1, LANES), lambda bi, i: (0, 0)),
        ],
        out_specs=pl.BlockSpec((1, tm, 4 * LANES), lambda bi, i: (bi, i, 0)),
        out_shape=jax.ShapeDtypeStruct((b, t, 4 * LANES), F32),
        compiler_params=_cparams("parallel", "parallel"),
        name="gates",
    )(z_aux, alog_row, bias_row)


def _conv_kernel(z_ref, zp_ref, zn_ref, w_ref, o_ref):
    i = pl.program_id(1)
    j = pl.program_id(2)
    z = z_ref[0].astype(F32)
    rows = z.shape[0]
    row = lax.broadcasted_iota(jnp.int32, z.shape, 0)
    prev_row = jnp.where(i == 0, 0.0, zp_ref[0, HALO - 1:HALO, :].astype(F32))
    next_row = jnp.where(i == pl.num_programs(1) - 1, 0.0, zn_ref[0, 0:1, :].astype(F32))
    z_prev = jnp.where(row == 0, prev_row, pltpu.roll(z, 1, 0))
    z_next = jnp.where(row == rows - 1, next_row, pltpu.roll(z, rows - 1, 0))
    y = z_prev * w_ref[0:1, :] + z * w_ref[1:2, :] + z_next * w_ref[2:3, :]
    y = y * _sigmoid(y)
    is_qk = j < 2
    for h in range(GDN_HEADS):
        cols = slice(h * GDN_DK, (h + 1) * GDN_DK)
        yh = y[:, cols]
        inv = lax.rsqrt(jnp.sum(yh * yh, axis=-1, keepdims=True) + EPS)
        o_ref[0, :, cols] = (yh * jnp.where(is_qk, inv, 1.0)).astype(o_ref.dtype)


def _conv_call(z, conv_w):
    b, t, _ = z.shape
    tt = _pick_tile(t, 512)
    g = GDN_QK
    nbh = tt // HALO
    return pl.pallas_call(
        _conv_kernel,
        grid=(b, t // tt, 3),
        in_specs=[
            pl.BlockSpec((1, tt, g), lambda bi, i, j: (bi, i, j)),
            pl.BlockSpec((1, HALO, g), lambda bi, i, j: (bi, jnp.maximum(i * nbh - 1, 0), j)),
            pl.BlockSpec((1, HALO, g), lambda bi, i, j: (bi, jnp.minimum((i + 1) * nbh, t // HALO - 1), j)),
            pl.BlockSpec((3, g), lambda bi, i, j: (0, j)),
        ],
        out_specs=pl.BlockSpec((1, tt, g), lambda bi, i, j: (bi, i, j)),
        out_shape=jax.ShapeDtypeStruct((b, t, 3 * g), BF16),
        compiler_params=_cparams("parallel", "parallel", "parallel"),
        name="gdn_conv",
    )(z, z, z, conv_w)


QUAD = 4
QW = QUAD * CHUNK
NQUAD = GDN_HEADS // QUAD


def _quad_masks(reverse):
    ii = lax.broadcasted_iota(jnp.int32, (QW, QW), 0)
    jj = lax.broadcasted_iota(jnp.int32, (QW, QW), 1)
    same = (ii ^ jj) < CHUNK
    if reverse:
        return same & (jj > ii), same & (jj >= ii), ii == jj
    return same & (jj < ii), same & (jj <= ii), ii == jj


def _gdn_kernel(*refs, with_out):
    (qf, kf, vf, qb, kb, vb, gcf, gcb, grf, grb, s0_ref) = refs[:11]
    if with_out:
        of_ref, ob_ref, sfin_ref = refs[11:]
    else:
        (sfin_ref,) = refs[11:]
        of_ref = ob_ref = None

    @pl.when(pl.program_id(1) == 0)
    def _():
        sfin_ref[...] = s0_ref[...]

    scale = GDN_DK ** -0.5
    groups = []
    for d, (q_ref, k_ref, v_ref, gc_ref, gr_ref, o_ref) in enumerate(
            ((qf, kf, vf, gcf, grf, of_ref), (qb, kb, vb, gcb, grb, ob_ref))):
        for g in range(NQUAD):
            heads = tuple(range(g * QUAD, (g + 1) * QUAD))

            def stack(ref):
                return jnp.concatenate([ref[0, :, h * GDN_DK:(h + 1) * GDN_DK] for h in heads], axis=0)

            def col(base):
                lanes = [base + d * GDN_HEADS + h for h in heads]
                return jnp.concatenate([gc_ref[0, :, l:l + 1] for l in lanes], axis=0)

            r = d * NQUAD + g
            groups.append(dict(
                d=d, g=g, heads=heads, o_ref=o_ref, k=stack(k_ref), v=stack(v_ref),
                q=stack(q_ref) if with_out else None, gcol=col(G_A), bcol=col(G_B),
                grow=gr_ref[0, 0, r:r + 1, :], brow=gr_ref[0, 0, 2 * NQUAD + r:2 * NQUAD + r + 1, :]))

    for grp in groups:
        strict, incl, diag = _quad_masks(bool(grp["d"]))
        decay = jnp.exp(jnp.where(incl, grp["gcol"] - grp["grow"], NEG_BIG))
        kk = _dot_nt(grp["k"], grp["k"])
        x = jnp.where(strict, kk * decay * (-grp["bcol"]), 0.0)
        grp["x"] = x
        grp["s"] = jnp.where(diag, 1.0, x)
        if with_out:
            grp["attn"] = (_dot_nt(grp["q"], grp["k"]) * decay * scale).astype(BF16)
    for grp in groups:
        xb = grp["x"].astype(BF16)
        grp["xm"] = _dot(xb, xb)
    m = 2
    while 2 * m < CHUNK:
        for grp in groups:
            xb = grp["xm"].astype(BF16)
            both = _dot(jnp.concatenate([grp["s"].astype(BF16), xb], axis=0), xb)
            grp["s"] = grp["s"] + both[:QW]
            grp["xm"] = both[QW:]
        m *= 2
    for grp in groups:
        t = grp["s"] + _dot(grp["s"].astype(BF16), grp["xm"].astype(BF16))
        brow, grow = grp["brow"], grp["grow"]
        grp["u"] = _dot((t * brow).astype(BF16), grp["v"])
        grp["w"] = _dot((t * (brow * jnp.exp(grow))).astype(BF16), grp["k"])

    row = lax.broadcasted_iota(jnp.int32, (QW, GDN_DV), 0)
    for grp in groups:
        d, g, gcol = grp["d"], grp["g"], grp["gcol"]
        s4 = sfin_ref[0, d, g]
        lhs = grp["w"].astype(BF16)
        if with_out:
            lhs = jnp.concatenate([lhs, grp["q"]], axis=0)
        ws = _dot(lhs, s4.astype(BF16))

        def diag_blocks(base):
            return jnp.concatenate(
                [ws[base + c * CHUNK:base + (c + 1) * CHUNK, c * GDN_DV:(c + 1) * GDN_DV] for c in range(QUAD)],
                axis=0)

        v_new = grp["u"] - diag_blocks(0)
        end = 0 if d else CHUNK - 1
        g_end = [gcol[c * CHUNK + end:c * CHUNK + end + 1] for c in range(QUAD)]
        g_end_col = jnp.concatenate([jnp.broadcast_to(ge, (CHUNK, 1)) for ge in g_end], axis=0)
        vt = v_new * jnp.exp(g_end_col - gcol)
        vbd = jnp.concatenate(
            [jnp.where((row >= c * CHUNK) & (row < (c + 1) * CHUNK), vt, 0.0).astype(BF16) for c in range(QUAD)],
            axis=1)
        decay_lane = jnp.concatenate([jnp.broadcast_to(jnp.exp(ge), (1, GDN_DV)) for ge in g_end], axis=1)
        sfin_ref[0, d, g] = s4 * decay_lane + _dot_tn(grp["k"], vbd)
        if with_out:
            o = (jnp.exp(gcol) * scale) * diag_blocks(QW) + _dot(grp["attn"], v_new.astype(BF16))
            for c, h in enumerate(grp["heads"]):
                grp["o_ref"][0, :, h * GDN_DV:(h + 1) * GDN_DV] = o[c * CHUNK:(c + 1) * CHUNK].astype(BF16)


def _gdn_call(qkv, gcol, grow, s0, with_out):
    b, t, _ = qkv.shape
    nb = t // CHUNK
    w = GDN_QK

    def fwd(*tail):
        return lambda bi, n: (bi, n) + tail

    def bwd(*tail):
        return lambda bi, n: (bi, nb - 1 - n) + tail

    s_spec = pl.BlockSpec((1, 2, NQUAD, GDN_DK, QUAD * GDN_DV), lambda bi, n: (bi, 0, 0, 0, 0))
    in_specs = (
        [pl.BlockSpec((1, CHUNK, w), fwd(o)) for o in range(3)]
        + [pl.BlockSpec((1, CHUNK, w), bwd(o)) for o in range(3)]
        + [pl.BlockSpec((1, CHUNK, LANES), fwd(0)), pl.BlockSpec((1, CHUNK, LANES), bwd(0)),
           pl.BlockSpec((1, 1, 4 * NQUAD, QW), fwd(0, 0)), pl.BlockSpec((1, 1, 4 * NQUAD, QW), bwd(0, 0)),
           s_spec])
    s_shape = jax.ShapeDtypeStruct(s0.shape, F32)
    if with_out:
        out_specs = [pl.BlockSpec((1, CHUNK, w), fwd(0)), pl.BlockSpec((1, CHUNK, w), bwd(0)), s_spec]
        o_shape = jax.ShapeDtypeStruct((b, t, GDN_V), BF16)
        out_shape = [o_shape, o_shape, s_shape]
    else:
        out_specs = [s_spec]
        out_shape = [s_shape]
    return pl.pallas_call(
        functools.partial(_gdn_kernel, with_out=with_out),
        grid=(b, nb),
        in_specs=in_specs,
        out_specs=out_specs,
        out_shape=out_shape,
        compiler_params=_cparams("parallel", "arbitrary"),
        name="gdn_scan_out" if with_out else "gdn_scan_state",
    )(qkv, qkv, qkv, qkv, qkv, qkv, gcol, gcol, grow, grow, s0)


def _ml_kernel(*refs, with_out):
    (qf, kf, vf, qb, kb, vb, gcf, gcb, grf, grb, c0_ref, n0_ref, m0_ref) = refs[:13]
    if with_out:
        of_ref, ob_ref, cfin_ref, nfin_ref, mfin_ref = refs[13:]
    else:
        cfin_ref, nfin_ref, mfin_ref = refs[13:]
        of_ref = ob_ref = None

    @pl.when(pl.program_id(1) == 0)
    def _():
        cfin_ref[...] = c0_ref[...]
        nfin_ref[...] = n0_ref[...]
        mfin_ref[...] = m0_ref[...]

    scale = ML_DK ** -0.5
    heads = tuple(range(ML_HEADS))

    def rows_of(c):
        return slice(c * CHUNK, (c + 1) * CHUNK)

    def spread(xs):
        return jnp.concatenate([jnp.broadcast_to(x, (CHUNK, 1)) for x in xs], axis=0)

    groups = []
    for d, (q_ref, k_ref, v_ref, gc_ref, gr_ref, o_ref) in enumerate(
            ((qf, kf, vf, gcf, grf, of_ref), (qb, kb, vb, gcb, grb, ob_ref))):

        def stack(ref, width):
            return jnp.concatenate([ref[0, :, h * width:(h + 1) * width] for h in heads], axis=0)

        def col(base):
            lanes = [base + G_I + d * ML_HEADS + h for h in heads]
            return jnp.concatenate([gc_ref[0, :, l:l + 1] for l in lanes], axis=0)

        grp = dict(d=d, o_ref=o_ref, k=stack(k_ref, ML_DK), v=stack(v_ref, ML_DV),
                   bcol=col(LANES), ucol=col(2 * LANES), dmcol=col(3 * LANES), urow=gr_ref[0, 0, d:d + 1, :])
        if with_out:
            grp["q"] = stack(q_ref, ML_DK)
            grp["qk"] = _dot_nt(grp["q"], grp["k"])
        groups.append(grp)

    row = lax.broadcasted_iota(jnp.int32, (QW, ML_DV), 0)
    for grp in groups:
        d, k4, v4, bcol, ucol, dmcol = grp["d"], grp["k"], grp["v"], grp["bcol"], grp["ucol"], grp["dmcol"]
        end = 0 if d else CHUNK - 1
        m = [mfin_ref[0, d, h, 0:1, 0:1] for h in heads]
        n = [nfin_ref[0, d, h, 0:1, :] for h in heads]
        tot = [bcol[c * CHUNK + end:c * CHUNK + end + 1] for c in heads]
        m_new = [jnp.maximum(tot[c] + m[c], dmcol[c * CHUNK + end:c * CHUNK + end + 1]) for c in heads]
        dec = [jnp.exp(tot[c] + m[c] - m_new[c]) for c in heads]
        wt = jnp.exp(spread(tot) + ucol - spread(m_new))
        vw = v4.astype(F32) * wt
        vbd = jnp.concatenate(
            [jnp.where((row >= c * CHUNK) & (row < (c + 1) * CHUNK), vw, 0.0).astype(BF16) for c in heads], axis=1)
        c4 = cfin_ref[0, d]
        dec_lane = jnp.concatenate([jnp.broadcast_to(dec[c], (1, ML_DV)) for c in heads], axis=1)
        cfin_ref[0, d] = c4 * dec_lane + _dot_tn(k4, vbd)
        kw = k4.astype(F32) * wt
        for c in heads:
            n_next = dec[c] * n[c] + jnp.sum(kw[rows_of(c)], axis=0, keepdims=True)
            nfin_ref[0, d, c] = jnp.broadcast_to(n_next, (8, ML_DK))
            mfin_ref[0, d, c] = jnp.broadcast_to(m_new[c], (8, LANES))
        if not with_out:
            continue
        q4 = grp["q"]
        _, incl, _ = _quad_masks(bool(d))
        m4 = spread(m)
        m_t = jnp.maximum(bcol + m4, dmcol)
        inter = jnp.exp(bcol + m4 - m_t)
        p = grp["qk"] * scale * jnp.exp(jnp.where(incl, bcol + grp["urow"] - m_t, NEG_BIG))
        c4b = c4.astype(BF16)
        qc = jnp.concatenate([_dot(q4[rows_of(c)], c4b[:, c * ML_DV:(c + 1) * ML_DV]) for c in heads], axis=0)
        num = (inter * scale) * qc + _dot(p.astype(BF16), v4)
        n_rows = jnp.concatenate([jnp.broadcast_to(n[c], (CHUNK, ML_DK)) for c in heads], axis=0)
        qn = jnp.sum(q4.astype(F32) * n_rows, axis=1, keepdims=True) * scale
        den = inter * qn + jnp.sum(p, axis=1, keepdims=True)
        out = num * (1.0 / jnp.maximum(jnp.abs(den), jnp.exp(-m_t)))
        for c in heads:
            grp["o_ref"][0, :, c * ML_DV:(c + 1) * ML_DV] = out[rows_of(c)].astype(BF16)


def _ml_call(z, col0, gates, urow, c0, n0, m0, with_out):
    b, t, _ = z.shape
    nb = t // CHUNK
    qoff = col0 * LANES // ML_QK
    koff = (col0 * LANES + ML_QK) // ML_QK
    voff = (col0 * LANES + 2 * ML_QK) // ML_V

    def fwd(*tail):
        return lambda bi, n: (bi, n) + tail

    def bwd(*tail):
        return lambda bi, n: (bi, nb - 1 - n) + tail

    def state_spec(shape):
        return pl.BlockSpec((1,) + shape, lambda bi, n: (bi,) + (0,) * len(shape))

    st = [(2, ML_DK, ML_HEADS * ML_DV), (2, ML_HEADS, 8, ML_DK), (2, ML_HEADS, 8, LANES)]
    st_specs = [state_spec(s) for s in st]
    st_shapes = [jax.ShapeDtypeStruct((b,) + s, F32) for s in st]
    in_specs = (
        [pl.BlockSpec((1, CHUNK, ML_QK), fwd(qoff)), pl.BlockSpec((1, CHUNK, ML_QK), fwd(koff)),
         pl.BlockSpec((1, CHUNK, ML_V), fwd(voff)),
         pl.BlockSpec((1, CHUNK, ML_QK), bwd(qoff)), pl.BlockSpec((1, CHUNK, ML_QK), bwd(koff)),
         pl.BlockSpec((1, CHUNK, ML_V), bwd(voff)),
         pl.BlockSpec((1, CHUNK, 4 * LANES), fwd(0)), pl.BlockSpec((1, CHUNK, 4 * LANES), bwd(0)),
         pl.BlockSpec((1, 1, 8, QW), fwd(0, 0)), pl.BlockSpec((1, 1, 8, QW), bwd(0, 0))]
        + st_specs)
    if with_out:
        out_specs = [pl.BlockSpec((1, CHUNK, ML_V), fwd(0)), pl.BlockSpec((1, CHUNK, ML_V), bwd(0))] + st_specs
        o_shape = jax.ShapeDtypeStruct((b, t, ML_V), BF16)
        out_shape = [o_shape, o_shape] + st_shapes
    else:
        out_specs = st_specs
        out_shape = st_shapes
    return pl.pallas_call(
        functools.partial(_ml_kernel, with_out=with_out),
        grid=(b, nb),
        in_specs=in_specs,
        out_specs=out_specs,
        out_shape=out_shape,
        compiler_params=_cparams("parallel", "arbitrary"),
        name="mlstm_scan_out" if with_out else "mlstm_scan_state",
    )(z, z, z, z, z, z, gates, gates, urow, urow, c0, n0, m0)


def _head_rms(x, width):
    outs = []
    for h in range(x.shape[1] // width):
        xh = x[:, h * width:(h + 1) * width]
        ms = jnp.sum(xh * xh, axis=-1, keepdims=True) * (1.0 / width)
        outs.append(xh * lax.rsqrt(ms + EPS))
    return jnp.concatenate(outs, axis=-1)


def _merge_kernel(x_ref, of_ref, ob_ref, hf_ref, hb_ref, gz_ref, mo_ref, gg_ref, gm_ref,
                  gnw_ref, mnw_ref, wg_ref, wm_ref, wo_ref, gate_ref, o_ref):
    f32 = lambda ref: ref[0].astype(F32)
    og = _head_rms(f32(of_ref) + f32(ob_ref), GDN_DV) * gnw_ref[...]
    gz = f32(gz_ref)
    og = og * (gz * _sigmoid(gz))
    y_gdn = _dot(og.astype(BF16), wg_ref[...])
    hm = _head_rms(f32(hf_ref) + f32(hb_ref), ML_DV) * mnw_ref[...]
    hm = hm * _sigmoid(f32(mo_ref))
    y_ml = _dot(hm.astype(BF16), wm_ref[...])
    merged = _sigmoid(f32(gg_ref)) * y_gdn + _sigmoid(f32(gm_ref)) * y_ml
    x_mix = _dot(merged.astype(BF16), wo_ref[...])
    o_ref[0] = x_ref[0] + gate_ref[0] * x_mix


def _merge_call(x, o_f, o_b, h_f, h_b, z, zo_col0, gnw, mnw, wg, wm, wo, gate):
    b, t, d = x.shape
    tm = _pick_tile(t, 256)
    nz = zo_col0 * LANES // d

    def tok(bi, i):
        return (bi, i, 0)

    def zcol(k):
        return lambda bi, i: (bi, i, nz + k)

    tile = lambda imap: pl.BlockSpec((1, tm, d), imap)
    full = lambda shape: pl.BlockSpec(shape, lambda bi, i: (0,) * len(shape))
    return pl.pallas_call(
        _merge_kernel,
        grid=(b, t // tm),
        in_specs=[tile(tok), tile(tok), tile(tok), tile(tok), tile(tok),
                  tile(zcol(0)), tile(zcol(1)), tile(zcol(2)), tile(zcol(3)),
                  full((1, d)), full((1, d)), full((d, d)), full((d, d)), full((d, d)),
                  pl.BlockSpec((1, 1, d), lambda bi, i: (bi, 0, 0))],
        out_specs=tile(tok),
        out_shape=jax.ShapeDtypeStruct((b, t, d), F32),
        compiler_params=_cparams("parallel", "parallel"),
        name="merge",
    )(x, o_f, o_b, h_f, h_b, z, z, z, z, gnw, mnw, wg, wm, wo, gate)


def _ffn2_kernel(ug_ref, ugp_ref, ugn_ref, uv_ref, uvp_ref, uvn_ref, cwg_ref, cwv_ref, wd_ref,
                 x_ref, gate_ref, nw_ref, o_ref, acc_ref, act_ref):
    i = pl.program_id(1)
    j = pl.program_id(2)
    first = i == 0
    last = i == pl.num_programs(1) - 1
    n_img_rows = ug_ref.shape[1] // GRID_W
    col = lax.broadcasted_iota(jnp.int32, (GRID_W, LANES), 0)

    def row_conv(u_ref, up_ref, un_ref, cw_ref, cols):
        w = [cw_ref[k:k + 1, cols] for k in range(9)]
        cache = {}

        def strip(r):
            if r not in cache:
                if r < 0:
                    cache[r] = jnp.where(first, 0.0, up_ref[0, :, cols].astype(F32))
                elif r == n_img_rows:
                    cache[r] = jnp.where(last, 0.0, un_ref[0, :, cols].astype(F32))
                else:
                    cache[r] = u_ref[0, r * GRID_W:(r + 1) * GRID_W, cols].astype(F32)
            return cache[r]

        def conv(r):
            taps = [strip(r - 1), strip(r), strip(r + 1)]
            side = [taps[0] * w[dc] + taps[1] * w[3 + dc] + taps[2] * w[6 + dc] for dc in range(3)]
            left = jnp.where(col == 0, 0.0, pltpu.roll(side[0], 1, 0))
            right = jnp.where(col == GRID_W - 1, 0.0, pltpu.roll(side[2], GRID_W - 1, 0))
            return side[1] + left + right

        return conv

    for cb in range(ug_ref.shape[2] // LANES):
        cols = slice(cb * LANES, (cb + 1) * LANES)
        conv_g = row_conv(ug_ref, ugp_ref, ugn_ref, cwg_ref, cols)
        conv_v = row_conv(uv_ref, uvp_ref, uvn_ref, cwv_ref, cols)
        for r in range(n_img_rows):
            g = conv_g(r)
            act_ref[r * GRID_W:(r + 1) * GRID_W, cols] = (g * _sigmoid(g) * conv_v(r)).astype(BF16)
    part = _dot(act_ref[...], wd_ref[j])

    @pl.when(j == 0)
    def _():
        acc_ref[...] = part

    @pl.when(j > 0)
    def _():
        acc_ref[...] += part

    @pl.when(j == pl.num_programs(2) - 1)
    def _():
        x = x_ref[0] + gate_ref[0] * acc_ref[...]
        ms = jnp.mean(x * x, axis=-1, keepdims=True)
        o_ref[0] = x * lax.rsqrt(ms + EPS) * nw_ref[...]


def _ffn2_call(u, conv_w, w_down, x, gate, nw):
    b, t, d = x.shape
    f = w_down.shape[0]
    tm = _pick_tile(t, 512)
    tc = 256
    nj = f // tc
    w_tiles = w_down.reshape(nj, tc, d)
    rpt = tm // GRID_W
    nrows = t // GRID_W

    def main(off):
        return pl.BlockSpec((1, tm, tc), lambda bi, i, j: (bi, i, off + j))

    def prev(off):
        return pl.BlockSpec((1, GRID_W, tc), lambda bi, i, j: (bi, jnp.maximum(i * rpt - 1, 0), off + j))

    def nxt(off):
        return pl.BlockSpec((1, GRID_W, tc), lambda bi, i, j: (bi, jnp.minimum((i + 1) * rpt, nrows - 1), off + j))

    return pl.pallas_call(
        _ffn2_kernel,
        grid=(b, t // tm, nj),
        in_specs=[main(0), prev(0), nxt(0), main(nj), prev(nj), nxt(nj),
                  pl.BlockSpec((9, tc), lambda bi, i, j: (0, j)),
                  pl.BlockSpec((9, tc), lambda bi, i, j: (0, nj + j)),
                  pl.BlockSpec((nj, tc, d), lambda bi, i, j: (0, 0, 0), pipeline_mode=pl.Buffered(1)),
                  pl.BlockSpec((1, tm, d), lambda bi, i, j: (bi, i, 0)),
                  pl.BlockSpec((1, 1, d), lambda bi, i, j: (bi, 0, 0)),
                  pl.BlockSpec((1, d), lambda bi, i, j: (0, 0))],
        out_specs=pl.BlockSpec((1, tm, d), lambda bi, i, j: (bi, i, 0)),
        out_shape=jax.ShapeDtypeStruct((b, t, d), F32),
        scratch_shapes=[pltpu.VMEM((tm, d), F32), pltpu.VMEM((tm, tc), BF16)],
        compiler_params=_cparams("parallel", "parallel", "arbitrary"),
        name="ffn2",
    )(u, u, u, u, u, u, conv_w, conv_w, w_tiles, x, gate, nw)


def _gdn_gate_rows(gates):
    b, t, _ = gates.shape
    nc = t // CHUNK

    def rows(base):
        q = gates[:, :, base:base + 2 * GDN_HEADS].reshape(b, nc, CHUNK, 2, NQUAD, QUAD)
        return q.transpose(0, 1, 3, 4, 5, 2).reshape(b, nc, 2 * NQUAD, QW)

    return jnp.concatenate([rows(G_A), rows(G_B)], axis=2)


def _ml_gate_rows(gates):
    b, t, _ = gates.shape
    nc = t // CHUNK
    base = 2 * LANES + G_I
    u = gates[:, :, base:base + 2 * ML_HEADS].reshape(b, nc, CHUNK, 2, ML_HEADS)
    u = u.transpose(0, 1, 3, 4, 2).reshape(b, nc, 2, QW)
    return jnp.pad(u, ((0, 0), (0, 0), (0, 6), (0, 0)))


def _mixer_states(x_seq, nw, shift, scale, w_state, w_aux, conv_w, alog_row, bias_row, states,
                  with_out, w_full=None):
    w = w_full if with_out else w_state
    z, z_aux = _nmm_call(x_seq, nw, shift, scale, w, w_aux, name="in_proj")
    gates = _gates_call(z_aux, alog_row, bias_row)
    qkv = _conv_call(z, conv_w)
    s_gdn, c_ml, n_ml, m_ml = states
    gdn_res = _gdn_call(qkv, gates[:, :, :LANES], _gdn_gate_rows(gates), s_gdn, with_out)
    ml_col0 = (2 * GDN_QK + GDN_V) // LANES
    ml_res = _ml_call(z, ml_col0, gates, _ml_gate_rows(gates), c_ml, n_ml, m_ml, with_out)
    return z, gdn_res, ml_res


def kernel(x, c, ctx, c_ctx, w_ada, b_ada, norm1_w, w_in, gdn_conv, gdn_a_log, gdn_dt_bias, gdn_norm_w,
           ml_igate_b, ml_fgate_b, ml_norm_w, w_branch_gdn, w_branch_ml, w_out, norm2_w, w_up, ffn_conv,
           w_down, norm_out_w):
    bsz, _, d = x.shape
    depth = w_ada.shape[0]
    assert depth == 1, "single-layer problem: the context stream is never updated"
    l = 0

    sizes = (2 * GDN_QK + GDN_V, 2 * GDN_HEADS, 2 * GDN_HEADS, ML_QK, ML_QK, ML_V, 2 * ML_HEADS, 2 * ML_HEADS,
             GDN_V, ML_V, d, d)
    offs = [0]
    for s in sizes:
        offs.append(offs[-1] + s)
    wi = w_in[l]
    seg = lambda k: wi[:, offs[k]:offs[k + 1]]
    w_state = jnp.concatenate([seg(0), seg(3), seg(4), seg(5)], axis=1).astype(BF16)
    w_full = jnp.concatenate([seg(0), seg(3), seg(4), seg(5), seg(8), seg(9), seg(10), seg(11)], axis=1).astype(BF16)
    n_gate = 4 * GDN_HEADS + 4 * ML_HEADS
    w_aux = jnp.concatenate([seg(1), seg(2), seg(6), seg(7), jnp.zeros((d, LANES - n_gate), F32)], axis=1).astype(BF16)
    pad = lambda v, n: jnp.pad(v.reshape(1, -1).astype(F32), ((0, 0), (0, n - v.size)))
    alog_row = pad(gdn_a_log[l], LANES)
    bias_row = pad(jnp.concatenate([gdn_dt_bias[l].reshape(-1), jnp.zeros((2 * GDN_HEADS,), F32),
                                    ml_igate_b[l].reshape(-1), ml_fgate_b[l].reshape(-1)]), LANES)
    row = lambda v: v.reshape(1, -1).astype(F32)

    c_all = jnp.concatenate([c, c_ctx[None], jnp.zeros((8 - bsz - 1, d), F32)], axis=0)
    mods = _mod_call(c_all, w_ada[l].astype(BF16), row(b_ada[l]))
    mod_x = mods[:bsz].reshape(bsz, N_MOD, 1, d)
    mod_c = jnp.broadcast_to(mods[bsz].reshape(1, N_MOD, 1, d), (bsz, N_MOD, 1, d))

    zero_states = (jnp.zeros((bsz, 2, NQUAD, GDN_DK, QUAD * GDN_DV), F32),
                   jnp.zeros((bsz, 2, ML_DK, ML_HEADS * ML_DV), F32),
                   jnp.zeros((bsz, 2, ML_HEADS, 8, ML_DK), F32),
                   jnp.zeros((bsz, 2, ML_HEADS, 8, LANES), F32))
    common = (w_state, w_aux, gdn_conv[l].astype(F32), alog_row, bias_row)

    _, (s_gdn,), (c_ml, n_ml, m_ml) = _mixer_states(
        ctx, row(norm1_w[l]), mod_c[:, 0], mod_c[:, 1], *common, zero_states, False)

    z, (o_f, o_b, _), (h_f, h_b, _, _, _) = _mixer_states(
        x, row(norm1_w[l]), mod_x[:, 0], mod_x[:, 1], *common, (s_gdn, c_ml, n_ml, m_ml), True, w_full)
    x1 = _merge_call(x, o_f, o_b, h_f, h_b, z, STATE_COLS // LANES,
                     row(jnp.tile(gdn_norm_w[l], GDN_HEADS)), row(ml_norm_w[l]),
                     w_branch_gdn[l].astype(BF16), w_branch_ml[l].astype(BF16), w_out[l].astype(BF16),
                     mod_x[:, 2])
    u = _nmm_call(x1, row(norm2_w[l]), mod_x[:, 3], mod_x[:, 4], w_up[l].astype(BF16), name="ffn_up")
    return _ffn2_call(u, ffn_conv[l].reshape(9, -1).astype(F32), w_down[l].astype(BF16), x1,
                      mod_x[:, 5], row(norm_out_w))
```

```python
import functools

import jax
import jax.numpy as jnp
from jax import lax
from jax.experimental import pallas as pl
from jax.experimental.pallas import tpu as pltpu

F32 = jnp.float32
BF16 = jnp.bfloat16

GDN_HEADS = 8
GDN_DK = 128
GDN_DV = 128
ML_HEADS = 4
ML_DK = 128
ML_DV = 256
CHUNK = 64
GATE_CAP = 15.0
GRID_W = 64
N_MOD = 6
EPS = 1e-6
LANES = 128
HALO = 16
NEG_BIG = -1e30

GDN_QK = GDN_HEADS * GDN_DK
GDN_V = GDN_HEADS * GDN_DV
ML_QK = ML_HEADS * ML_DK
ML_V = ML_HEADS * ML_DV
STATE_COLS = 2 * GDN_QK + GDN_V + 2 * ML_QK + ML_V
OUT_COLS = GDN_V + ML_V + 2 * 1024

VMEM_LIMIT = 48 * 1024 * 1024


def _cparams(*sem):
    return pltpu.CompilerParams(dimension_semantics=sem, vmem_limit_bytes=VMEM_LIMIT)


def _dot(a, b):
    return jnp.dot(a, b, preferred_element_type=F32)


def _dot_nt(a, b):
    return lax.dot_general(a, b, (((1,), (1,)), ((), ())), preferred_element_type=F32)


def _dot_tn(a, b):
    return lax.dot_general(a, b, (((0,), (0,)), ((), ())), preferred_element_type=F32)


def _sigmoid(x):
    return 1.0 / (1.0 + jnp.exp(-x))


def _softplus(x):
    return jnp.maximum(x, 0.0) + jnp.log1p(jnp.exp(-jnp.abs(x)))


def _pick_tile(n, pref):
    t = min(n, pref)
    while n % t:
        t //= 2
    return t


def _pick_cols(c, cap):
    t = cap - cap % LANES
    while c % t:
        t -= LANES
    return t


def _mod_kernel(c_ref, w_ref, b_ref, o_ref):
    c = c_ref[...]
    s = c * _sigmoid(c)
    o_ref[...] = _dot(s.astype(BF16), w_ref[...]) + b_ref[...]


def _mod_call(c_all, w_ada, b_ada):
    rows, d = c_all.shape
    n = w_ada.shape[1]
    tn = 1024
    return pl.pallas_call(
        _mod_kernel,
        grid=(n // tn,),
        in_specs=[
            pl.BlockSpec((rows, d), lambda j: (0, 0)),
            pl.BlockSpec((d, tn), lambda j: (0, j)),
            pl.BlockSpec((1, tn), lambda j: (0, j)),
        ],
        out_specs=pl.BlockSpec((rows, tn), lambda j: (0, j)),
        out_shape=jax.ShapeDtypeStruct((rows, n), F32),
        compiler_params=_cparams("parallel"),
        name="mod",
    )(c_all, w_ada, b_ada)


def _nmm_kernel(x_ref, nw_ref, sh_ref, sc_ref, w_ref, *rest, has_aux):
    if has_aux:
        wa_ref, o_ref, oa_ref, hn_ref = rest
    else:
        o_ref, hn_ref = rest

    @pl.when(pl.program_id(2) == 0)
    def _():
        x = x_ref[0]
        ms = jnp.mean(x * x, axis=-1, keepdims=True)
        y = x * lax.rsqrt(ms + EPS) * nw_ref[...]
        hb = (y * (1.0 + sc_ref[0]) + sh_ref[0]).astype(BF16)
        hn_ref[...] = hb
        if has_aux:
            oa_ref[0] = _dot(hb, wa_ref[...])

    o_ref[0] = _dot(hn_ref[...], w_ref[pl.program_id(2)]).astype(o_ref.dtype)


def _nmm_call(x, nw, shift, scale, w, w_aux=None, out_dtype=BF16, name="nmm"):
    b, t, d = x.shape
    c = w.shape[1]
    tm = _pick_tile(t, 1024)
    tn = _pick_cols(c, 1536)
    nj = c // tn
    has_aux = w_aux is not None
    w_tiles = w.reshape(d, nj, tn).transpose(1, 0, 2)
    in_specs = [
        pl.BlockSpec((1, tm, d), lambda bi, i, j: (bi, i, 0)),
        pl.BlockSpec((1, d), lambda bi, i, j: (0, 0)),
        pl.BlockSpec((1, 1, d), lambda bi, i, j: (bi, 0, 0)),
        pl.BlockSpec((1, 1, d), lambda bi, i, j: (bi, 0, 0)),
        pl.BlockSpec((nj, d, tn), lambda bi, i, j: (0, 0, 0), pipeline_mode=pl.Buffered(1)),
    ]
    out_specs = [pl.BlockSpec((1, tm, tn), lambda bi, i, j: (bi, i, j))]
    out_shape = [jax.ShapeDtypeStruct((b, t, c), out_dtype)]
    args = [x, nw, shift, scale, w_tiles]
    if has_aux:
        in_specs.append(pl.BlockSpec((d, LANES), lambda bi, i, j: (0, 0)))
        out_specs.append(pl.BlockSpec((1, tm, LANES), lambda bi, i, j: (bi, i, 0)))
        out_shape.append(jax.ShapeDtypeStruct((b, t, LANES), F32))
        args.append(w_aux)
    res = pl.pallas_call(
        functools.partial(_nmm_kernel, has_aux=has_aux),
        grid=(b, t // tm, c // tn),
        in_specs=in_specs,
        out_specs=out_specs,
        out_shape=out_shape,
        scratch_shapes=[pltpu.VMEM((tm, d), BF16)],
        compiler_params=_cparams("parallel", "parallel", "arbitrary"),
        name=name,
    )(*args)
    return res if has_aux else res[0]


G_A, G_B, G_I, G_F = 0, 16, 32, 40


def _chunk_scan(x, pos, backward, op, ident):
    rows = x.shape[0]
    yf, yb = x, x
    s = 1
    while s < CHUNK:
        yf = op(yf, jnp.where(pos >= s, pltpu.roll(yf, s, 0), ident))
        yb = op(yb, jnp.where(pos + s < CHUNK, pltpu.roll(yb, rows - s, 0), ident))
        s *= 2
    return jnp.where(backward, yb, yf)


def _gates_kernel(z_ref, alog_ref, bias_ref, o_ref):
    z = z_ref[0] + bias_ref[...]
    lane = lax.broadcasted_iota(jnp.int32, z.shape, 1)
    pos = lax.broadcasted_iota(jnp.int32, z.shape, 0) & (CHUNK - 1)
    backward = ((lane < G_I) & ((lane & 15) >= 8)) | ((lane >= G_I) & ((lane & 7) >= 4))

    log_decay = -jnp.exp(alog_ref[...]) * _softplus(z)
    beta = _sigmoid(z)
    capped = GATE_CAP * jnp.tanh(z * (1.0 / GATE_CAP))
    log_forget = -_softplus(-capped)

    summand = jnp.where(lane < G_B, log_decay, jnp.where(lane >= G_F, log_forget, 0.0))
    csum = _chunk_scan(summand, pos, backward, jnp.add, 0.0)
    o_ref[0, :, 0:LANES] = jnp.where(lane < G_B, csum, beta)

    bc = pltpu.roll(csum, LANES - (G_F - G_I), 1)
    backward_i = (lane & 7) >= 4
    u = capped - bc
    umax = _chunk_scan(u, pos, backward_i, jnp.maximum, NEG_BIG)
    o_ref[0, :, LANES:2 * LANES] = bc
    o_ref[0, :, 2 * LANES:3 * LANES] = u
    o_ref[0, :, 3 * LANES:4 * LANES] = bc + umax


def _gates_call(z_aux, alog_row, bias_row):
    b, t, _ = z_aux.shape
    tm = _pick_tile(t, 512)
    return pl.pallas_call(
        _gates_kernel,
        grid=(b, t // tm),
        in_specs=[
            pl.BlockSpec((1, tm, LANES), lambda bi, i: (bi, i, 0)),
            pl.BlockSpec((1, LANES), lambda bi, i: (0, 0)),
            pl.BlockSpec((1, LANES), lambda bi, i: (0, 0)),
        ],
        out_specs=pl.BlockSpec((1, tm, 4 * LANES), lambda bi, i: (bi, i, 0)),
        out_shape=jax.ShapeDtypeStruct((b, t, 4 * LANES), F32),
        compiler_params=_cparams("parallel", "parallel"),
        name="gates",
    )(z_aux, alog_row, bias_row)


def _conv_kernel(z_ref, zp_ref, zn_ref, w_ref, o_ref):
    i = pl.program_id(1)
    j = pl.program_id(2)
    z = z_ref[0].astype(F32)
    rows = z.shape[0]
    row = lax.broadcasted_iota(jnp.int32, z.shape, 0)
    prev_row = jnp.where(i == 0, 0.0, zp_ref[0, HALO - 1:HALO, :].astype(F32))
    next_row = jnp.where(i == pl.num_programs(1) - 1, 0.0, zn_ref[0, 0:1, :].astype(F32))
    z_prev = jnp.where(row == 0, prev_row, pltpu.roll(z, 1, 0))
    z_next = jnp.where(row == rows - 1, next_row, pltpu.roll(z, rows - 1, 0))
    y = z_prev * w_ref[0:1, :] + z * w_ref[1:2, :] + z_next * w_ref[2:3, :]
    y = y * _sigmoid(y)
    is_qk = j < 2
    for h in range(GDN_HEADS):
        cols = slice(h * GDN_DK, (h + 1) * GDN_DK)
        yh = y[:, cols]
        inv = lax.rsqrt(jnp.sum(yh * yh, axis=-1, keepdims=True) + EPS)
        o_ref[0, :, cols] = (yh * jnp.where(is_qk, inv, 1.0)).astype(o_ref.dtype)


def _conv_call(z, conv_w):
    b, t, _ = z.shape
    tt = _pick_tile(t, 512)
    g = GDN_QK
    nbh = tt // HALO
    return pl.pallas_call(
        _conv_kernel,
        grid=(b, t // tt, 3),
        in_specs=[
            pl.BlockSpec((1, tt, g), lambda bi, i, j: (bi, i, j)),
            pl.BlockSpec((1, HALO, g), lambda bi, i, j: (bi, jnp.maximum(i * nbh - 1, 0), j)),
            pl.BlockSpec((1, HALO, g), lambda bi, i, j: (bi, jnp.minimum((i + 1) * nbh, t // HALO - 1), j)),
            pl.BlockSpec((3, g), lambda bi, i, j: (0, j)),
        ],
        out_specs=pl.BlockSpec((1, tt, g), lambda bi, i, j: (bi, i, j)),
        out_shape=jax.ShapeDtypeStruct((b, t, 3 * g), BF16),
        compiler_params=_cparams("parallel", "parallel", "parallel"),
        name="gdn_conv",
    )(z, z, z, conv_w)


QUAD = 4
QW = QUAD * CHUNK
NQUAD = GDN_HEADS // QUAD
GDN_CHUNKS_PER_STEP = 2
ML_CHUNKS_PER_STEP = 2


def _quad_masks(reverse):
    ii = lax.broadcasted_iota(jnp.int32, (QW, QW), 0)
    jj = lax.broadcasted_iota(jnp.int32, (QW, QW), 1)
    same = (ii ^ jj) < CHUNK
    if reverse:
        return same & (jj > ii), same & (jj >= ii), ii == jj
    return same & (jj < ii), same & (jj <= ii), ii == jj


def _gdn_kernel(*refs, with_out):
    (qf, kf, vf, qb, kb, vb, gcf, gcb, grf, grb, s0_ref) = refs[:11]
    if with_out:
        of_ref, ob_ref, sfin_ref = refs[11:]
    else:
        (sfin_ref,) = refs[11:]
        of_ref = ob_ref = None

    @pl.when(pl.program_id(1) == 0)
    def _():
        sfin_ref[...] = s0_ref[...]

    scale = GDN_DK ** -0.5
    groups = []
    cs = grf.shape[1]
    for ci, d, g in [(ci, d, g) for ci in range(cs) for d in range(2) for g in range(NQUAD)]:
        q_ref, k_ref, v_ref, gc_ref, gr_ref, o_ref = (
            (qb, kb, vb, gcb, grb, ob_ref) if d else (qf, kf, vf, gcf, grf, of_ref))
        ch = cs - 1 - ci if d else ci
        rows = slice(ch * CHUNK, (ch + 1) * CHUNK)
        heads = tuple(range(g * QUAD, (g + 1) * QUAD))

        def stack(ref):
            return jnp.concatenate([ref[0, rows, h * GDN_DK:(h + 1) * GDN_DK] for h in heads], axis=0)

        def col(base):
            lanes = [base + d * GDN_HEADS + h for h in heads]
            return jnp.concatenate([gc_ref[0, rows, l:l + 1] for l in lanes], axis=0)

        r = d * NQUAD + g
        groups.append(dict(
            d=d, g=g, heads=heads, rows=rows, o_ref=o_ref, k=stack(k_ref), v=stack(v_ref),
            q=stack(q_ref) if with_out else None, gcol=col(G_A), bcol=col(G_B),
            grow=gr_ref[0, ch, r:r + 1, :], brow=gr_ref[0, ch, 2 * NQUAD + r:2 * NQUAD + r + 1, :]))

    for grp in groups:
        strict, incl, diag = _quad_masks(bool(grp["d"]))
        decay = jnp.exp(jnp.where(incl, grp["gcol"] - grp["grow"], NEG_BIG))
        kk = _dot_nt(grp["k"], grp["k"])
        x = jnp.where(strict, kk * decay * (-grp["bcol"]), 0.0)
        grp["x"] = x
        grp["s"] = jnp.where(diag, 1.0, x)
        if with_out:
            grp["attn"] = (_dot_nt(grp["q"], grp["k"]) * decay * scale).astype(BF16)
    for grp in groups:
        xb = grp["x"].astype(BF16)
        grp["xm"] = _dot(xb, xb)
    m = 2
    while 2 * m < CHUNK:
        for grp in groups:
            xb = grp["xm"].astype(BF16)
            both = _dot(jnp.concatenate([grp["s"].astype(BF16), xb], axis=0), xb)
            grp["s"] = grp["s"] + both[:QW]
            grp["xm"] = both[QW:]
        m *= 2
    for grp in groups:
        t = grp["s"] + _dot(grp["s"].astype(BF16), grp["xm"].astype(BF16))
        brow, grow = grp["brow"], grp["grow"]
        grp["u"] = _dot((t * brow).astype(BF16), grp["v"])
        grp["w"] = _dot((t * (brow * jnp.exp(grow))).astype(BF16), grp["k"])

    row = lax.broadcasted_iota(jnp.int32, (QW, GDN_DV), 0)
    for grp in groups:
        d, g, gcol = grp["d"], grp["g"], grp["gcol"]
        s4 = sfin_ref[0, d, g]
        sb = s4.astype(BF16)
        wb = grp["w"].astype(BF16)
        per_head = []
        for c in range(QUAD):
            lhs = wb[c * CHUNK:(c + 1) * CHUNK]
            if with_out:
                lhs = jnp.concatenate([lhs, grp["q"][c * CHUNK:(c + 1) * CHUNK]], axis=0)
            per_head.append(_dot(lhs, sb[:, c * GDN_DV:(c + 1) * GDN_DV]))

        def diag_blocks(base):
            return jnp.concatenate([r[base:base + CHUNK] for r in per_head], axis=0)

        v_new = grp["u"] - diag_blocks(0)
        end = 0 if d else CHUNK - 1
        g_end = [gcol[c * CHUNK + end:c * CHUNK + end + 1] for c in range(QUAD)]
        g_end_col = jnp.concatenate([jnp.broadcast_to(ge, (CHUNK, 1)) for ge in g_end], axis=0)
        vt = v_new * jnp.exp(g_end_col - gcol)
        vbd = jnp.concatenate(
            [jnp.where((row >= c * CHUNK) & (row < (c + 1) * CHUNK), vt, 0.0).astype(BF16) for c in range(QUAD)],
            axis=1)
        decay_lane = jnp.concatenate([jnp.broadcast_to(jnp.exp(ge), (1, GDN_DV)) for ge in g_end], axis=1)
        sfin_ref[0, d, g] = s4 * decay_lane + _dot_tn(grp["k"], vbd)
        if with_out:
            o = (jnp.exp(gcol) * scale) * diag_blocks(CHUNK) + _dot(grp["attn"], v_new.astype(BF16))
            for c, h in enumerate(grp["heads"]):
                grp["o_ref"][0, grp["rows"], h * GDN_DV:(h + 1) * GDN_DV] = o[c * CHUNK:(c + 1) * CHUNK].astype(BF16)


def _gdn_call(qkv, gcol, grow, s0, with_out):
    b, t, _ = qkv.shape
    cs = GDN_CHUNKS_PER_STEP
    blk = cs * CHUNK
    nb = t // blk
    w = GDN_QK

    def fwd(*tail):
        return lambda bi, n: (bi, n) + tail

    def bwd(*tail):
        return lambda bi, n: (bi, nb - 1 - n) + tail

    s_spec = pl.BlockSpec((1, 2, NQUAD, GDN_DK, QUAD * GDN_DV), lambda bi, n: (bi, 0, 0, 0, 0))
    in_specs = (
        [pl.BlockSpec((1, blk, w), fwd(o)) for o in range(3)]
        + [pl.BlockSpec((1, blk, w), bwd(o)) for o in range(3)]
        + [pl.BlockSpec((1, blk, LANES), fwd(0)), pl.BlockSpec((1, blk, LANES), bwd(0)),
           pl.BlockSpec((1, cs, 4 * NQUAD, QW), fwd(0, 0)), pl.BlockSpec((1, cs, 4 * NQUAD, QW), bwd(0, 0)),
           s_spec])
    s_shape = jax.ShapeDtypeStruct(s0.shape, F32)
    if with_out:
        out_specs = [pl.BlockSpec((1, blk, w), fwd(0)), pl.BlockSpec((1, blk, w), bwd(0)), s_spec]
        o_shape = jax.ShapeDtypeStruct((b, t, GDN_V), BF16)
        out_shape = [o_shape, o_shape, s_shape]
    else:
        out_specs = [s_spec]
        out_shape = [s_shape]
    return pl.pallas_call(
        functools.partial(_gdn_kernel, with_out=with_out),
        grid=(b, nb),
        in_specs=in_specs,
        out_specs=out_specs,
        out_shape=out_shape,
        compiler_params=_cparams("parallel", "arbitrary"),
        name="gdn_scan_out" if with_out else "gdn_scan_state",
    )(qkv, qkv, qkv, qkv, qkv, qkv, gcol, gcol, grow, grow, s0)


def _ml_kernel(*refs, with_out):
    (qf, kf, vf, qb, kb, vb, gcf, gcb, grf, grb, c0_ref, n0_ref, m0_ref) = refs[:13]
    if with_out:
        of_ref, ob_ref, cfin_ref, nfin_ref, mfin_ref = refs[13:]
    else:
        cfin_ref, nfin_ref, mfin_ref = refs[13:]
        of_ref = ob_ref = None

    @pl.when(pl.program_id(1) == 0)
    def _():
        cfin_ref[...] = c0_ref[...]
        nfin_ref[...] = n0_ref[...]
        mfin_ref[...] = m0_ref[...]

    scale = ML_DK ** -0.5
    heads = tuple(range(ML_HEADS))

    def rows_of(c):
        return slice(c * CHUNK, (c + 1) * CHUNK)

    def spread(xs):
        return jnp.concatenate([jnp.broadcast_to(x, (CHUNK, 1)) for x in xs], axis=0)

    groups = []
    cs = grf.shape[1]
    for ci, d in [(ci, d) for ci in range(cs) for d in range(2)]:
        q_ref, k_ref, v_ref, gc_ref, gr_ref, o_ref = (
            (qb, kb, vb, gcb, grb, ob_ref) if d else (qf, kf, vf, gcf, grf, of_ref))
        ch = cs - 1 - ci if d else ci
        rows = slice(ch * CHUNK, (ch + 1) * CHUNK)

        def stack(ref, width):
            return jnp.concatenate([ref[0, rows, h * width:(h + 1) * width] for h in heads], axis=0)

        def col(base):
            lanes = [base + G_I + d * ML_HEADS + h for h in heads]
            return jnp.concatenate([gc_ref[0, rows, l:l + 1] for l in lanes], axis=0)

        grp = dict(d=d, rows=rows, o_ref=o_ref, k=stack(k_ref, ML_DK), v=stack(v_ref, ML_DV),
                   bcol=col(LANES), ucol=col(2 * LANES), dmcol=col(3 * LANES), urow=gr_ref[0, ch, d:d + 1, :])
        if with_out:
            grp["q"] = stack(q_ref, ML_DK)
            grp["qk"] = _dot_nt(grp["q"], grp["k"])
        groups.append(grp)

    row = lax.broadcasted_iota(jnp.int32, (QW, ML_DV), 0)
    for grp in groups:
        d, k4, v4, bcol, ucol, dmcol = grp["d"], grp["k"], grp["v"], grp["bcol"], grp["ucol"], grp["dmcol"]
        end = 0 if d else CHUNK - 1
        m = [mfin_ref[0, d, h, 0:1, 0:1] for h in heads]
        n = [nfin_ref[0, d, h, 0:1, :] for h in heads]
        tot = [bcol[c * CHUNK + end:c * CHUNK + end + 1] for c in heads]
        m_new = [jnp.maximum(tot[c] + m[c], dmcol[c * CHUNK + end:c * CHUNK + end + 1]) for c in heads]
        dec = [jnp.exp(tot[c] + m[c] - m_new[c]) for c in heads]
        wt = jnp.exp(spread(tot) + ucol - spread(m_new))
        vw = v4.astype(F32) * wt
        vbd = jnp.concatenate(
            [jnp.where((row >= c * CHUNK) & (row < (c + 1) * CHUNK), vw, 0.0).astype(BF16) for c in heads], axis=1)
        c4 = cfin_ref[0, d]
        dec_lane = jnp.concatenate([jnp.broadcast_to(dec[c], (1, ML_DV)) for c in heads], axis=1)
        cfin_ref[0, d] = c4 * dec_lane + _dot_tn(k4, vbd)
        kw = k4.astype(F32) * wt
        for c in heads:
            n_next = dec[c] * n[c] + jnp.sum(kw[rows_of(c)], axis=0, keepdims=True)
            nfin_ref[0, d, c] = jnp.broadcast_to(n_next, (8, ML_DK))
            mfin_ref[0, d, c] = jnp.broadcast_to(m_new[c], (8, LANES))
        if not with_out:
            continue
        q4 = grp["q"]
        _, incl, _ = _quad_masks(bool(d))
        m4 = spread(m)
        m_t = jnp.maximum(bcol + m4, dmcol)
        inter = jnp.exp(bcol + m4 - m_t)
        p = grp["qk"] * scale * jnp.exp(jnp.where(incl, bcol + grp["urow"] - m_t, NEG_BIG))
        c4b = c4.astype(BF16)
        qc = jnp.concatenate([_dot(q4[rows_of(c)], c4b[:, c * ML_DV:(c + 1) * ML_DV]) for c in heads], axis=0)
        num = (inter * scale) * qc + _dot(p.astype(BF16), v4)
        n_rows = jnp.concatenate([jnp.broadcast_to(n[c], (CHUNK, ML_DK)) for c in heads], axis=0)
        qn = jnp.sum(q4.astype(F32) * n_rows, axis=1, keepdims=True) * scale
        den = inter * qn + jnp.sum(p, axis=1, keepdims=True)
        out = num * (1.0 / jnp.maximum(jnp.abs(den), jnp.exp(-m_t)))
        for c in heads:
            grp["o_ref"][0, grp["rows"], c * ML_DV:(c + 1) * ML_DV] = out[rows_of(c)].astype(BF16)


def _ml_call(z, col0, gates, urow, c0, n0, m0, with_out):
    b, t, _ = z.shape
    cs = ML_CHUNKS_PER_STEP
    blk = cs * CHUNK
    nb = t // blk
    qoff = col0 * LANES // ML_QK
    koff = (col0 * LANES + ML_QK) // ML_QK
    voff = (col0 * LANES + 2 * ML_QK) // ML_V

    def fwd(*tail):
        return lambda bi, n: (bi, n) + tail

    def bwd(*tail):
        return lambda bi, n: (bi, nb - 1 - n) + tail

    def state_spec(shape):
        return pl.BlockSpec((1,) + shape, lambda bi, n: (bi,) + (0,) * len(shape))

    st = [(2, ML_DK, ML_HEADS * ML_DV), (2, ML_HEADS, 8, ML_DK), (2, ML_HEADS, 8, LANES)]
    st_specs = [state_spec(s) for s in st]
    st_shapes = [jax.ShapeDtypeStruct((b,) + s, F32) for s in st]
    in_specs = (
        [pl.BlockSpec((1, blk, ML_QK), fwd(qoff)), pl.BlockSpec((1, blk, ML_QK), fwd(koff)),
         pl.BlockSpec((1, blk, ML_V), fwd(voff)),
         pl.BlockSpec((1, blk, ML_QK), bwd(qoff)), pl.BlockSpec((1, blk, ML_QK), bwd(koff)),
         pl.BlockSpec((1, blk, ML_V), bwd(voff)),
         pl.BlockSpec((1, blk, 4 * LANES), fwd(0)), pl.BlockSpec((1, blk, 4 * LANES), bwd(0)),
         pl.BlockSpec((1, cs, 8, QW), fwd(0, 0)), pl.BlockSpec((1, cs, 8, QW), bwd(0, 0))]
        + st_specs)
    if with_out:
        out_specs = [pl.BlockSpec((1, blk, ML_V), fwd(0)), pl.BlockSpec((1, blk, ML_V), bwd(0))] + st_specs
        o_shape = jax.ShapeDtypeStruct((b, t, ML_V), BF16)
        out_shape = [o_shape, o_shape] + st_shapes
    else:
        out_specs = st_specs
        out_shape = st_shapes
    return pl.pallas_call(
        functools.partial(_ml_kernel, with_out=with_out),
        grid=(b, nb),
        in_specs=in_specs,
        out_specs=out_specs,
        out_shape=out_shape,
        compiler_params=_cparams("parallel", "arbitrary"),
        name="mlstm_scan_out" if with_out else "mlstm_scan_state",
    )(z, z, z, z, z, z, gates, gates, urow, urow, c0, n0, m0)


def _head_rms(x, width):
    outs = []
    for h in range(x.shape[1] // width):
        xh = x[:, h * width:(h + 1) * width]
        ms = jnp.sum(xh * xh, axis=-1, keepdims=True) * (1.0 / width)
        outs.append(xh * lax.rsqrt(ms + EPS))
    return jnp.concatenate(outs, axis=-1)


def _merge_kernel(x_ref, of_ref, ob_ref, hf_ref, hb_ref, gz_ref, mo_ref, gg_ref, gm_ref,
                  gnw_ref, mnw_ref, wg_ref, wm_ref, wo_ref, gate_ref, o_ref):
    f32 = lambda ref: ref[0].astype(F32)
    og = _head_rms(f32(of_ref) + f32(ob_ref), GDN_DV) * gnw_ref[...]
    gz = f32(gz_ref)
    og = og * (gz * _sigmoid(gz))
    y_gdn = _dot(og.astype(BF16), wg_ref[...])
    hm = _head_rms(f32(hf_ref) + f32(hb_ref), ML_DV) * mnw_ref[...]
    hm = hm * _sigmoid(f32(mo_ref))
    y_ml = _dot(hm.astype(BF16), wm_ref[...])
    merged = _sigmoid(f32(gg_ref)) * y_gdn + _sigmoid(f32(gm_ref)) * y_ml
    x_mix = _dot(merged.astype(BF16), wo_ref[...])
    o_ref[0] = x_ref[0] + gate_ref[0] * x_mix


def _merge_call(x, o_f, o_b, h_f, h_b, z, zo_col0, gnw, mnw, wg, wm, wo, gate):
    b, t, d = x.shape
    tm = _pick_tile(t, 512)
    nz = zo_col0 * LANES // d

    def tok(bi, i):
        return (bi, i, 0)

    def zcol(k):
        return lambda bi, i: (bi, i, nz + k)

    tile = lambda imap: pl.BlockSpec((1, tm, d), imap)
    full = lambda shape: pl.BlockSpec(shape, lambda bi, i: (0,) * len(shape))
    return pl.pallas_call(
        _merge_kernel,
        grid=(b, t // tm),
        in_specs=[tile(tok), tile(tok), tile(tok), tile(tok), tile(tok),
                  tile(zcol(0)), tile(zcol(1)), tile(zcol(2)), tile(zcol(3)),
                  full((1, d)), full((1, d)), full((d, d)), full((d, d)), full((d, d)),
                  pl.BlockSpec((1, 1, d), lambda bi, i: (bi, 0, 0))],
        out_specs=tile(tok),
        out_shape=jax.ShapeDtypeStruct((b, t, d), F32),
        compiler_params=_cparams("parallel", "parallel"),
        name="merge",
    )(x, o_f, o_b, h_f, h_b, z, z, z, z, gnw, mnw, wg, wm, wo, gate)


def _ffn2_kernel(ug_ref, ugp_ref, ugn_ref, uv_ref, uvp_ref, uvn_ref, cwg_ref, cwv_ref, wd_ref,
                 x_ref, gate_ref, nw_ref, o_ref, acc_ref, act_ref):
    i = pl.program_id(1)
    j = pl.program_id(2)
    first = i == 0
    last = i == pl.num_programs(1) - 1
    n_img_rows = ug_ref.shape[1] // GRID_W
    col = lax.broadcasted_iota(jnp.int32, (GRID_W, LANES), 0)

    def row_conv(u_ref, up_ref, un_ref, cw_ref, cols):
        w = [cw_ref[k:k + 1, cols] for k in range(9)]
        cache = {}

        def strip(r):
            if r not in cache:
                if r < 0:
                    cache[r] = jnp.where(first, 0.0, up_ref[0, :, cols].astype(F32))
                elif r == n_img_rows:
                    cache[r] = jnp.where(last, 0.0, un_ref[0, :, cols].astype(F32))
                else:
                    cache[r] = u_ref[0, r * GRID_W:(r + 1) * GRID_W, cols].astype(F32)
            return cache[r]

        def conv(r):
            taps = [strip(r - 1), strip(r), strip(r + 1)]
            side = [taps[0] * w[dc] + taps[1] * w[3 + dc] + taps[2] * w[6 + dc] for dc in range(3)]
            left = jnp.where(col == 0, 0.0, pltpu.roll(side[0], 1, 0))
            right = jnp.where(col == GRID_W - 1, 0.0, pltpu.roll(side[2], GRID_W - 1, 0))
            return side[1] + left + right

        return conv

    for cb in range(ug_ref.shape[2] // LANES):
        cols = slice(cb * LANES, (cb + 1) * LANES)
        conv_g = row_conv(ug_ref, ugp_ref, ugn_ref, cwg_ref, cols)
        conv_v = row_conv(uv_ref, uvp_ref, uvn_ref, cwv_ref, cols)
        for r in range(n_img_rows):
            g = conv_g(r)
            act_ref[r * GRID_W:(r + 1) * GRID_W, cols] = (g * _sigmoid(g) * conv_v(r)).astype(BF16)
    part = _dot(act_ref[...], wd_ref[j])

    @pl.when(j == 0)
    def _():
        acc_ref[...] = part

    @pl.when(j > 0)
    def _():
        acc_ref[...] += part

    @pl.when(j == pl.num_programs(2) - 1)
    def _():
        x = x_ref[0] + gate_ref[0] * acc_ref[...]
        ms = jnp.mean(x * x, axis=-1, keepdims=True)
        o_ref[0] = x * lax.rsqrt(ms + EPS) * nw_ref[...]


def _ffn2_call(u, conv_w, w_down, x, gate, nw):
    b, t, d = x.shape
    f = w_down.shape[0]
    tm = _pick_tile(t, 1024)
    tc = 256
    nj = f // tc
    w_tiles = w_down.reshape(nj, tc, d)
    rpt = tm // GRID_W
    nrows = t // GRID_W

    def main(off):
        return pl.BlockSpec((1, tm, tc), lambda bi, i, j: (bi, i, off + j))

    def prev(off):
        return pl.BlockSpec((1, GRID_W, tc), lambda bi, i, j: (bi, jnp.maximum(i * rpt - 1, 0), off + j))

    def nxt(off):
        return pl.BlockSpec((1, GRID_W, tc), lambda bi, i, j: (bi, jnp.minimum((i + 1) * rpt, nrows - 1), off + j))

    return pl.pallas_call(
        _ffn2_kernel,
        grid=(b, t // tm, nj),
        in_specs=[main(0), prev(0), nxt(0), main(nj), prev(nj), nxt(nj),
                  pl.BlockSpec((9, tc), lambda bi, i, j: (0, j)),
                  pl.BlockSpec((9, tc), lambda bi, i, j: (0, nj + j)),
                  pl.BlockSpec((nj, tc, d), lambda bi, i, j: (0, 0, 0), pipeline_mode=pl.Buffered(1)),
                  pl.BlockSpec((1, tm, d), lambda bi, i, j: (bi, i, 0)),
                  pl.BlockSpec((1, 1, d), lambda bi, i, j: (bi, 0, 0)),
                  pl.BlockSpec((1, d), lambda bi, i, j: (0, 0))],
        out_specs=pl.BlockSpec((1, tm, d), lambda bi, i, j: (bi, i, 0)),
        out_shape=jax.ShapeDtypeStruct((b, t, d), F32),
        scratch_shapes=[pltpu.VMEM((tm, d), F32), pltpu.VMEM((tm, tc), BF16)],
        compiler_params=_cparams("parallel", "parallel", "arbitrary"),
        name="ffn2",
    )(u, u, u, u, u, u, conv_w, conv_w, w_tiles, x, gate, nw)


def _gdn_gate_rows(gates):
    b, t, _ = gates.shape
    nc = t // CHUNK

    def rows(base):
        q = gates[:, :, base:base + 2 * GDN_HEADS].reshape(b, nc, CHUNK, 2, NQUAD, QUAD)
        return q.transpose(0, 1, 3, 4, 5, 2).reshape(b, nc, 2 * NQUAD, QW)

    return jnp.concatenate([rows(G_A), rows(G_B)], axis=2)


def _ml_gate_rows(gates):
    b, t, _ = gates.shape
    nc = t // CHUNK
    base = 2 * LANES + G_I
    u = gates[:, :, base:base + 2 * ML_HEADS].reshape(b, nc, CHUNK, 2, ML_HEADS)
    u = u.transpose(0, 1, 3, 4, 2).reshape(b, nc, 2, QW)
    return jnp.pad(u, ((0, 0), (0, 0), (0, 6), (0, 0)))


def _mixer_states(x_seq, nw, shift, scale, w_state, w_aux, conv_w, alog_row, bias_row, states,
                  with_out, w_full=None):
    w = w_full if with_out else w_state
    z, z_aux = _nmm_call(x_seq, nw, shift, scale, w, w_aux, name="in_proj")
    gates = _gates_call(z_aux, alog_row, bias_row)
    qkv = _conv_call(z, conv_w)
    s_gdn, c_ml, n_ml, m_ml = states
    gdn_res = _gdn_call(qkv, gates[:, :, :LANES], _gdn_gate_rows(gates), s_gdn, with_out)
    ml_col0 = (2 * GDN_QK + GDN_V) // LANES
    ml_res = _ml_call(z, ml_col0, gates, _ml_gate_rows(gates), c_ml, n_ml, m_ml, with_out)
    return z, gdn_res, ml_res


def kernel(x, c, ctx, c_ctx, w_ada, b_ada, norm1_w, w_in, gdn_conv, gdn_a_log, gdn_dt_bias, gdn_norm_w,
           ml_igate_b, ml_fgate_b, ml_norm_w, w_branch_gdn, w_branch_ml, w_out, norm2_w, w_up, ffn_conv,
           w_down, norm_out_w):
    bsz, _, d = x.shape
    depth = w_ada.shape[0]
    assert depth == 1, "single-layer problem: the context stream is never updated"
    l = 0

    sizes = (2 * GDN_QK + GDN_V, 2 * GDN_HEADS, 2 * GDN_HEADS, ML_QK, ML_QK, ML_V, 2 * ML_HEADS, 2 * ML_HEADS,
             GDN_V, ML_V, d, d)
    offs = [0]
    for s in sizes:
        offs.append(offs[-1] + s)
    wi = w_in[l]
    seg = lambda k: wi[:, offs[k]:offs[k + 1]]
    w_state = jnp.concatenate([seg(0), seg(3), seg(4), seg(5)], axis=1).astype(BF16)
    w_full = jnp.concatenate([seg(0), seg(3), seg(4), seg(5), seg(8), seg(9), seg(10), seg(11)], axis=1).astype(BF16)
    n_gate = 4 * GDN_HEADS + 4 * ML_HEADS
    w_aux = jnp.concatenate([seg(1), seg(2), seg(6), seg(7), jnp.zeros((d, LANES - n_gate), F32)], axis=1).astype(BF16)
    pad = lambda v, n: jnp.pad(v.reshape(1, -1).astype(F32), ((0, 0), (0, n - v.size)))
    alog_row = pad(gdn_a_log[l], LANES)
    bias_row = pad(jnp.concatenate([gdn_dt_bias[l].reshape(-1), jnp.zeros((2 * GDN_HEADS,), F32),
                                    ml_igate_b[l].reshape(-1), ml_fgate_b[l].reshape(-1)]), LANES)
    row = lambda v: v.reshape(1, -1).astype(F32)

    c_all = jnp.concatenate([c, c_ctx[None], jnp.zeros((8 - bsz - 1, d), F32)], axis=0)
    mods = _mod_call(c_all, w_ada[l].astype(BF16), row(b_ada[l]))
    mod_x = mods[:bsz].reshape(bsz, N_MOD, 1, d)
    mod_c = jnp.broadcast_to(mods[bsz].reshape(1, N_MOD, 1, d), (bsz, N_MOD, 1, d))

    zero_states = (jnp.zeros((bsz, 2, NQUAD, GDN_DK, QUAD * GDN_DV), F32),
                   jnp.zeros((bsz, 2, ML_DK, ML_HEADS * ML_DV), F32),
                   jnp.zeros((bsz, 2, ML_HEADS, 8, ML_DK), F32),
                   jnp.zeros((bsz, 2, ML_HEADS, 8, LANES), F32))
    common = (w_state, w_aux, gdn_conv[l].astype(F32), alog_row, bias_row)

    _, (s_gdn,), (c_ml, n_ml, m_ml) = _mixer_states(
        ctx, row(norm1_w[l]), mod_c[:, 0], mod_c[:, 1], *common, zero_states, False)

    z, (o_f, o_b, _), (h_f, h_b, _, _, _) = _mixer_states(
        x, row(norm1_w[l]), mod_x[:, 0], mod_x[:, 1], *common, (s_gdn, c_ml, n_ml, m_ml), True, w_full)
    x1 = _merge_call(x, o_f, o_b, h_f, h_b, z, STATE_COLS // LANES,
                     row(jnp.tile(gdn_norm_w[l], GDN_HEADS)), row(ml_norm_w[l]),
                     w_branch_gdn[l].astype(BF16), w_branch_ml[l].astype(BF16), w_out[l].astype(BF16),
                     mod_x[:, 2])
    u = _nmm_call(x1, row(norm2_w[l]), mod_x[:, 3], mod_x[:, 4], w_up[l].astype(BF16), name="ffn_up")
    return _ffn2_call(u, ffn_conv[l].reshape(9, -1).astype(F32), w_down[l].astype(BF16), x1,
                      mod_x[:, 5], row(norm_out_w))
```

```python
import functools

import jax
import jax.numpy as jnp
from jax import lax
from jax.experimental import pallas as pl
from jax.experimental.pallas import tpu as pltpu

F32 = jnp.float32
BF16 = jnp.bfloat16

GDN_HEADS = 8
GDN_DK = 128
GDN_DV = 128
ML_HEADS = 4
ML_DK = 128
ML_DV = 256
CHUNK = 64
GATE_CAP = 15.0
GRID_W = 64
N_MOD = 6
EPS = 1e-6
LANES = 128
HALO = 16
NEG_BIG = -1e30

GDN_QK = GDN_HEADS * GDN_DK
GDN_V = GDN_HEADS * GDN_DV
ML_QK = ML_HEADS * ML_DK
ML_V = ML_HEADS * ML_DV
STATE_COLS = 2 * GDN_QK + GDN_V + 2 * ML_QK + ML_V
OUT_COLS = GDN_V + ML_V + 2 * 1024

VMEM_LIMIT = 48 * 1024 * 1024
FFN_VMEM_LIMIT = 56 * 1024 * 1024
FFN_ROW_SPLIT = 4


def _cparams(*sem):
    return pltpu.CompilerParams(dimension_semantics=sem, vmem_limit_bytes=VMEM_LIMIT)


def _dot(a, b):
    return jnp.dot(a, b, preferred_element_type=F32)


def _dot_nt(a, b):
    return lax.dot_general(a, b, (((1,), (1,)), ((), ())), preferred_element_type=F32)


def _dot_tn(a, b):
    return lax.dot_general(a, b, (((0,), (0,)), ((), ())), preferred_element_type=F32)


def _sigmoid(x):
    return 1.0 / (1.0 + jnp.exp(-x))


def _softplus(x):
    return jnp.maximum(x, 0.0) + jnp.log1p(jnp.exp(-jnp.abs(x)))


def _pick_tile(n, pref):
    t = min(n, pref)
    while n % t:
        t //= 2
    return t


def _pick_cols(c, cap):
    t = cap - cap % LANES
    while c % t:
        t -= LANES
    return t


def _mod_kernel(c_ref, w_ref, b_ref, o_ref):
    c = c_ref[...]
    s = c * _sigmoid(c)
    o_ref[...] = _dot(s.astype(BF16), w_ref[...]) + b_ref[...]


def _mod_call(c_all, w_ada, b_ada):
    rows, d = c_all.shape
    n = w_ada.shape[1]
    tn = 1024
    return pl.pallas_call(
        _mod_kernel,
        grid=(n // tn,),
        in_specs=[
            pl.BlockSpec((rows, d), lambda j: (0, 0)),
            pl.BlockSpec((d, tn), lambda j: (0, j)),
            pl.BlockSpec((1, tn), lambda j: (0, j)),
        ],
        out_specs=pl.BlockSpec((rows, tn), lambda j: (0, j)),
        out_shape=jax.ShapeDtypeStruct((rows, n), F32),
        compiler_params=_cparams("parallel"),
        name="mod",
    )(c_all, w_ada, b_ada)


def _nmm_kernel(x_ref, nw_ref, sh_ref, sc_ref, w_ref, *rest, has_aux):
    if has_aux:
        wa_ref, o_ref, oa_ref, hn_ref = rest
    else:
        o_ref, hn_ref = rest

    @pl.when(pl.program_id(2) == 0)
    def _():
        x = x_ref[0]
        ms = jnp.mean(x * x, axis=-1, keepdims=True)
        y = x * lax.rsqrt(ms + EPS) * nw_ref[...]
        hb = (y * (1.0 + sc_ref[0]) + sh_ref[0]).astype(BF16)
        hn_ref[...] = hb
        if has_aux:
            oa_ref[0] = _dot(hb, wa_ref[...])

    o_ref[0] = _dot(hn_ref[...], w_ref[pl.program_id(2)]).astype(o_ref.dtype)


def _nmm_call(x, nw, shift, scale, w, w_aux=None, out_dtype=BF16, name="nmm"):
    b, t, d = x.shape
    c = w.shape[1]
    tm = _pick_tile(t, 1024)
    tn = _pick_cols(c, 1536)
    nj = c // tn
    has_aux = w_aux is not None
    w_tiles = w.reshape(d, nj, tn).transpose(1, 0, 2)
    in_specs = [
        pl.BlockSpec((1, tm, d), lambda bi, i, j: (bi, i, 0)),
        pl.BlockSpec((1, d), lambda bi, i, j: (0, 0)),
        pl.BlockSpec((1, 1, d), lambda bi, i, j: (bi, 0, 0)),
        pl.BlockSpec((1, 1, d), lambda bi, i, j: (bi, 0, 0)),
        pl.BlockSpec((nj, d, tn), lambda bi, i, j: (0, 0, 0), pipeline_mode=pl.Buffered(1)),
    ]
    out_specs = [pl.BlockSpec((1, tm, tn), lambda bi, i, j: (bi, i, j))]
    out_shape = [jax.ShapeDtypeStruct((b, t, c), out_dtype)]
    args = [x, nw, shift, scale, w_tiles]
    if has_aux:
        in_specs.append(pl.BlockSpec((d, LANES), lambda bi, i, j: (0, 0)))
        out_specs.append(pl.BlockSpec((1, tm, LANES), lambda bi, i, j: (bi, i, 0)))
        out_shape.append(jax.ShapeDtypeStruct((b, t, LANES), F32))
        args.append(w_aux)
    res = pl.pallas_call(
        functools.partial(_nmm_kernel, has_aux=has_aux),
        grid=(b, t // tm, c // tn),
        in_specs=in_specs,
        out_specs=out_specs,
        out_shape=out_shape,
        scratch_shapes=[pltpu.VMEM((tm, d), BF16)],
        compiler_params=_cparams("parallel", "parallel", "arbitrary"),
        name=name,
    )(*args)
    return res if has_aux else res[0]


G_A, G_B, G_I, G_F = 0, 16, 32, 40


def _chunk_scan(x, pos, backward, op, ident):
    rows = x.shape[0]
    yf, yb = x, x
    s = 1
    while s < CHUNK:
        yf = op(yf, jnp.where(pos >= s, pltpu.roll(yf, s, 0), ident))
        yb = op(yb, jnp.where(pos + s < CHUNK, pltpu.roll(yb, rows - s, 0), ident))
        s *= 2
    return jnp.where(backward, yb, yf)


def _gates_kernel(z_ref, alog_ref, bias_ref, o_ref):
    z = z_ref[0] + bias_ref[...]
    lane = lax.broadcasted_iota(jnp.int32, z.shape, 1)
    pos = lax.broadcasted_iota(jnp.int32, z.shape, 0) & (CHUNK - 1)
    backward = ((lane < G_I) & ((lane & 15) >= 8)) | ((lane >= G_I) & ((lane & 7) >= 4))

    log_decay = -jnp.exp(alog_ref[...]) * _softplus(z)
    beta = _sigmoid(z)
    capped = GATE_CAP * jnp.tanh(z * (1.0 / GATE_CAP))
    log_forget = -_softplus(-capped)

    summand = jnp.where(lane < G_B, log_decay, jnp.where(lane >= G_F, log_forget, 0.0))
    csum = _chunk_scan(summand, pos, backward, jnp.add, 0.0)
    o_ref[0, :, 0:LANES] = jnp.where(lane < G_B, csum, beta)

    bc = pltpu.roll(csum, LANES - (G_F - G_I), 1)
    backward_i = (lane & 7) >= 4
    u = capped - bc
    umax = _chunk_scan(u, pos, backward_i, jnp.maximum, NEG_BIG)
    o_ref[0, :, LANES:2 * LANES] = bc
    o_ref[0, :, 2 * LANES:3 * LANES] = u
    o_ref[0, :, 3 * LANES:4 * LANES] = bc + umax


def _gates_call(z_aux, alog_row, bias_row):
    b, t, _ = z_aux.shape
    tm = _pick_tile(t, 512)
    return pl.pallas_call(
        _gates_kernel,
        grid=(b, t // tm),
        in_specs=[
            pl.BlockSpec((1, tm, LANES), lambda bi, i: (bi, i, 0)),
            pl.BlockSpec((1, LANES), lambda bi, i: (0, 0)),
            pl.BlockSpec((1, LANES), lambda bi, i: (0, 0)),
        ],
        out_specs=pl.BlockSpec((1, tm, 4 * LANES), lambda bi, i: (bi, i, 0)),
        out_shape=jax.ShapeDtypeStruct((b, t, 4 * LANES), F32),
        compiler_params=_cparams("parallel", "parallel"),
        name="gates",
    )(z_aux, alog_row, bias_row)


def _conv_kernel(z_ref, zp_ref, zn_ref, w_ref, o_ref):
    i = pl.program_id(1)
    j = pl.program_id(2)
    z = z_ref[0].astype(F32)
    rows = z.shape[0]
    row = lax.broadcasted_iota(jnp.int32, z.shape, 0)
    prev_row = jnp.where(i == 0, 0.0, zp_ref[0, HALO - 1:HALO, :].astype(F32))
    next_row = jnp.where(i == pl.num_programs(1) - 1, 0.0, zn_ref[0, 0:1, :].astype(F32))
    z_prev = jnp.where(row == 0, prev_row, pltpu.roll(z, 1, 0))
    z_next = jnp.where(row == rows - 1, next_row, pltpu.roll(z, rows - 1, 0))
    y = z_prev * w_ref[0:1, :] + z * w_ref[1:2, :] + z_next * w_ref[2:3, :]
    y = y * _sigmoid(y)
    is_qk = j < 2
    for h in range(GDN_HEADS):
        cols = slice(h * GDN_DK, (h + 1) * GDN_DK)
        yh = y[:, cols]
        inv = lax.rsqrt(jnp.sum(yh * yh, axis=-1, keepdims=True) + EPS)
        o_ref[0, :, cols] = (yh * jnp.where(is_qk, inv, 1.0)).astype(o_ref.dtype)


def _conv_call(z, conv_w):
    b, t, _ = z.shape
    tt = _pick_tile(t, 512)
    g = GDN_QK
    nbh = tt // HALO
    return pl.pallas_call(
        _conv_kernel,
        grid=(b, t // tt, 3),
        in_specs=[
            pl.BlockSpec((1, tt, g), lambda bi, i, j: (bi, i, j)),
            pl.BlockSpec((1, HALO, g), lambda bi, i, j: (bi, jnp.maximum(i * nbh - 1, 0), j)),
            pl.BlockSpec((1, HALO, g), lambda bi, i, j: (bi, jnp.minimum((i + 1) * nbh, t // HALO - 1), j)),
            pl.BlockSpec((3, g), lambda bi, i, j: (0, j)),
        ],
        out_specs=pl.BlockSpec((1, tt, g), lambda bi, i, j: (bi, i, j)),
        out_shape=jax.ShapeDtypeStruct((b, t, 3 * g), BF16),
        compiler_params=_cparams("parallel", "parallel", "parallel"),
        name="gdn_conv",
    )(z, z, z, conv_w)


QUAD = 4
QW = QUAD * CHUNK
NQUAD = GDN_HEADS // QUAD
GDN_CHUNKS_PER_STEP = 2
ML_CHUNKS_PER_STEP = 2


def _quad_masks(reverse):
    ii = lax.broadcasted_iota(jnp.int32, (QW, QW), 0)
    jj = lax.broadcasted_iota(jnp.int32, (QW, QW), 1)
    same = (ii ^ jj) < CHUNK
    if reverse:
        return same & (jj > ii), same & (jj >= ii), ii == jj
    return same & (jj < ii), same & (jj <= ii), ii == jj


def _gdn_kernel(*refs, with_out):
    (qf, kf, vf, qb, kb, vb, gcf, gcb, grf, grb, s0_ref) = refs[:11]
    if with_out:
        of_ref, ob_ref, sfin_ref = refs[11:]
    else:
        (sfin_ref,) = refs[11:]
        of_ref = ob_ref = None

    @pl.when(pl.program_id(1) == 0)
    def _():
        sfin_ref[...] = s0_ref[...]

    scale = GDN_DK ** -0.5
    groups = []
    cs = grf.shape[1]
    for ci, d, g in [(ci, d, g) for ci in range(cs) for d in range(2) for g in range(NQUAD)]:
        q_ref, k_ref, v_ref, gc_ref, gr_ref, o_ref = (
            (qb, kb, vb, gcb, grb, ob_ref) if d else (qf, kf, vf, gcf, grf, of_ref))
        ch = cs - 1 - ci if d else ci
        rows = slice(ch * CHUNK, (ch + 1) * CHUNK)
        heads = tuple(range(g * QUAD, (g + 1) * QUAD))

        def stack(ref):
            return jnp.concatenate([ref[0, rows, h * GDN_DK:(h + 1) * GDN_DK] for h in heads], axis=0)

        def col(base):
            lanes = [base + d * GDN_HEADS + h for h in heads]
            return jnp.concatenate([gc_ref[0, rows, l:l + 1] for l in lanes], axis=0)

        r = d * NQUAD + g
        groups.append(dict(
            d=d, g=g, heads=heads, rows=rows, o_ref=o_ref, k=stack(k_ref), v=stack(v_ref),
            q=stack(q_ref) if with_out else None, gcol=col(G_A), bcol=col(G_B),
            grow=gr_ref[0, ch, r:r + 1, :], brow=gr_ref[0, ch, 2 * NQUAD + r:2 * NQUAD + r + 1, :]))

    for grp in groups:
        strict, incl, diag = _quad_masks(bool(grp["d"]))
        decay = jnp.exp(jnp.where(incl, grp["gcol"] - grp["grow"], NEG_BIG))
        kk = _dot_nt(grp["k"], grp["k"])
        x = jnp.where(strict, kk * decay * (-grp["bcol"]), 0.0)
        grp["x"] = x
        grp["s"] = jnp.where(diag, 1.0, x)
        if with_out:
            grp["attn"] = (_dot_nt(grp["q"], grp["k"]) * decay * scale).astype(BF16)
    for grp in groups:
        xb = grp["x"].astype(BF16)
        grp["xm"] = _dot(xb, xb)
    m = 2
    while 2 * m < CHUNK:
        for grp in groups:
            xb = grp["xm"].astype(BF16)
            both = _dot(jnp.concatenate([grp["s"].astype(BF16), xb], axis=0), xb)
            grp["s"] = grp["s"] + both[:QW]
            grp["xm"] = both[QW:]
        m *= 2
    for grp in groups:
        t = grp["s"] + _dot(grp["s"].astype(BF16), grp["xm"].astype(BF16))
        brow, grow = grp["brow"], grp["grow"]
        grp["u"] = _dot((t * brow).astype(BF16), grp["v"])
        grp["w"] = _dot((t * (brow * jnp.exp(grow))).astype(BF16), grp["k"])

    row = lax.broadcasted_iota(jnp.int32, (QW, GDN_DV), 0)
    for grp in groups:
        d, g, gcol = grp["d"], grp["g"], grp["gcol"]
        s4 = sfin_ref[0, d, g]
        sb = s4.astype(BF16)
        wb = grp["w"].astype(BF16)
        per_head = []
        for c in range(QUAD):
            lhs = wb[c * CHUNK:(c + 1) * CHUNK]
            if with_out:
                lhs = jnp.concatenate([lhs, grp["q"][c * CHUNK:(c + 1) * CHUNK]], axis=0)
            per_head.append(_dot(lhs, sb[:, c * GDN_DV:(c + 1) * GDN_DV]))

        def diag_blocks(base):
            return jnp.concatenate([r[base:base + CHUNK] for r in per_head], axis=0)

        v_new = grp["u"] - diag_blocks(0)
        end = 0 if d else CHUNK - 1
        g_end = [gcol[c * CHUNK + end:c * CHUNK + end + 1] for c in range(QUAD)]
        g_end_col = jnp.concatenate([jnp.broadcast_to(ge, (CHUNK, 1)) for ge in g_end], axis=0)
        vt = v_new * jnp.exp(g_end_col - gcol)
        vbd = jnp.concatenate(
            [jnp.where((row >= c * CHUNK) & (row < (c + 1) * CHUNK), vt, 0.0).astype(BF16) for c in range(QUAD)],
            axis=1)
        decay_lane = jnp.concatenate([jnp.broadcast_to(jnp.exp(ge), (1, GDN_DV)) for ge in g_end], axis=1)
        sfin_ref[0, d, g] = s4 * decay_lane + _dot_tn(grp["k"], vbd)
        if with_out:
            o = (jnp.exp(gcol) * scale) * diag_blocks(CHUNK) + _dot(grp["attn"], v_new.astype(BF16))
            for c, h in enumerate(grp["heads"]):
                grp["o_ref"][0, grp["rows"], h * GDN_DV:(h + 1) * GDN_DV] = o[c * CHUNK:(c + 1) * CHUNK].astype(BF16)


def _gdn_call(qkv, gcol, grow, s0, with_out):
    b, t, _ = qkv.shape
    cs = GDN_CHUNKS_PER_STEP
    blk = cs * CHUNK
    nb = t // blk
    w = GDN_QK

    def fwd(*tail):
        return lambda bi, n: (bi, n) + tail

    def bwd(*tail):
        return lambda bi, n: (bi, nb - 1 - n) + tail

    s_spec = pl.BlockSpec((1, 2, NQUAD, GDN_DK, QUAD * GDN_DV), lambda bi, n: (bi, 0, 0, 0, 0))
    in_specs = (
        [pl.BlockSpec((1, blk, w), fwd(o)) for o in range(3)]
        + [pl.BlockSpec((1, blk, w), bwd(o)) for o in range(3)]
        + [pl.BlockSpec((1, blk, LANES), fwd(0)), pl.BlockSpec((1, blk, LANES), bwd(0)),
           pl.BlockSpec((1, cs, 4 * NQUAD, QW), fwd(0, 0)), pl.BlockSpec((1, cs, 4 * NQUAD, QW), bwd(0, 0)),
           s_spec])
    s_shape = jax.ShapeDtypeStruct(s0.shape, F32)
    if with_out:
        out_specs = [pl.BlockSpec((1, blk, w), fwd(0)), pl.BlockSpec((1, blk, w), bwd(0)), s_spec]
        o_shape = jax.ShapeDtypeStruct((b, t, GDN_V), BF16)
        out_shape = [o_shape, o_shape, s_shape]
    else:
        out_specs = [s_spec]
        out_shape = [s_shape]
    return pl.pallas_call(
        functools.partial(_gdn_kernel, with_out=with_out),
        grid=(b, nb),
        in_specs=in_specs,
        out_specs=out_specs,
        out_shape=out_shape,
        compiler_params=_cparams("parallel", "arbitrary"),
        name="gdn_scan_out" if with_out else "gdn_scan_state",
    )(qkv, qkv, qkv, qkv, qkv, qkv, gcol, gcol, grow, grow, s0)


def _ml_kernel(*refs, with_out):
    (qf, kf, vf, qb, kb, vb, gcf, gcb, grf, grb, c0_ref, n0_ref, m0_ref) = refs[:13]
    if with_out:
        of_ref, ob_ref, cfin_ref, nfin_ref, mfin_ref = refs[13:]
    else:
        cfin_ref, nfin_ref, mfin_ref = refs[13:]
        of_ref = ob_ref = None

    @pl.when(pl.program_id(1) == 0)
    def _():
        cfin_ref[...] = c0_ref[...]
        nfin_ref[...] = n0_ref[...]
        mfin_ref[...] = m0_ref[...]

    scale = ML_DK ** -0.5
    heads = tuple(range(ML_HEADS))

    def rows_of(c):
        return slice(c * CHUNK, (c + 1) * CHUNK)

    def spread(xs):
        return jnp.concatenate([jnp.broadcast_to(x, (CHUNK, 1)) for x in xs], axis=0)

    groups = []
    cs = grf.shape[1]
    for ci, d in [(ci, d) for ci in range(cs) for d in range(2)]:
        q_ref, k_ref, v_ref, gc_ref, gr_ref, o_ref = (
            (qb, kb, vb, gcb, grb, ob_ref) if d else (qf, kf, vf, gcf, grf, of_ref))
        ch = cs - 1 - ci if d else ci
        rows = slice(ch * CHUNK, (ch + 1) * CHUNK)

        def stack(ref, width):
            return jnp.concatenate([ref[0, rows, h * width:(h + 1) * width] for h in heads], axis=0)

        def col(base):
            lanes = [base + G_I + d * ML_HEADS + h for h in heads]
            return jnp.concatenate([gc_ref[0, rows, l:l + 1] for l in lanes], axis=0)

        grp = dict(d=d, rows=rows, o_ref=o_ref, k=stack(k_ref, ML_DK), v=stack(v_ref, ML_DV),
                   bcol=col(LANES), ucol=col(2 * LANES), dmcol=col(3 * LANES), urow=gr_ref[0, ch, d:d + 1, :])
        if with_out:
            grp["q"] = stack(q_ref, ML_DK)
            grp["qk"] = _dot_nt(grp["q"], grp["k"])
        groups.append(grp)

    row = lax.broadcasted_iota(jnp.int32, (QW, ML_DV), 0)
    for grp in groups:
        d, k4, v4, bcol, ucol, dmcol = grp["d"], grp["k"], grp["v"], grp["bcol"], grp["ucol"], grp["dmcol"]
        end = 0 if d else CHUNK - 1
        m = [mfin_ref[0, d, h, 0:1, 0:1] for h in heads]
        n = [nfin_ref[0, d, h, 0:1, :] for h in heads]
        tot = [bcol[c * CHUNK + end:c * CHUNK + end + 1] for c in heads]
        m_new = [jnp.maximum(tot[c] + m[c], dmcol[c * CHUNK + end:c * CHUNK + end + 1]) for c in heads]
        dec = [jnp.exp(tot[c] + m[c] - m_new[c]) for c in heads]
        wt = jnp.exp(spread(tot) + ucol - spread(m_new))
        vw = v4.astype(F32) * wt
        vbd = jnp.concatenate(
            [jnp.where((row >= c * CHUNK) & (row < (c + 1) * CHUNK), vw, 0.0).astype(BF16) for c in heads], axis=1)
        c4 = cfin_ref[0, d]
        dec_lane = jnp.concatenate([jnp.broadcast_to(dec[c], (1, ML_DV)) for c in heads], axis=1)
        cfin_ref[0, d] = c4 * dec_lane + _dot_tn(k4, vbd)
        kw = k4.astype(F32) * wt
        for c in heads:
            n_next = dec[c] * n[c] + jnp.sum(kw[rows_of(c)], axis=0, keepdims=True)
            nfin_ref[0, d, c] = jnp.broadcast_to(n_next, (8, ML_DK))
            mfin_ref[0, d, c] = jnp.broadcast_to(m_new[c], (8, LANES))
        if not with_out:
            continue
        q4 = grp["q"]
        _, incl, _ = _quad_masks(bool(d))
        m4 = spread(m)
        m_t = jnp.maximum(bcol + m4, dmcol)
        inter = jnp.exp(bcol + m4 - m_t)
        p = grp["qk"] * scale * jnp.exp(jnp.where(incl, bcol + grp["urow"] - m_t, NEG_BIG))
        c4b = c4.astype(BF16)
        qc = jnp.concatenate([_dot(q4[rows_of(c)], c4b[:, c * ML_DV:(c + 1) * ML_DV]) for c in heads], axis=0)
        num = (inter * scale) * qc + _dot(p.astype(BF16), v4)
        n_rows = jnp.concatenate([jnp.broadcast_to(n[c], (CHUNK, ML_DK)) for c in heads], axis=0)
        qn = jnp.sum(q4.astype(F32) * n_rows, axis=1, keepdims=True) * scale
        den = inter * qn + jnp.sum(p, axis=1, keepdims=True)
        out = num * (1.0 / jnp.maximum(jnp.abs(den), jnp.exp(-m_t)))
        for c in heads:
            grp["o_ref"][0, grp["rows"], c * ML_DV:(c + 1) * ML_DV] = out[rows_of(c)].astype(BF16)


def _ml_call(z, col0, gates, urow, c0, n0, m0, with_out):
    b, t, _ = z.shape
    cs = ML_CHUNKS_PER_STEP
    blk = cs * CHUNK
    nb = t // blk
    qoff = col0 * LANES // ML_QK
    koff = (col0 * LANES + ML_QK) // ML_QK
    voff = (col0 * LANES + 2 * ML_QK) // ML_V

    def fwd(*tail):
        return lambda bi, n: (bi, n) + tail

    def bwd(*tail):
        return lambda bi, n: (bi, nb - 1 - n) + tail

    def state_spec(shape):
        return pl.BlockSpec((1,) + shape, lambda bi, n: (bi,) + (0,) * len(shape))

    st = [(2, ML_DK, ML_HEADS * ML_DV), (2, ML_HEADS, 8, ML_DK), (2, ML_HEADS, 8, LANES)]
    st_specs = [state_spec(s) for s in st]
    st_shapes = [jax.ShapeDtypeStruct((b,) + s, F32) for s in st]
    in_specs = (
        [pl.BlockSpec((1, blk, ML_QK), fwd(qoff)), pl.BlockSpec((1, blk, ML_QK), fwd(koff)),
         pl.BlockSpec((1, blk, ML_V), fwd(voff)),
         pl.BlockSpec((1, blk, ML_QK), bwd(qoff)), pl.BlockSpec((1, blk, ML_QK), bwd(koff)),
         pl.BlockSpec((1, blk, ML_V), bwd(voff)),
         pl.BlockSpec((1, blk, 4 * LANES), fwd(0)), pl.BlockSpec((1, blk, 4 * LANES), bwd(0)),
         pl.BlockSpec((1, cs, 8, QW), fwd(0, 0)), pl.BlockSpec((1, cs, 8, QW), bwd(0, 0))]
        + st_specs)
    if with_out:
        out_specs = [pl.BlockSpec((1, blk, ML_V), fwd(0)), pl.BlockSpec((1, blk, ML_V), bwd(0))] + st_specs
        o_shape = jax.ShapeDtypeStruct((b, t, ML_V), BF16)
        out_shape = [o_shape, o_shape] + st_shapes
    else:
        out_specs = st_specs
        out_shape = st_shapes
    return pl.pallas_call(
        functools.partial(_ml_kernel, with_out=with_out),
        grid=(b, nb),
        in_specs=in_specs,
        out_specs=out_specs,
        out_shape=out_shape,
        compiler_params=_cparams("parallel", "arbitrary"),
        name="mlstm_scan_out" if with_out else "mlstm_scan_state",
    )(z, z, z, z, z, z, gates, gates, urow, urow, c0, n0, m0)


def _head_rms(x, width):
    outs = []
    for h in range(x.shape[1] // width):
        xh = x[:, h * width:(h + 1) * width]
        ms = jnp.sum(xh * xh, axis=-1, keepdims=True) * (1.0 / width)
        outs.append(xh * lax.rsqrt(ms + EPS))
    return jnp.concatenate(outs, axis=-1)


def _merge_kernel(x_ref, of_ref, ob_ref, hf_ref, hb_ref, gz_ref, mo_ref, gg_ref, gm_ref,
                  gnw_ref, mnw_ref, wg_ref, wm_ref, wo_ref, gate_ref, o_ref):
    f32 = lambda ref: ref[0].astype(F32)
    og = _head_rms(f32(of_ref) + f32(ob_ref), GDN_DV) * gnw_ref[...]
    gz = f32(gz_ref)
    og = og * (gz * _sigmoid(gz))
    y_gdn = _dot(og.astype(BF16), wg_ref[...])
    hm = _head_rms(f32(hf_ref) + f32(hb_ref), ML_DV) * mnw_ref[...]
    hm = hm * _sigmoid(f32(mo_ref))
    y_ml = _dot(hm.astype(BF16), wm_ref[...])
    merged = _sigmoid(f32(gg_ref)) * y_gdn + _sigmoid(f32(gm_ref)) * y_ml
    x_mix = _dot(merged.astype(BF16), wo_ref[...])
    o_ref[0] = x_ref[0] + gate_ref[0] * x_mix


def _merge_call(x, o_f, o_b, h_f, h_b, z, zo_col0, gnw, mnw, wg, wm, wo, gate):
    b, t, d = x.shape
    tm = _pick_tile(t, 512)
    nz = zo_col0 * LANES // d

    def tok(bi, i):
        return (bi, i, 0)

    def zcol(k):
        return lambda bi, i: (bi, i, nz + k)

    tile = lambda imap: pl.BlockSpec((1, tm, d), imap)
    full = lambda shape: pl.BlockSpec(shape, lambda bi, i: (0,) * len(shape))
    return pl.pallas_call(
        _merge_kernel,
        grid=(b, t // tm),
        in_specs=[tile(tok), tile(tok), tile(tok), tile(tok), tile(tok),
                  tile(zcol(0)), tile(zcol(1)), tile(zcol(2)), tile(zcol(3)),
                  full((1, d)), full((1, d)), full((d, d)), full((d, d)), full((d, d)),
                  pl.BlockSpec((1, 1, d), lambda bi, i: (bi, 0, 0))],
        out_specs=tile(tok),
        out_shape=jax.ShapeDtypeStruct((b, t, d), F32),
        compiler_params=_cparams("parallel", "parallel"),
        name="merge",
    )(x, o_f, o_b, h_f, h_b, z, z, z, z, gnw, mnw, wg, wm, wo, gate)


def _ffn2_kernel(ug_ref, ugp_ref, ugn_ref, uv_ref, uvp_ref, uvn_ref, cwg_ref, cwv_ref, wd_ref,
                 x_ref, gate_ref, nw_ref, o_ref, acc_ref, act_ref):
    i = pl.program_id(1)
    j = pl.program_id(2)
    first = i == 0
    last = i == pl.num_programs(1) - 1
    n_img_rows = ug_ref.shape[1] // GRID_W
    col = lax.broadcasted_iota(jnp.int32, (GRID_W, LANES), 0)

    def row_conv(u_ref, up_ref, un_ref, cw_ref, cols):
        w = [cw_ref[k:k + 1, cols] for k in range(9)]
        cache = {}

        def strip(r):
            if r not in cache:
                if r < 0:
                    cache[r] = jnp.where(first, 0.0, up_ref[0, :, cols].astype(F32))
                elif r == n_img_rows:
                    cache[r] = jnp.where(last, 0.0, un_ref[0, :, cols].astype(F32))
                else:
                    cache[r] = u_ref[0, r * GRID_W:(r + 1) * GRID_W, cols].astype(F32)
            return cache[r]

        def conv(r):
            taps = [strip(r - 1), strip(r), strip(r + 1)]
            side = [taps[0] * w[dc] + taps[1] * w[3 + dc] + taps[2] * w[6 + dc] for dc in range(3)]
            left = jnp.where(col == 0, 0.0, pltpu.roll(side[0], 1, 0))
            right = jnp.where(col == GRID_W - 1, 0.0, pltpu.roll(side[2], GRID_W - 1, 0))
            return side[1] + left + right

        return conv

    for cb in range(ug_ref.shape[2] // LANES):
        cols = slice(cb * LANES, (cb + 1) * LANES)
        conv_g = row_conv(ug_ref, ugp_ref, ugn_ref, cwg_ref, cols)
        conv_v = row_conv(uv_ref, uvp_ref, uvn_ref, cwv_ref, cols)
        for r in range(n_img_rows):
            g = conv_g(r)
            act_ref[r * GRID_W:(r + 1) * GRID_W, cols] = (g * _sigmoid(g) * conv_v(r)).astype(BF16)
    part = _dot(act_ref[...], wd_ref[j])

    @pl.when(j == 0)
    def _():
        acc_ref[...] = part

    @pl.when(j > 0)
    def _():
        acc_ref[...] += part

    @pl.when(j == pl.num_programs(2) - 1)
    def _():
        x = x_ref[0] + gate_ref[0] * acc_ref[...]
        ms = jnp.mean(x * x, axis=-1, keepdims=True)
        o_ref[0] = x * lax.rsqrt(ms + EPS) * nw_ref[...]


def _ffn2_call(u, conv_w, w_down, x, gate, nw):
    b, t, d = x.shape
    f = w_down.shape[0]
    tm = _pick_tile(t, 1024)
    tc = 256
    nj = f // tc
    w_tiles = w_down.reshape(nj, tc, d)
    rpt = tm // GRID_W
    nrows = t // GRID_W

    def main(off):
        return pl.BlockSpec((1, tm, tc), lambda bi, i, j: (bi, i, off + j))

    def prev(off):
        return pl.BlockSpec((1, GRID_W, tc), lambda bi, i, j: (bi, jnp.maximum(i * rpt - 1, 0), off + j))

    def nxt(off):
        return pl.BlockSpec((1, GRID_W, tc), lambda bi, i, j: (bi, jnp.minimum((i + 1) * rpt, nrows - 1), off + j))

    return pl.pallas_call(
        _ffn2_kernel,
        grid=(b, t // tm, nj),
        in_specs=[main(0), prev(0), nxt(0), main(nj), prev(nj), nxt(nj),
                  pl.BlockSpec((9, tc), lambda bi, i, j: (0, j)),
                  pl.BlockSpec((9, tc), lambda bi, i, j: (0, nj + j)),
                  pl.BlockSpec((nj, tc, d), lambda bi, i, j: (0, 0, 0), pipeline_mode=pl.Buffered(1)),
                  pl.BlockSpec((1, tm, d), lambda bi, i, j: (bi, i, 0)),
                  pl.BlockSpec((1, 1, d), lambda bi, i, j: (bi, 0, 0)),
                  pl.BlockSpec((1, d), lambda bi, i, j: (0, 0))],
        out_specs=pl.BlockSpec((1, tm, d), lambda bi, i, j: (bi, i, 0)),
        out_shape=jax.ShapeDtypeStruct((b, t, d), F32),
        scratch_shapes=[pltpu.VMEM((tm, d), F32), pltpu.VMEM((tm, tc), BF16)],
        compiler_params=_cparams("parallel", "parallel", "arbitrary"),
        name="ffn2",
    )(u, u, u, u, u, u, conv_w, conv_w, w_tiles, x, gate, nw)


def _ffn_kernel(x_ref, xp_ref, xn_ref, nw2_ref, sh_ref, sc_ref, wup_ref, cwg_ref, cwv_ref, wd_ref,
                gate_ref, nwo_ref, o_ref, hn_ref, ua_ref, ub_ref, acta_ref, actb_ref, acc_ref):
    i = pl.program_id(1)
    j = pl.program_id(2)
    nj = wd_ref.shape[0]
    first = i == 0
    last = i == pl.num_programs(1) - 1
    tm = x_ref.shape[1]
    n_img_rows = tm // GRID_W
    col = lax.broadcasted_iota(jnp.int32, (GRID_W, LANES), 0)

    def up_project(u_ref, jt):
        hn = hn_ref[...]
        u_ref[0] = _dot(hn, wup_ref[jt])
        u_ref[1] = _dot(hn, wup_ref[nj + jt])

    @pl.when(j == 0)
    def _():
        def norm_mod(x):
            ms = jnp.mean(x * x, axis=-1, keepdims=True)
            y = x * lax.rsqrt(ms + EPS) * nw2_ref[...]
            return (y * (1.0 + sc_ref[0]) + sh_ref[0]).astype(BF16)

        hn_ref[0:GRID_W] = norm_mod(xp_ref[0])
        hn_ref[GRID_W:GRID_W + tm] = norm_mod(x_ref[0])
        hn_ref[GRID_W + tm:] = norm_mod(xn_ref[0])
        up_project(ua_ref, 0)
        actb_ref[...] = jnp.zeros(actb_ref.shape, BF16)
        acc_ref[...] = jnp.zeros(acc_ref.shape, F32)

    def row_conv(u_ref, cw_ref, cols):
        w = [cw_ref[k:k + 1, cols] for k in range(9)]
        cache = {}

        def strip(r):
            if r not in cache:
                s = u_ref[(r + 1) * GRID_W:(r + 2) * GRID_W, cols]
                if r < 0:
                    s = jnp.where(first, 0.0, s)
                elif r == n_img_rows:
                    s = jnp.where(last, 0.0, s)
                cache[r] = s
            return cache[r]

        def conv(r):
            taps = [strip(r - 1), strip(r), strip(r + 1)]
            side = [taps[0] * w[dc] + taps[1] * w[3 + dc] + taps[2] * w[6 + dc] for dc in range(3)]
            left = jnp.where(col == 0, 0.0, pltpu.roll(side[0], 1, 0))
            right = jnp.where(col == GRID_W - 1, 0.0, pltpu.roll(side[2], GRID_W - 1, 0))
            return side[1] + left + right

        return conv

    def step(u_cur, act_cur, u_next, act_prev):
        jn = jnp.minimum(j + 1, nj - 1)
        jp = jnp.maximum(j - 1, 0)
        ext = tm + 2 * GRID_W
        half_e, half_t = ext // FFN_ROW_SPLIT, tm // FFN_ROW_SPLIT

        def up_piece(which, lo):
            rows = slice(lo, lo + half_e)
            return lambda: u_next.__setitem__((which, rows), _dot(hn_ref[rows], wup_ref[which * nj + jn]))

        def down_piece(lo):
            rows = slice(lo, lo + half_t)

            def run():
                acc_ref[rows] += _dot(act_prev[rows], wd_ref[jp])
            return run

        def conv_piece(cb, conv_g, conv_v, r):
            cols = slice(cb * LANES, (cb + 1) * LANES)

            def run():
                g = conv_g(r)
                act_cur[r * GRID_W:(r + 1) * GRID_W, cols] = (g * _sigmoid(g) * conv_v(r)).astype(BF16)
            return run

        mxu_work = ([up_piece(which, k * half_e) for which in range(2) for k in range(FFN_ROW_SPLIT)]
                    + [down_piece(k * half_t) for k in range(FFN_ROW_SPLIT)])
        vpu_work = []
        for cb in range(act_cur.shape[1] // LANES):
            cols = slice(cb * LANES, (cb + 1) * LANES)
            conv_g = row_conv(u_cur.at[0], cwg_ref, cols)
            conv_v = row_conv(u_cur.at[1], cwv_ref, cols)
            vpu_work += [conv_piece(cb, conv_g, conv_v, r) for r in range(n_img_rows)]
        per = -(-len(vpu_work) // len(mxu_work))
        for k, piece in enumerate(mxu_work):
            piece()
            for run in vpu_work[k * per:(k + 1) * per]:
                run()

    @pl.when((j & 1) == 0)
    def _():
        step(ua_ref, acta_ref, ub_ref, actb_ref)

    @pl.when((j & 1) == 1)
    def _():
        step(ub_ref, actb_ref, ua_ref, acta_ref)

    @pl.when(j == nj - 1)
    def _():
        act_last = acta_ref if (nj - 1) % 2 == 0 else actb_ref
        acc_ref[...] += _dot(act_last[...], wd_ref[nj - 1])
        x = x_ref[0] + gate_ref[0] * acc_ref[...]
        ms = jnp.mean(x * x, axis=-1, keepdims=True)
        o_ref[0] = x * lax.rsqrt(ms + EPS) * nwo_ref[...]


def _ffn_call(x, nw2, shift, scale, w_up, conv_w, w_down, gate, nwo):
    b, t, d = x.shape
    f = w_down.shape[0]
    tm = _pick_tile(t, 1024)
    tc = 256
    nj = f // tc
    rpt = tm // GRID_W
    nrows = t // GRID_W
    wup_tiles = w_up.reshape(d, 2 * nj, tc).transpose(1, 0, 2)
    wd_tiles = w_down.reshape(nj, tc, d)
    const = lambda *shape: pl.BlockSpec(shape, lambda bi, i, j: (0,) * len(shape))
    resident = lambda *shape: pl.BlockSpec(shape, lambda bi, i, j: (0,) * len(shape), pipeline_mode=pl.Buffered(1))
    per_batch = pl.BlockSpec((1, 1, d), lambda bi, i, j: (bi, 0, 0))
    return pl.pallas_call(
        _ffn_kernel,
        grid=(b, t // tm, nj),
        in_specs=[pl.BlockSpec((1, tm, d), lambda bi, i, j: (bi, i, 0)),
                  pl.BlockSpec((1, GRID_W, d), lambda bi, i, j: (bi, jnp.maximum(i * rpt - 1, 0), 0)),
                  pl.BlockSpec((1, GRID_W, d), lambda bi, i, j: (bi, jnp.minimum((i + 1) * rpt, nrows - 1), 0)),
                  const(1, d), per_batch, per_batch,
                  resident(2 * nj, d, tc),
                  pl.BlockSpec((9, tc), lambda bi, i, j: (0, j)),
                  pl.BlockSpec((9, tc), lambda bi, i, j: (0, nj + j)),
                  resident(nj, tc, d),
                  per_batch, const(1, d)],
        out_specs=pl.BlockSpec((1, tm, d), lambda bi, i, j: (bi, i, 0)),
        out_shape=jax.ShapeDtypeStruct((b, t, d), F32),
        scratch_shapes=[pltpu.VMEM((tm + 2 * GRID_W, d), BF16),
                        pltpu.VMEM((2, tm + 2 * GRID_W, tc), F32), pltpu.VMEM((2, tm + 2 * GRID_W, tc), F32),
                        pltpu.VMEM((tm, tc), BF16), pltpu.VMEM((tm, tc), BF16), pltpu.VMEM((tm, d), F32)],
        compiler_params=pltpu.CompilerParams(
            dimension_semantics=("parallel", "parallel", "arbitrary"), vmem_limit_bytes=FFN_VMEM_LIMIT),
        name="ffn",
    )(x, x, x, nw2, shift, scale, wup_tiles, conv_w, conv_w, wd_tiles, gate, nwo)


def _gdn_gate_rows(gates):
    b, t, _ = gates.shape
    nc = t // CHUNK

    def rows(base):
        q = gates[:, :, base:base + 2 * GDN_HEADS].reshape(b, nc, CHUNK, 2, NQUAD, QUAD)
        return q.transpose(0, 1, 3, 4, 5, 2).reshape(b, nc, 2 * NQUAD, QW)

    return jnp.concatenate([rows(G_A), rows(G_B)], axis=2)


def _ml_gate_rows(gates):
    b, t, _ = gates.shape
    nc = t // CHUNK
    base = 2 * LANES + G_I
    u = gates[:, :, base:base + 2 * ML_HEADS].reshape(b, nc, CHUNK, 2, ML_HEADS)
    u = u.transpose(0, 1, 3, 4, 2).reshape(b, nc, 2, QW)
    return jnp.pad(u, ((0, 0), (0, 0), (0, 6), (0, 0)))


def _mixer_states(x_seq, nw, shift, scale, w_state, w_aux, conv_w, alog_row, bias_row, states,
                  with_out, w_full=None):
    w = w_full if with_out else w_state
    z, z_aux = _nmm_call(x_seq, nw, shift, scale, w, w_aux, name="in_proj")
    gates = _gates_call(z_aux, alog_row, bias_row)
    qkv = _conv_call(z, conv_w)
    s_gdn, c_ml, n_ml, m_ml = states
    gdn_res = _gdn_call(qkv, gates[:, :, :LANES], _gdn_gate_rows(gates), s_gdn, with_out)
    ml_col0 = (2 * GDN_QK + GDN_V) // LANES
    ml_res = _ml_call(z, ml_col0, gates, _ml_gate_rows(gates), c_ml, n_ml, m_ml, with_out)
    return z, gdn_res, ml_res


def kernel(x, c, ctx, c_ctx, w_ada, b_ada, norm1_w, w_in, gdn_conv, gdn_a_log, gdn_dt_bias, gdn_norm_w,
           ml_igate_b, ml_fgate_b, ml_norm_w, w_branch_gdn, w_branch_ml, w_out, norm2_w, w_up, ffn_conv,
           w_down, norm_out_w):
    bsz, _, d = x.shape
    depth = w_ada.shape[0]
    assert depth == 1, "single-layer problem: the context stream is never updated"
    l = 0

    sizes = (2 * GDN_QK + GDN_V, 2 * GDN_HEADS, 2 * GDN_HEADS, ML_QK, ML_QK, ML_V, 2 * ML_HEADS, 2 * ML_HEADS,
             GDN_V, ML_V, d, d)
    offs = [0]
    for s in sizes:
        offs.append(offs[-1] + s)
    wi = w_in[l]
    seg = lambda k: wi[:, offs[k]:offs[k + 1]]
    w_state = jnp.concatenate([seg(0), seg(3), seg(4), seg(5)], axis=1).astype(BF16)
    w_full = jnp.concatenate([seg(0), seg(3), seg(4), seg(5), seg(8), seg(9), seg(10), seg(11)], axis=1).astype(BF16)
    n_gate = 4 * GDN_HEADS + 4 * ML_HEADS
    w_aux = jnp.concatenate([seg(1), seg(2), seg(6), seg(7), jnp.zeros((d, LANES - n_gate), F32)], axis=1).astype(BF16)
    pad = lambda v, n: jnp.pad(v.reshape(1, -1).astype(F32), ((0, 0), (0, n - v.size)))
    alog_row = pad(gdn_a_log[l], LANES)
    bias_row = pad(jnp.concatenate([gdn_dt_bias[l].reshape(-1), jnp.zeros((2 * GDN_HEADS,), F32),
                                    ml_igate_b[l].reshape(-1), ml_fgate_b[l].reshape(-1)]), LANES)
    row = lambda v: v.reshape(1, -1).astype(F32)

    c_all = jnp.concatenate([c, c_ctx[None], jnp.zeros((8 - bsz - 1, d), F32)], axis=0)
    mods = _mod_call(c_all, w_ada[l].astype(BF16), row(b_ada[l]))
    mod_x = mods[:bsz].reshape(bsz, N_MOD, 1, d)
    mod_c = jnp.broadcast_to(mods[bsz].reshape(1, N_MOD, 1, d), (bsz, N_MOD, 1, d))

    zero_states = (jnp.zeros((bsz, 2, NQUAD, GDN_DK, QUAD * GDN_DV), F32),
                   jnp.zeros((bsz, 2, ML_DK, ML_HEADS * ML_DV), F32),
                   jnp.zeros((bsz, 2, ML_HEADS, 8, ML_DK), F32),
                   jnp.zeros((bsz, 2, ML_HEADS, 8, LANES), F32))
    common = (w_state, w_aux, gdn_conv[l].astype(F32), alog_row, bias_row)

    _, (s_gdn,), (c_ml, n_ml, m_ml) = _mixer_states(
        ctx, row(norm1_w[l]), mod_c[:, 0], mod_c[:, 1], *common, zero_states, False)

    z, (o_f, o_b, _), (h_f, h_b, _, _, _) = _mixer_states(
        x, row(norm1_w[l]), mod_x[:, 0], mod_x[:, 1], *common, (s_gdn, c_ml, n_ml, m_ml), True, w_full)
    x1 = _merge_call(x, o_f, o_b, h_f, h_b, z, STATE_COLS // LANES,
                     row(jnp.tile(gdn_norm_w[l], GDN_HEADS)), row(ml_norm_w[l]),
                     w_branch_gdn[l].astype(BF16), w_branch_ml[l].astype(BF16), w_out[l].astype(BF16),
                     mod_x[:, 2])
    return _ffn_call(x1, row(norm2_w[l]), mod_x[:, 3], mod_x[:, 4], w_up[l].astype(BF16),
                     ffn_conv[l].reshape(9, -1).astype(F32), w_down[l].astype(BF16), mod_x[:, 5], row(norm_out_w))
```

```python
import functools

import jax
import jax.numpy as jnp
from jax import lax
from jax.experimental import pallas as pl
from jax.experimental.pallas import tpu as pltpu

F32 = jnp.float32
BF16 = jnp.bfloat16

GDN_HEADS = 8
GDN_DK = 128
GDN_DV = 128
ML_HEADS = 4
ML_DK = 128
ML_DV = 256
CHUNK = 64
GATE_CAP = 15.0
GRID_W = 64
N_MOD = 6
EPS = 1e-6
LANES = 128
HALO = 16
NEG_BIG = -1e30

GDN_QK = GDN_HEADS * GDN_DK
GDN_V = GDN_HEADS * GDN_DV
ML_QK = ML_HEADS * ML_DK
ML_V = ML_HEADS * ML_DV
STATE_COLS = 2 * GDN_QK + GDN_V + 2 * ML_QK + ML_V
OUT_COLS = GDN_V + ML_V + 2 * 1024

VMEM_LIMIT = 48 * 1024 * 1024
FFN_VMEM_LIMIT = 56 * 1024 * 1024
FFN_ROW_SPLIT = 4


def _cparams(*sem):
    return pltpu.CompilerParams(dimension_semantics=sem, vmem_limit_bytes=VMEM_LIMIT)


def _dot(a, b):
    return jnp.dot(a, b, preferred_element_type=F32)


def _dot_nt(a, b):
    return lax.dot_general(a, b, (((1,), (1,)), ((), ())), preferred_element_type=F32)


def _dot_tn(a, b):
    return lax.dot_general(a, b, (((0,), (0,)), ((), ())), preferred_element_type=F32)


def _sigmoid(x):
    return 1.0 / (1.0 + jnp.exp(-x))


def _softplus(x):
    return jnp.maximum(x, 0.0) + jnp.log1p(jnp.exp(-jnp.abs(x)))


def _pick_tile(n, pref):
    t = min(n, pref)
    while n % t:
        t //= 2
    return t


def _pick_cols(c, cap):
    t = cap - cap % LANES
    while c % t:
        t -= LANES
    return t


def _mod_kernel(c_ref, w_ref, b_ref, o_ref):
    c = c_ref[...]
    s = c * _sigmoid(c)
    o_ref[...] = _dot(s.astype(BF16), w_ref[...]) + b_ref[...]


def _mod_call(c_all, w_ada, b_ada):
    rows, d = c_all.shape
    n = w_ada.shape[1]
    tn = 1024
    return pl.pallas_call(
        _mod_kernel,
        grid=(n // tn,),
        in_specs=[
            pl.BlockSpec((rows, d), lambda j: (0, 0)),
            pl.BlockSpec((d, tn), lambda j: (0, j)),
            pl.BlockSpec((1, tn), lambda j: (0, j)),
        ],
        out_specs=pl.BlockSpec((rows, tn), lambda j: (0, j)),
        out_shape=jax.ShapeDtypeStruct((rows, n), F32),
        compiler_params=_cparams("parallel"),
        name="mod",
    )(c_all, w_ada, b_ada)


def _nmm_kernel(x_ref, nw_ref, sh_ref, sc_ref, w_ref, *rest, has_aux):
    if has_aux:
        wa_ref, o_ref, oa_ref, hn_ref = rest
    else:
        o_ref, hn_ref = rest

    @pl.when(pl.program_id(2) == 0)
    def _():
        x = x_ref[0]
        ms = jnp.mean(x * x, axis=-1, keepdims=True)
        y = x * lax.rsqrt(ms + EPS) * nw_ref[...]
        hb = (y * (1.0 + sc_ref[0]) + sh_ref[0]).astype(BF16)
        hn_ref[...] = hb
        if has_aux:
            oa_ref[0] = _dot(hb, wa_ref[...])

    o_ref[0] = _dot(hn_ref[...], w_ref[pl.program_id(2)]).astype(o_ref.dtype)


def _nmm_call(x, nw, shift, scale, w, w_aux=None, out_dtype=BF16, name="nmm"):
    b, t, d = x.shape
    c = w.shape[1]
    tm = _pick_tile(t, 1024)
    tn = _pick_cols(c, 1536)
    nj = c // tn
    has_aux = w_aux is not None
    w_tiles = w.reshape(d, nj, tn).transpose(1, 0, 2)
    in_specs = [
        pl.BlockSpec((1, tm, d), lambda bi, i, j: (bi, i, 0)),
        pl.BlockSpec((1, d), lambda bi, i, j: (0, 0)),
        pl.BlockSpec((1, 1, d), lambda bi, i, j: (bi, 0, 0)),
        pl.BlockSpec((1, 1, d), lambda bi, i, j: (bi, 0, 0)),
        pl.BlockSpec((nj, d, tn), lambda bi, i, j: (0, 0, 0), pipeline_mode=pl.Buffered(1)),
    ]
    out_specs = [pl.BlockSpec((1, tm, tn), lambda bi, i, j: (bi, i, j))]
    out_shape = [jax.ShapeDtypeStruct((b, t, c), out_dtype)]
    args = [x, nw, shift, scale, w_tiles]
    if has_aux:
        in_specs.append(pl.BlockSpec((d, LANES), lambda bi, i, j: (0, 0)))
        out_specs.append(pl.BlockSpec((1, tm, LANES), lambda bi, i, j: (bi, i, 0)))
        out_shape.append(jax.ShapeDtypeStruct((b, t, LANES), F32))
        args.append(w_aux)
    res = pl.pallas_call(
        functools.partial(_nmm_kernel, has_aux=has_aux),
        grid=(b, t // tm, c // tn),
        in_specs=in_specs,
        out_specs=out_specs,
        out_shape=out_shape,
        scratch_shapes=[pltpu.VMEM((tm, d), BF16)],
        compiler_params=_cparams("parallel", "parallel", "arbitrary"),
        name=name,
    )(*args)
    return res if has_aux else res[0]


G_A, G_B, G_I, G_F = 0, 16, 32, 40


def _chunk_scan(x, pos, backward, op, ident):
    rows = x.shape[0]
    yf, yb = x, x
    s = 1
    while s < CHUNK:
        yf = op(yf, jnp.where(pos >= s, pltpu.roll(yf, s, 0), ident))
        yb = op(yb, jnp.where(pos + s < CHUNK, pltpu.roll(yb, rows - s, 0), ident))
        s *= 2
    return jnp.where(backward, yb, yf)


def _gates_kernel(z_ref, alog_ref, bias_ref, o_ref):
    z = z_ref[0] + bias_ref[...]
    lane = lax.broadcasted_iota(jnp.int32, z.shape, 1)
    pos = lax.broadcasted_iota(jnp.int32, z.shape, 0) & (CHUNK - 1)
    backward = ((lane < G_I) & ((lane & 15) >= 8)) | ((lane >= G_I) & ((lane & 7) >= 4))

    log_decay = -jnp.exp(alog_ref[...]) * _softplus(z)
    beta = _sigmoid(z)
    capped = GATE_CAP * jnp.tanh(z * (1.0 / GATE_CAP))
    log_forget = -_softplus(-capped)

    summand = jnp.where(lane < G_B, log_decay, jnp.where(lane >= G_F, log_forget, 0.0))
    csum = _chunk_scan(summand, pos, backward, jnp.add, 0.0)
    o_ref[0, :, 0:LANES] = jnp.where(lane < G_B, csum, beta)

    bc = pltpu.roll(csum, LANES - (G_F - G_I), 1)
    backward_i = (lane & 7) >= 4
    u = capped - bc
    umax = _chunk_scan(u, pos, backward_i, jnp.maximum, NEG_BIG)
    o_ref[0, :, LANES:2 * LANES] = bc
    o_ref[0, :, 2 * LANES:3 * LANES] = u
    o_ref[0, :, 3 * LANES:4 * LANES] = bc + umax


def _gates_call(z_aux, alog_row, bias_row):
    b, t, _ = z_aux.shape
    tm = _pick_tile(t, 512)
    return pl.pallas_call(
        _gates_kernel,
        grid=(b, t // tm),
        in_specs=[
            pl.BlockSpec((1, tm, LANES), lambda bi, i: (bi, i, 0)),
            pl.BlockSpec((1, LANES), lambda bi, i: (0, 0)),
            pl.BlockSpec((1, LANES), lambda bi, i: (0, 0)),
        ],
        out_specs=pl.BlockSpec((1, tm, 4 * LANES), lambda bi, i: (bi, i, 0)),
        out_shape=jax.ShapeDtypeStruct((b, t, 4 * LANES), F32),
        compiler_params=_cparams("parallel", "parallel"),
        name="gates",
    )(z_aux, alog_row, bias_row)


def _conv_kernel(z_ref, zp_ref, zn_ref, w_ref, o_ref):
    i = pl.program_id(1)
    j = pl.program_id(2)
    z = z_ref[0].astype(F32)
    rows = z.shape[0]
    row8 = lax.broadcasted_iota(jnp.int32, (8, z.shape[1]), 0)
    prev_row = jnp.where(i == 0, 0.0, zp_ref[0, HALO - 1:HALO, :].astype(F32))
    next_row = jnp.where(i == pl.num_programs(1) - 1, 0.0, zn_ref[0, 0:1, :].astype(F32))
    z_prev = pltpu.roll(z, 1, 0)
    z_prev = jnp.concatenate([jnp.where(row8 == 0, prev_row, z_prev[:8]), z_prev[8:]], axis=0)
    z_next = pltpu.roll(z, rows - 1, 0)
    z_next = jnp.concatenate([z_next[:rows - 8], jnp.where(row8 == 7, next_row, z_next[rows - 8:])], axis=0)
    y = z_prev * w_ref[0:1, :] + z * w_ref[1:2, :] + z_next * w_ref[2:3, :]
    y = y * _sigmoid(y)

    is_qk = j < 2
    for h in range(GDN_HEADS):
        cols = slice(h * GDN_DK, (h + 1) * GDN_DK)
        yh = y[:, cols]
        inv = lax.rsqrt(jnp.sum(yh * yh, axis=-1, keepdims=True) + EPS)
        o_ref[0, :, cols] = (yh * jnp.where(is_qk, inv, 1.0)).astype(o_ref.dtype)


def _conv_call(z, conv_w):
    b, t, _ = z.shape
    tt = _pick_tile(t, 512)
    g = GDN_QK
    nbh = tt // HALO
    return pl.pallas_call(
        _conv_kernel,
        grid=(b, t // tt, 3),
        in_specs=[
            pl.BlockSpec((1, tt, g), lambda bi, i, j: (bi, i, j)),
            pl.BlockSpec((1, HALO, g), lambda bi, i, j: (bi, jnp.maximum(i * nbh - 1, 0), j)),
            pl.BlockSpec((1, HALO, g), lambda bi, i, j: (bi, jnp.minimum((i + 1) * nbh, t // HALO - 1), j)),
            pl.BlockSpec((3, g), lambda bi, i, j: (0, j)),
        ],
        out_specs=pl.BlockSpec((1, tt, g), lambda bi, i, j: (bi, i, j)),
        out_shape=jax.ShapeDtypeStruct((b, t, 3 * g), BF16),
        compiler_params=_cparams("parallel", "parallel", "parallel"),
        name="gdn_conv",
    )(z, z, z, conv_w)


QUAD = 4
QW = QUAD * CHUNK
NQUAD = GDN_HEADS // QUAD
GDN_CHUNKS_PER_STEP = 4
ML_CHUNKS_PER_STEP = 4


def _quad_masks(reverse):
    ii = lax.broadcasted_iota(jnp.int32, (QW, QW), 0)
    jj = lax.broadcasted_iota(jnp.int32, (QW, QW), 1)
    same = (ii ^ jj) < CHUNK
    if reverse:
        return same & (jj > ii), same & (jj >= ii), ii == jj
    return same & (jj < ii), same & (jj <= ii), ii == jj


def _gdn_kernel(*refs, with_out):
    (qf, kf, vf, qb, kb, vb, gcf, gcb, grf, grb, s0_ref) = refs[:11]
    if with_out:
        of_ref, ob_ref, sfin_ref = refs[11:]
    else:
        (sfin_ref,) = refs[11:]
        of_ref = ob_ref = None

    @pl.when(pl.program_id(1) == 0)
    def _():
        sfin_ref[...] = s0_ref[...]

    scale = GDN_DK ** -0.5
    groups = []
    cs = grf.shape[1]
    for ci, d, g in [(ci, d, g) for ci in range(cs) for d in range(2) for g in range(NQUAD)]:
        q_ref, k_ref, v_ref, gc_ref, gr_ref, o_ref = (
            (qb, kb, vb, gcb, grb, ob_ref) if d else (qf, kf, vf, gcf, grf, of_ref))
        ch = cs - 1 - ci if d else ci
        rows = slice(ch * CHUNK, (ch + 1) * CHUNK)
        heads = tuple(range(g * QUAD, (g + 1) * QUAD))

        def stack(ref):
            return jnp.concatenate([ref[0, rows, h * GDN_DK:(h + 1) * GDN_DK] for h in heads], axis=0)

        def col(base):
            lanes = [base + d * GDN_HEADS + h for h in heads]
            return jnp.concatenate([gc_ref[0, rows, l:l + 1] for l in lanes], axis=0)

        r = d * NQUAD + g
        groups.append(dict(
            d=d, g=g, heads=heads, rows=rows, o_ref=o_ref, k=stack(k_ref), v=stack(v_ref),
            q=stack(q_ref) if with_out else None, gcol=col(G_A), bcol=col(G_B),
            grow=gr_ref[0, ch, r:r + 1, :]))

    for grp in groups:
        strict, incl, diag = _quad_masks(bool(grp["d"]))
        decay = jnp.exp(jnp.where(incl, grp["gcol"] - grp["grow"], NEG_BIG))
        kk = _dot_nt(grp["k"], grp["k"])
        x = jnp.where(strict, kk * decay * (-grp["bcol"]), 0.0)
        grp["x"] = x
        grp["s"] = jnp.where(diag, 1.0, x)
        if with_out:
            grp["attn"] = (_dot_nt(grp["q"], grp["k"]) * decay * scale).astype(BF16)
    for grp in groups:
        xb = grp["x"].astype(BF16)
        grp["xm"] = _dot(xb, xb)
    m = 2
    while 2 * m < CHUNK:
        for grp in groups:
            xb = grp["xm"].astype(BF16)
            both = _dot(jnp.concatenate([grp["s"].astype(BF16), xb], axis=0), xb)
            grp["s"] = grp["s"] + both[:QW]
            grp["xm"] = both[QW:]
        m *= 2
    for grp in groups:
        t = grp["s"] + _dot(grp["s"].astype(BF16), grp["xm"].astype(BF16))
        bcol, gcol = grp["bcol"], grp["gcol"]
        rhs = jnp.concatenate([(grp["v"].astype(F32) * bcol).astype(BF16),
                               (grp["k"].astype(F32) * (bcol * jnp.exp(gcol))).astype(BF16)], axis=1)
        uw = _dot(t.astype(BF16), rhs)
        grp["u"] = uw[:, :GDN_DV]
        grp["w"] = uw[:, GDN_DV:]

    row = lax.broadcasted_iota(jnp.int32, (QW, GDN_DV), 0)
    for grp in groups:
        d, g, gcol = grp["d"], grp["g"], grp["gcol"]
        s4 = sfin_ref[0, d, g]
        sb = s4.astype(BF16)
        wb = grp["w"].astype(BF16)
        per_head = []
        for c in range(QUAD):
            lhs = wb[c * CHUNK:(c + 1) * CHUNK]
            if with_out:
                lhs = jnp.concatenate([lhs, grp["q"][c * CHUNK:(c + 1) * CHUNK]], axis=0)
            per_head.append(_dot(lhs, sb[:, c * GDN_DV:(c + 1) * GDN_DV]))

        def diag_blocks(base):
            return jnp.concatenate([r[base:base + CHUNK] for r in per_head], axis=0)

        v_new = grp["u"] - diag_blocks(0)
        end = 0 if d else CHUNK - 1
        g_end = [gcol[c * CHUNK + end:c * CHUNK + end + 1] for c in range(QUAD)]
        g_end_col = jnp.concatenate([jnp.broadcast_to(ge, (CHUNK, 1)) for ge in g_end], axis=0)
        vt = v_new * jnp.exp(g_end_col - gcol)
        vbd = jnp.concatenate(
            [jnp.where((row >= c * CHUNK) & (row < (c + 1) * CHUNK), vt, 0.0).astype(BF16) for c in range(QUAD)],
            axis=1)
        decay_lane = jnp.concatenate([jnp.broadcast_to(jnp.exp(ge), (1, GDN_DV)) for ge in g_end], axis=1)
        sfin_ref[0, d, g] = s4 * decay_lane + _dot_tn(grp["k"], vbd)
        if with_out:
            o = (jnp.exp(gcol) * scale) * diag_blocks(CHUNK) + _dot(grp["attn"], v_new.astype(BF16))
            for c, h in enumerate(grp["heads"]):
                grp["o_ref"][0, grp["rows"], h * GDN_DV:(h + 1) * GDN_DV] = o[c * CHUNK:(c + 1) * CHUNK].astype(BF16)


def _gdn_call(qkv, gcol, grow, s0, with_out):
    b, t, _ = qkv.shape
    cs = GDN_CHUNKS_PER_STEP
    blk = cs * CHUNK
    nb = t // blk
    w = GDN_QK

    def fwd(*tail):
        return lambda bi, n: (bi, n) + tail

    def bwd(*tail):
        return lambda bi, n: (bi, nb - 1 - n) + tail

    s_spec = pl.BlockSpec((1, 2, NQUAD, GDN_DK, QUAD * GDN_DV), lambda bi, n: (bi, 0, 0, 0, 0))
    in_specs = (
        [pl.BlockSpec((1, blk, w), fwd(o)) for o in range(3)]
        + [pl.BlockSpec((1, blk, w), bwd(o)) for o in range(3)]
        + [pl.BlockSpec((1, blk, LANES), fwd(0)), pl.BlockSpec((1, blk, LANES), bwd(0)),
           pl.BlockSpec((1, cs, 4 * NQUAD, QW), fwd(0, 0)), pl.BlockSpec((1, cs, 4 * NQUAD, QW), bwd(0, 0)),
           s_spec])
    s_shape = jax.ShapeDtypeStruct(s0.shape, F32)
    if with_out:
        out_specs = [pl.BlockSpec((1, blk, w), fwd(0)), pl.BlockSpec((1, blk, w), bwd(0)), s_spec]
        o_shape = jax.ShapeDtypeStruct((b, t, GDN_V), BF16)
        out_shape = [o_shape, o_shape, s_shape]
    else:
        out_specs = [s_spec]
        out_shape = [s_shape]
    return pl.pallas_call(
        functools.partial(_gdn_kernel, with_out=with_out),
        grid=(b, nb),
        in_specs=in_specs,
        out_specs=out_specs,
        out_shape=out_shape,
        compiler_params=_cparams("parallel", "arbitrary"),
        name="gdn_scan_out" if with_out else "gdn_scan_state",
    )(qkv, qkv, qkv, qkv, qkv, qkv, gcol, gcol, grow, grow, s0)


def _ml_kernel(*refs, with_out):
    (qf, kf, vf, qb, kb, vb, gcf, gcb, grf, grb, c0_ref, n0_ref, m0_ref) = refs[:13]
    if with_out:
        of_ref, ob_ref, cfin_ref, nfin_ref, mfin_ref = refs[13:]
    else:
        cfin_ref, nfin_ref, mfin_ref = refs[13:]
        of_ref = ob_ref = None

    @pl.when(pl.program_id(1) == 0)
    def _():
        cfin_ref[...] = c0_ref[...]
        nfin_ref[...] = n0_ref[...]
        mfin_ref[...] = m0_ref[...]

    scale = ML_DK ** -0.5
    heads = tuple(range(ML_HEADS))

    def rows_of(c):
        return slice(c * CHUNK, (c + 1) * CHUNK)

    def spread(xs):
        return jnp.concatenate([jnp.broadcast_to(x, (CHUNK, 1)) for x in xs], axis=0)

    groups = []
    cs = grf.shape[1]
    for ci, d in [(ci, d) for ci in range(cs) for d in range(2)]:
        q_ref, k_ref, v_ref, gc_ref, gr_ref, o_ref = (
            (qb, kb, vb, gcb, grb, ob_ref) if d else (qf, kf, vf, gcf, grf, of_ref))
        ch = cs - 1 - ci if d else ci
        rows = slice(ch * CHUNK, (ch + 1) * CHUNK)

        def stack(ref, width):
            return jnp.concatenate([ref[0, rows, h * width:(h + 1) * width] for h in heads], axis=0)

        def col(base):
            lanes = [base + G_I + d * ML_HEADS + h for h in heads]
            return jnp.concatenate([gc_ref[0, rows, l:l + 1] for l in lanes], axis=0)

        grp = dict(d=d, rows=rows, o_ref=o_ref, k=stack(k_ref, ML_DK), v=stack(v_ref, ML_DV),
                   bcol=col(LANES), ucol=col(2 * LANES), dmcol=col(3 * LANES), urow=gr_ref[0, ch, d:d + 1, :])
        if with_out:
            grp["q"] = stack(q_ref, ML_DK)
            grp["qk"] = _dot_nt(grp["q"], grp["k"])
        groups.append(grp)

    row = lax.broadcasted_iota(jnp.int32, (QW, ML_DV), 0)
    for grp in groups:
        d, k4, v4, bcol, ucol, dmcol = grp["d"], grp["k"], grp["v"], grp["bcol"], grp["ucol"], grp["dmcol"]
        end = 0 if d else CHUNK - 1
        m = [mfin_ref[0, d, h, 0:1, 0:1] for h in heads]
        n = [nfin_ref[0, d, h, 0:1, :] for h in heads]
        tot = [bcol[c * CHUNK + end:c * CHUNK + end + 1] for c in heads]
        m_new = [jnp.maximum(tot[c] + m[c], dmcol[c * CHUNK + end:c * CHUNK + end + 1]) for c in heads]
        dec = [jnp.exp(tot[c] + m[c] - m_new[c]) for c in heads]
        wt = jnp.exp(spread(tot) + ucol - spread(m_new))
        vw = v4.astype(F32) * wt
        vbd = jnp.concatenate(
            [jnp.where((row >= c * CHUNK) & (row < (c + 1) * CHUNK), vw, 0.0).astype(BF16) for c in heads], axis=1)
        c4 = cfin_ref[0, d]
        dec_lane = jnp.concatenate([jnp.broadcast_to(dec[c], (1, ML_DV)) for c in heads], axis=1)
        cfin_ref[0, d] = c4 * dec_lane + _dot_tn(k4, vbd)
        kw = k4.astype(F32) * wt
        for c in heads:
            n_next = dec[c] * n[c] + jnp.sum(kw[rows_of(c)], axis=0, keepdims=True)
            nfin_ref[0, d, c] = jnp.broadcast_to(n_next, (8, ML_DK))
            mfin_ref[0, d, c] = jnp.broadcast_to(m_new[c], (8, LANES))
        if not with_out:
            continue
        q4 = grp["q"]
        _, incl, _ = _quad_masks(bool(d))
        m4 = spread(m)
        m_t = jnp.maximum(bcol + m4, dmcol)
        inter = jnp.exp(bcol + m4 - m_t)
        p = grp["qk"] * scale * jnp.exp(jnp.where(incl, bcol + grp["urow"] - m_t, NEG_BIG))
        c4b = c4.astype(BF16)
        qc = jnp.concatenate([_dot(q4[rows_of(c)], c4b[:, c * ML_DV:(c + 1) * ML_DV]) for c in heads], axis=0)
        num = (inter * scale) * qc + _dot(p.astype(BF16), v4)
        n_rows = jnp.concatenate([jnp.broadcast_to(n[c], (CHUNK, ML_DK)) for c in heads], axis=0)
        qn = jnp.sum(q4.astype(F32) * n_rows, axis=1, keepdims=True) * scale
        den = inter * qn + jnp.sum(p, axis=1, keepdims=True)
        out = num * (1.0 / jnp.maximum(jnp.abs(den), jnp.exp(-m_t)))
        for c in heads:
            grp["o_ref"][0, grp["rows"], c * ML_DV:(c + 1) * ML_DV] = out[rows_of(c)].astype(BF16)


def _ml_call(z, col0, gates, urow, c0, n0, m0, with_out):
    b, t, _ = z.shape
    cs = ML_CHUNKS_PER_STEP
    blk = cs * CHUNK
    nb = t // blk
    qoff = col0 * LANES // ML_QK
    koff = (col0 * LANES + ML_QK) // ML_QK
    voff = (col0 * LANES + 2 * ML_QK) // ML_V

    def fwd(*tail):
        return lambda bi, n: (bi, n) + tail

    def bwd(*tail):
        return lambda bi, n: (bi, nb - 1 - n) + tail

    def state_spec(shape):
        return pl.BlockSpec((1,) + shape, lambda bi, n: (bi,) + (0,) * len(shape))

    st = [(2, ML_DK, ML_HEADS * ML_DV), (2, ML_HEADS, 8, ML_DK), (2, ML_HEADS, 8, LANES)]
    st_specs = [state_spec(s) for s in st]
    st_shapes = [jax.ShapeDtypeStruct((b,) + s, F32) for s in st]
    in_specs = (
        [pl.BlockSpec((1, blk, ML_QK), fwd(qoff)), pl.BlockSpec((1, blk, ML_QK), fwd(koff)),
         pl.BlockSpec((1, blk, ML_V), fwd(voff)),
         pl.BlockSpec((1, blk, ML_QK), bwd(qoff)), pl.BlockSpec((1, blk, ML_QK), bwd(koff)),
         pl.BlockSpec((1, blk, ML_V), bwd(voff)),
         pl.BlockSpec((1, blk, 4 * LANES), fwd(0)), pl.BlockSpec((1, blk, 4 * LANES), bwd(0)),
         pl.BlockSpec((1, cs, 8, QW), fwd(0, 0)), pl.BlockSpec((1, cs, 8, QW), bwd(0, 0))]
        + st_specs)
    if with_out:
        out_specs = [pl.BlockSpec((1, blk, ML_V), fwd(0)), pl.BlockSpec((1, blk, ML_V), bwd(0))] + st_specs
        o_shape = jax.ShapeDtypeStruct((b, t, ML_V), BF16)
        out_shape = [o_shape, o_shape] + st_shapes
    else:
        out_specs = st_specs
        out_shape = st_shapes
    return pl.pallas_call(
        functools.partial(_ml_kernel, with_out=with_out),
        grid=(b, nb),
        in_specs=in_specs,
        out_specs=out_specs,
        out_shape=out_shape,
        compiler_params=_cparams("parallel", "arbitrary"),
        name="mlstm_scan_out" if with_out else "mlstm_scan_state",
    )(z, z, z, z, z, z, gates, gates, urow, urow, c0, n0, m0)


def _head_rms(x, width):
    outs = []
    for h in range(x.shape[1] // width):
        xh = x[:, h * width:(h + 1) * width]
        ms = jnp.sum(xh * xh, axis=-1, keepdims=True) * (1.0 / width)
        outs.append(xh * lax.rsqrt(ms + EPS))
    return jnp.concatenate(outs, axis=-1)


def _merge_kernel(x_ref, of_ref, ob_ref, hf_ref, hb_ref, gz_ref, mo_ref, gg_ref, gm_ref,
                  gnw_ref, mnw_ref, wg_ref, wm_ref, wo_ref, gate_ref, o_ref):
    f32 = lambda ref: ref[0].astype(F32)
    og = _head_rms(f32(of_ref) + f32(ob_ref), GDN_DV) * gnw_ref[...]
    gz = f32(gz_ref)
    og = og * (gz * _sigmoid(gz))
    y_gdn = _dot(og.astype(BF16), wg_ref[...])
    hm = _head_rms(f32(hf_ref) + f32(hb_ref), ML_DV) * mnw_ref[...]
    hm = hm * _sigmoid(f32(mo_ref))
    y_ml = _dot(hm.astype(BF16), wm_ref[...])
    merged = _sigmoid(f32(gg_ref)) * y_gdn + _sigmoid(f32(gm_ref)) * y_ml
    x_mix = _dot(merged.astype(BF16), wo_ref[...])
    o_ref[0] = x_ref[0] + gate_ref[0] * x_mix


def _merge_call(x, o_f, o_b, h_f, h_b, z, zo_col0, gnw, mnw, wg, wm, wo, gate):
    b, t, d = x.shape
    tm = _pick_tile(t, 512)
    nz = zo_col0 * LANES // d

    def tok(bi, i):
        return (bi, i, 0)

    def zcol(k):
        return lambda bi, i: (bi, i, nz + k)

    tile = lambda imap: pl.BlockSpec((1, tm, d), imap)
    full = lambda shape: pl.BlockSpec(shape, lambda bi, i: (0,) * len(shape))
    return pl.pallas_call(
        _merge_kernel,
        grid=(b, t // tm),
        in_specs=[tile(tok), tile(tok), tile(tok), tile(tok), tile(tok),
                  tile(zcol(0)), tile(zcol(1)), tile(zcol(2)), tile(zcol(3)),
                  full((1, d)), full((1, d)), full((d, d)), full((d, d)), full((d, d)),
                  pl.BlockSpec((1, 1, d), lambda bi, i: (bi, 0, 0))],
        out_specs=tile(tok),
        out_shape=jax.ShapeDtypeStruct((b, t, d), F32),
        compiler_params=_cparams("parallel", "parallel"),
        name="merge",
    )(x, o_f, o_b, h_f, h_b, z, z, z, z, gnw, mnw, wg, wm, wo, gate)


def _ffn2_kernel(ug_ref, ugp_ref, ugn_ref, uv_ref, uvp_ref, uvn_ref, cwg_ref, cwv_ref, wd_ref,
                 x_ref, gate_ref, nw_ref, o_ref, acc_ref, act_ref):
    i = pl.program_id(1)
    j = pl.program_id(2)
    first = i == 0
    last = i == pl.num_programs(1) - 1
    n_img_rows = ug_ref.shape[1] // GRID_W
    col = lax.broadcasted_iota(jnp.int32, (GRID_W, LANES), 0)

    def row_conv(u_ref, up_ref, un_ref, cw_ref, cols):
        w = [cw_ref[k:k + 1, cols] for k in range(9)]
        cache = {}

        def strip(r):
            if r not in cache:
                if r < 0:
                    cache[r] = jnp.where(first, 0.0, up_ref[0, :, cols].astype(F32))
                elif r == n_img_rows:
                    cache[r] = jnp.where(last, 0.0, un_ref[0, :, cols].astype(F32))
                else:
                    cache[r] = u_ref[0, r * GRID_W:(r + 1) * GRID_W, cols].astype(F32)
            return cache[r]

        def conv(r):
            taps = [strip(r - 1), strip(r), strip(r + 1)]
            side = [taps[0] * w[dc] + taps[1] * w[3 + dc] + taps[2] * w[6 + dc] for dc in range(3)]
            left = jnp.where(col == 0, 0.0, pltpu.roll(side[0], 1, 0))
            right = jnp.where(col == GRID_W - 1, 0.0, pltpu.roll(side[2], GRID_W - 1, 0))
            return side[1] + left + right

        return conv

    for cb in range(ug_ref.shape[2] // LANES):
        cols = slice(cb * LANES, (cb + 1) * LANES)
        conv_g = row_conv(ug_ref, ugp_ref, ugn_ref, cwg_ref, cols)
        conv_v = row_conv(uv_ref, uvp_ref, uvn_ref, cwv_ref, cols)
        for r in range(n_img_rows):
            g = conv_g(r)
            act_ref[r * GRID_W:(r + 1) * GRID_W, cols] = (g * _sigmoid(g) * conv_v(r)).astype(BF16)
    part = _dot(act_ref[...], wd_ref[j])

    @pl.when(j == 0)
    def _():
        acc_ref[...] = part

    @pl.when(j > 0)
    def _():
        acc_ref[...] += part

    @pl.when(j == pl.num_programs(2) - 1)
    def _():
        x = x_ref[0] + gate_ref[0] * acc_ref[...]
        ms = jnp.mean(x * x, axis=-1, keepdims=True)
        o_ref[0] = x * lax.rsqrt(ms + EPS) * nw_ref[...]


def _ffn2_call(u, conv_w, w_down, x, gate, nw):
    b, t, d = x.shape
    f = w_down.shape[0]
    tm = _pick_tile(t, 1024)
    tc = 256
    nj = f // tc
    w_tiles = w_down.reshape(nj, tc, d)
    rpt = tm // GRID_W
    nrows = t // GRID_W

    def main(off):
        return pl.BlockSpec((1, tm, tc), lambda bi, i, j: (bi, i, off + j))

    def prev(off):
        return pl.BlockSpec((1, GRID_W, tc), lambda bi, i, j: (bi, jnp.maximum(i * rpt - 1, 0), off + j))

    def nxt(off):
        return pl.BlockSpec((1, GRID_W, tc), lambda bi, i, j: (bi, jnp.minimum((i + 1) * rpt, nrows - 1), off + j))

    return pl.pallas_call(
        _ffn2_kernel,
        grid=(b, t // tm, nj),
        in_specs=[main(0), prev(0), nxt(0), main(nj), prev(nj), nxt(nj),
                  pl.BlockSpec((9, tc), lambda bi, i, j: (0, j)),
                  pl.BlockSpec((9, tc), lambda bi, i, j: (0, nj + j)),
                  pl.BlockSpec((nj, tc, d), lambda bi, i, j: (0, 0, 0), pipeline_mode=pl.Buffered(1)),
                  pl.BlockSpec((1, tm, d), lambda bi, i, j: (bi, i, 0)),
                  pl.BlockSpec((1, 1, d), lambda bi, i, j: (bi, 0, 0)),
                  pl.BlockSpec((1, d), lambda bi, i, j: (0, 0))],
        out_specs=pl.BlockSpec((1, tm, d), lambda bi, i, j: (bi, i, 0)),
        out_shape=jax.ShapeDtypeStruct((b, t, d), F32),
        scratch_shapes=[pltpu.VMEM((tm, d), F32), pltpu.VMEM((tm, tc), BF16)],
        compiler_params=_cparams("parallel", "parallel", "arbitrary"),
        name="ffn2",
    )(u, u, u, u, u, u, conv_w, conv_w, w_tiles, x, gate, nw)


def _ffn_kernel(x_ref, xp_ref, xn_ref, nw2_ref, sh_ref, sc_ref, wup_ref, cwg_ref, cwv_ref, wd_ref,
                gate_ref, nwo_ref, o_ref, hn_ref, ua_ref, ub_ref, acta_ref, actb_ref, acc_ref):
    i = pl.program_id(1)
    j = pl.program_id(2)
    nj = wd_ref.shape[0]
    first = i == 0
    last = i == pl.num_programs(1) - 1
    tm = x_ref.shape[1]
    n_img_rows = tm // GRID_W
    col = lax.broadcasted_iota(jnp.int32, (GRID_W, LANES), 0)

    def up_project(u_ref, jt):
        hn = hn_ref[...]
        u_ref[0] = _dot(hn, wup_ref[jt])
        u_ref[1] = _dot(hn, wup_ref[nj + jt])

    @pl.when(j == 0)
    def _():
        def norm_mod(x):
            ms = jnp.mean(x * x, axis=-1, keepdims=True)
            y = x * lax.rsqrt(ms + EPS) * nw2_ref[...]
            return (y * (1.0 + sc_ref[0]) + sh_ref[0]).astype(BF16)

        hn_ref[0:GRID_W] = norm_mod(xp_ref[0])
        hn_ref[GRID_W:GRID_W + tm] = norm_mod(x_ref[0])
        hn_ref[GRID_W + tm:] = norm_mod(xn_ref[0])
        up_project(ua_ref, 0)
        actb_ref[...] = jnp.zeros(actb_ref.shape, BF16)
        acc_ref[...] = jnp.zeros(acc_ref.shape, F32)

    def row_conv(u_ref, cw_ref, cols):
        w = [cw_ref[k:k + 1, cols] for k in range(9)]
        cache = {}

        def strip(r):
            if r not in cache:
                s = u_ref[(r + 1) * GRID_W:(r + 2) * GRID_W, cols]
                if r < 0:
                    s = jnp.where(first, 0.0, s)
                elif r == n_img_rows:
                    s = jnp.where(last, 0.0, s)
                cache[r] = s
            return cache[r]

        def conv(r):
            taps = [strip(r - 1), strip(r), strip(r + 1)]
            side = [taps[0] * w[dc] + taps[1] * w[3 + dc] + taps[2] * w[6 + dc] for dc in range(3)]
            left = jnp.where(col == 0, 0.0, pltpu.roll(side[0], 1, 0))
            right = jnp.where(col == GRID_W - 1, 0.0, pltpu.roll(side[2], GRID_W - 1, 0))
            return side[1] + left + right

        return conv

    def step(u_cur, act_cur, u_next, act_prev):
        jn = jnp.minimum(j + 1, nj - 1)
        jp = jnp.maximum(j - 1, 0)
        ext = tm + 2 * GRID_W
        half_e, half_t = ext // FFN_ROW_SPLIT, tm // FFN_ROW_SPLIT

        def up_piece(which, lo):
            rows = slice(lo, lo + half_e)
            return lambda: u_next.__setitem__((which, rows), _dot(hn_ref[rows], wup_ref[which * nj + jn]))

        def down_piece(lo):
            rows = slice(lo, lo + half_t)

            def run():
                acc_ref[rows] += _dot(act_prev[rows], wd_ref[jp])
            return run

        def conv_piece(cb, conv_g, conv_v, r):
            cols = slice(cb * LANES, (cb + 1) * LANES)

            def run():
                g = conv_g(r)
                act_cur[r * GRID_W:(r + 1) * GRID_W, cols] = (g * _sigmoid(g) * conv_v(r)).astype(BF16)
            return run

        mxu_work = ([up_piece(which, k * half_e) for which in range(2) for k in range(FFN_ROW_SPLIT)]
                    + [down_piece(k * half_t) for k in range(FFN_ROW_SPLIT)])
        vpu_work = []
        for cb in range(act_cur.shape[1] // LANES):
            cols = slice(cb * LANES, (cb + 1) * LANES)
            conv_g = row_conv(u_cur.at[0], cwg_ref, cols)
            conv_v = row_conv(u_cur.at[1], cwv_ref, cols)
            vpu_work += [conv_piece(cb, conv_g, conv_v, r) for r in range(n_img_rows)]
        per = -(-len(vpu_work) // len(mxu_work))
        for k, piece in enumerate(mxu_work):
            piece()
            for run in vpu_work[k * per:(k + 1) * per]:
                run()

    @pl.when((j & 1) == 0)
    def _():
        step(ua_ref, acta_ref, ub_ref, actb_ref)

    @pl.when((j & 1) == 1)
    def _():
        step(ub_ref, actb_ref, ua_ref, acta_ref)

    @pl.when(j == nj - 1)
    def _():
        act_last = acta_ref if (nj - 1) % 2 == 0 else actb_ref
        acc_ref[...] += _dot(act_last[...], wd_ref[nj - 1])
        x = x_ref[0] + gate_ref[0] * acc_ref[...]
        ms = jnp.mean(x * x, axis=-1, keepdims=True)
        o_ref[0] = x * lax.rsqrt(ms + EPS) * nwo_ref[...]


def _ffn_call(x, nw2, shift, scale, w_up, conv_w, w_down, gate, nwo):
    b, t, d = x.shape
    f = w_down.shape[0]
    tm = _pick_tile(t, 1024)
    tc = 256
    nj = f // tc
    rpt = tm // GRID_W
    nrows = t // GRID_W
    wup_tiles = w_up.reshape(d, 2 * nj, tc).transpose(1, 0, 2)
    wd_tiles = w_down.reshape(nj, tc, d)
    const = lambda *shape: pl.BlockSpec(shape, lambda bi, i, j: (0,) * len(shape))
    resident = lambda *shape: pl.BlockSpec(shape, lambda bi, i, j: (0,) * len(shape), pipeline_mode=pl.Buffered(1))
    per_batch = pl.BlockSpec((1, 1, d), lambda bi, i, j: (bi, 0, 0))
    return pl.pallas_call(
        _ffn_kernel,
        grid=(b, t // tm, nj),
        in_specs=[pl.BlockSpec((1, tm, d), lambda bi, i, j: (bi, i, 0)),
                  pl.BlockSpec((1, GRID_W, d), lambda bi, i, j: (bi, jnp.maximum(i * rpt - 1, 0), 0)),
                  pl.BlockSpec((1, GRID_W, d), lambda bi, i, j: (bi, jnp.minimum((i + 1) * rpt, nrows - 1), 0)),
                  const(1, d), per_batch, per_batch,
                  resident(2 * nj, d, tc),
                  pl.BlockSpec((9, tc), lambda bi, i, j: (0, j)),
                  pl.BlockSpec((9, tc), lambda bi, i, j: (0, nj + j)),
                  resident(nj, tc, d),
                  per_batch, const(1, d)],
        out_specs=pl.BlockSpec((1, tm, d), lambda bi, i, j: (bi, i, 0)),
        out_shape=jax.ShapeDtypeStruct((b, t, d), F32),
        scratch_shapes=[pltpu.VMEM((tm + 2 * GRID_W, d), BF16),
                        pltpu.VMEM((2, tm + 2 * GRID_W, tc), F32), pltpu.VMEM((2, tm + 2 * GRID_W, tc), F32),
                        pltpu.VMEM((tm, tc), BF16), pltpu.VMEM((tm, tc), BF16), pltpu.VMEM((tm, d), F32)],
        compiler_params=pltpu.CompilerParams(
            dimension_semantics=("parallel", "parallel", "arbitrary"), vmem_limit_bytes=FFN_VMEM_LIMIT),
        name="ffn",
    )(x, x, x, nw2, shift, scale, wup_tiles, conv_w, conv_w, wd_tiles, gate, nwo)


def _gdn_gate_rows(gates):
    b, t, _ = gates.shape
    nc = t // CHUNK

    def rows(base):
        q = gates[:, :, base:base + 2 * GDN_HEADS].reshape(b, nc, CHUNK, 2, NQUAD, QUAD)
        return q.transpose(0, 1, 3, 4, 5, 2).reshape(b, nc, 2 * NQUAD, QW)

    return jnp.concatenate([rows(G_A), rows(G_B)], axis=2)


def _ml_gate_rows(gates):
    b, t, _ = gates.shape
    nc = t // CHUNK
    base = 2 * LANES + G_I
    u = gates[:, :, base:base + 2 * ML_HEADS].reshape(b, nc, CHUNK, 2, ML_HEADS)
    u = u.transpose(0, 1, 3, 4, 2).reshape(b, nc, 2, QW)
    return jnp.pad(u, ((0, 0), (0, 0), (0, 6), (0, 0)))


def _mixer_states(x_seq, nw, shift, scale, w_state, w_aux, conv_w, alog_row, bias_row, states,
                  with_out, w_full=None):
    w = w_full if with_out else w_state
    z, z_aux = _nmm_call(x_seq, nw, shift, scale, w, w_aux, name="in_proj")
    gates = _gates_call(z_aux, alog_row, bias_row)
    qkv = _conv_call(z, conv_w)
    s_gdn, c_ml, n_ml, m_ml = states
    gdn_res = _gdn_call(qkv, gates[:, :, :LANES], _gdn_gate_rows(gates), s_gdn, with_out)
    ml_col0 = (2 * GDN_QK + GDN_V) // LANES
    ml_res = _ml_call(z, ml_col0, gates, _ml_gate_rows(gates), c_ml, n_ml, m_ml, with_out)
    return z, gdn_res, ml_res


def kernel(x, c, ctx, c_ctx, w_ada, b_ada, norm1_w, w_in, gdn_conv, gdn_a_log, gdn_dt_bias, gdn_norm_w,
           ml_igate_b, ml_fgate_b, ml_norm_w, w_branch_gdn, w_branch_ml, w_out, norm2_w, w_up, ffn_conv,
           w_down, norm_out_w):
    bsz, _, d = x.shape
    depth = w_ada.shape[0]
    assert depth == 1, "single-layer problem: the context stream is never updated"
    l = 0

    sizes = (2 * GDN_QK + GDN_V, 2 * GDN_HEADS, 2 * GDN_HEADS, ML_QK, ML_QK, ML_V, 2 * ML_HEADS, 2 * ML_HEADS,
             GDN_V, ML_V, d, d)
    offs = [0]
    for s in sizes:
        offs.append(offs[-1] + s)
    wi = w_in[l]
    seg = lambda k: wi[:, offs[k]:offs[k + 1]]
    w_state = jnp.concatenate([seg(0), seg(3), seg(4), seg(5)], axis=1).astype(BF16)
    w_full = jnp.concatenate([seg(0), seg(3), seg(4), seg(5), seg(8), seg(9), seg(10), seg(11)], axis=1).astype(BF16)
    n_gate = 4 * GDN_HEADS + 4 * ML_HEADS
    w_aux = jnp.concatenate([seg(1), seg(2), seg(6), seg(7), jnp.zeros((d, LANES - n_gate), F32)], axis=1).astype(BF16)
    pad = lambda v, n: jnp.pad(v.reshape(1, -1).astype(F32), ((0, 0), (0, n - v.size)))
    alog_row = pad(gdn_a_log[l], LANES)
    bias_row = pad(jnp.concatenate([gdn_dt_bias[l].reshape(-1), jnp.zeros((2 * GDN_HEADS,), F32),
                                    ml_igate_b[l].reshape(-1), ml_fgate_b[l].reshape(-1)]), LANES)
    row = lambda v: v.reshape(1, -1).astype(F32)

    c_all = jnp.concatenate([c, c_ctx[None], jnp.zeros((8 - bsz - 1, d), F32)], axis=0)
    mods = _mod_call(c_all, w_ada[l].astype(BF16), row(b_ada[l]))
    mod_x = mods[:bsz].reshape(bsz, N_MOD, 1, d)
    mod_c = jnp.broadcast_to(mods[bsz].reshape(1, N_MOD, 1, d), (bsz, N_MOD, 1, d))

    zero_states = (jnp.zeros((bsz, 2, NQUAD, GDN_DK, QUAD * GDN_DV), F32),
                   jnp.zeros((bsz, 2, ML_DK, ML_HEADS * ML_DV), F32),
                   jnp.zeros((bsz, 2, ML_HEADS, 8, ML_DK), F32),
                   jnp.zeros((bsz, 2, ML_HEADS, 8, LANES), F32))
    common = (w_state, w_aux, gdn_conv[l].astype(F32), alog_row, bias_row)

    _, (s_gdn,), (c_ml, n_ml, m_ml) = _mixer_states(
        ctx, row(norm1_w[l]), mod_c[:, 0], mod_c[:, 1], *common, zero_states, False)

    z, (o_f, o_b, _), (h_f, h_b, _, _, _) = _mixer_states(
        x, row(norm1_w[l]), mod_x[:, 0], mod_x[:, 1], *common, (s_gdn, c_ml, n_ml, m_ml), True, w_full)
    x1 = _merge_call(x, o_f, o_b, h_f, h_b, z, STATE_COLS // LANES,
                     row(jnp.tile(gdn_norm_w[l], GDN_HEADS)), row(ml_norm_w[l]),
                     w_branch_gdn[l].astype(BF16), w_branch_ml[l].astype(BF16), w_out[l].astype(BF16),
                     mod_x[:, 2])
    return _ffn_call(x1, row(norm2_w[l]), mod_x[:, 3], mod_x[:, 4], w_up[l].astype(BF16),
                     ffn_conv[l].reshape(9, -1).astype(F32), w_down[l].astype(BF16), mod_x[:, 5], row(norm_out_w))
```

```python
import functools

import jax
import jax.numpy as jnp
from jax import lax
from jax.experimental import pallas as pl
from jax.experimental.pallas import tpu as pltpu

F32 = jnp.float32
BF16 = jnp.bfloat16

GDN_HEADS = 8
GDN_DK = 128
GDN_DV = 128
ML_HEADS = 4
ML_DK = 128
ML_DV = 256
CHUNK = 64
GATE_CAP = 15.0
GRID_W = 64
N_MOD = 6
EPS = 1e-6
LANES = 128
HALO = 16
NEG_BIG = -1e30

GDN_QK = GDN_HEADS * GDN_DK
GDN_V = GDN_HEADS * GDN_DV
ML_QK = ML_HEADS * ML_DK
ML_V = ML_HEADS * ML_DV
STATE_COLS = 2 * GDN_QK + GDN_V + 2 * ML_QK + ML_V
OUT_COLS = GDN_V + ML_V + 2 * 1024

VMEM_LIMIT = 48 * 1024 * 1024
FFN_VMEM_LIMIT = 56 * 1024 * 1024
FFN_ROW_SPLIT = 4
SCAN_VMEM_LIMIT = 58 * 1024 * 1024


def _cparams(*sem):
    return pltpu.CompilerParams(dimension_semantics=sem, vmem_limit_bytes=VMEM_LIMIT)


def _dot(a, b):
    return jnp.dot(a, b, preferred_element_type=F32)


def _dot_nt(a, b):
    return lax.dot_general(a, b, (((1,), (1,)), ((), ())), preferred_element_type=F32)


def _dot_tn(a, b):
    return lax.dot_general(a, b, (((0,), (0,)), ((), ())), preferred_element_type=F32)


def _sigmoid(x):
    return 0.5 * jnp.tanh(0.5 * x) + 0.5


def _softplus(x):
    return jnp.maximum(x, 0.0) + jnp.log1p(jnp.exp(-jnp.abs(x)))


def _pick_tile(n, pref):
    t = min(n, pref)
    while n % t:
        t //= 2
    return t


def _pick_cols(c, cap):
    t = cap - cap % LANES
    while c % t:
        t -= LANES
    return t


def _mod_kernel(c_ref, w_ref, b_ref, o_ref):
    c = c_ref[...]
    s = c * _sigmoid(c)
    o_ref[...] = _dot(s.astype(BF16), w_ref[...]) + b_ref[...]


def _mod_call(c_all, w_ada, b_ada):
    rows, d = c_all.shape
    n = w_ada.shape[1]
    tn = 1024
    return pl.pallas_call(
        _mod_kernel,
        grid=(n // tn,),
        in_specs=[
            pl.BlockSpec((rows, d), lambda j: (0, 0)),
            pl.BlockSpec((d, tn), lambda j: (0, j)),
            pl.BlockSpec((1, tn), lambda j: (0, j)),
        ],
        out_specs=pl.BlockSpec((rows, tn), lambda j: (0, j)),
        out_shape=jax.ShapeDtypeStruct((rows, n), F32),
        compiler_params=_cparams("parallel"),
        name="mod",
    )(c_all, w_ada, b_ada)


def _nmm_kernel(x_ref, nw_ref, sh_ref, sc_ref, w_ref, *rest, has_aux):
    if has_aux:
        wa_ref, o_ref, oa_ref, hn_ref = rest
    else:
        o_ref, hn_ref = rest

    @pl.when(pl.program_id(2) == 0)
    def _():
        x = x_ref[0]
        ms = jnp.mean(x * x, axis=-1, keepdims=True)
        y = x * lax.rsqrt(ms + EPS) * nw_ref[...]
        hb = (y * (1.0 + sc_ref[0]) + sh_ref[0]).astype(BF16)
        hn_ref[...] = hb
        if has_aux:
            oa_ref[0] = _dot(hb, wa_ref[...])

    o_ref[0] = _dot(hn_ref[...], w_ref[pl.program_id(2)]).astype(o_ref.dtype)


def _nmm_call(x, nw, shift, scale, w, w_aux=None, out_dtype=BF16, name="nmm"):
    b, t, d = x.shape
    c = w.shape[1]
    tm = _pick_tile(t, 1024)
    tn = _pick_cols(c, 1536)
    nj = c // tn
    has_aux = w_aux is not None
    w_tiles = w.reshape(d, nj, tn).transpose(1, 0, 2)
    in_specs = [
        pl.BlockSpec((1, tm, d), lambda bi, i, j: (bi, i, 0)),
        pl.BlockSpec((1, d), lambda bi, i, j: (0, 0)),
        pl.BlockSpec((1, 1, d), lambda bi, i, j: (bi, 0, 0)),
        pl.BlockSpec((1, 1, d), lambda bi, i, j: (bi, 0, 0)),
        pl.BlockSpec((nj, d, tn), lambda bi, i, j: (0, 0, 0), pipeline_mode=pl.Buffered(1)),
    ]
    out_specs = [pl.BlockSpec((1, tm, tn), lambda bi, i, j: (bi, i, j))]
    out_shape = [jax.ShapeDtypeStruct((b, t, c), out_dtype)]
    args = [x, nw, shift, scale, w_tiles]
    if has_aux:
        in_specs.append(pl.BlockSpec((d, LANES), lambda bi, i, j: (0, 0)))
        out_specs.append(pl.BlockSpec((1, tm, LANES), lambda bi, i, j: (bi, i, 0)))
        out_shape.append(jax.ShapeDtypeStruct((b, t, LANES), F32))
        args.append(w_aux)
    res = pl.pallas_call(
        functools.partial(_nmm_kernel, has_aux=has_aux),
        grid=(b, t // tm, c // tn),
        in_specs=in_specs,
        out_specs=out_specs,
        out_shape=out_shape,
        scratch_shapes=[pltpu.VMEM((tm, d), BF16)],
        compiler_params=_cparams("parallel", "parallel", "arbitrary"),
        name=name,
    )(*args)
    return res if has_aux else res[0]


G_A, G_B, G_I, G_F = 0, 16, 32, 40


def _chunk_scan(x, pos, backward, op, ident):
    rows = x.shape[0]
    yf, yb = x, x
    s = 1
    while s < CHUNK:
        yf = op(yf, jnp.where(pos >= s, pltpu.roll(yf, s, 0), ident))
        yb = op(yb, jnp.where(pos + s < CHUNK, pltpu.roll(yb, rows - s, 0), ident))
        s *= 2
    return jnp.where(backward, yb, yf)


def _gates_kernel(z_ref, alog_ref, bias_ref, o_ref):
    z = z_ref[0] + bias_ref[...]
    lane = lax.broadcasted_iota(jnp.int32, z.shape, 1)
    pos = lax.broadcasted_iota(jnp.int32, z.shape, 0) & (CHUNK - 1)
    backward = ((lane < G_I) & ((lane & 15) >= 8)) | ((lane >= G_I) & ((lane & 7) >= 4))

    log_decay = -jnp.exp(alog_ref[...]) * _softplus(z)
    beta = _sigmoid(z)
    capped = GATE_CAP * jnp.tanh(z * (1.0 / GATE_CAP))
    log_forget = -_softplus(-capped)

    summand = jnp.where(lane < G_B, log_decay, jnp.where(lane >= G_F, log_forget, 0.0))
    csum = _chunk_scan(summand, pos, backward, jnp.add, 0.0)
    o_ref[0, :, 0:LANES] = jnp.where(lane < G_B, csum, beta)

    bc = pltpu.roll(csum, LANES - (G_F - G_I), 1)
    backward_i = (lane & 7) >= 4
    u = capped - bc
    umax = _chunk_scan(u, pos, backward_i, jnp.maximum, NEG_BIG)
    o_ref[0, :, LANES:2 * LANES] = bc
    o_ref[0, :, 2 * LANES:3 * LANES] = u
    o_ref[0, :, 3 * LANES:4 * LANES] = bc + umax


def _gates_call(z_aux, alog_row, bias_row):
    b, t, _ = z_aux.shape
    tm = _pick_tile(t, 512)
    return pl.pallas_call(
        _gates_kernel,
        grid=(b, t // tm),
        in_specs=[
            pl.BlockSpec((1, tm, LANES), lambda bi, i: (bi, i, 0)),
            pl.BlockSpec((1, LANES), lambda bi, i: (0, 0)),
            pl.BlockSpec((1, LANES), lambda bi, i: (0, 0)),
        ],
        out_specs=pl.BlockSpec((1, tm, 4 * LANES), lambda bi, i: (bi, i, 0)),
        out_shape=jax.ShapeDtypeStruct((b, t, 4 * LANES), F32),
        compiler_params=_cparams("parallel", "parallel"),
        name="gates",
    )(z_aux, alog_row, bias_row)


def _conv_kernel(z_ref, zp_ref, zn_ref, w_ref, o_ref):
    i = pl.program_id(1)
    j = pl.program_id(2)
    z = z_ref[0].astype(F32)
    rows = z.shape[0]
    row8 = lax.broadcasted_iota(jnp.int32, (8, z.shape[1]), 0)
    prev_row = jnp.where(i == 0, 0.0, zp_ref[0, HALO - 1:HALO, :].astype(F32))
    next_row = jnp.where(i == pl.num_programs(1) - 1, 0.0, zn_ref[0, 0:1, :].astype(F32))
    z_prev = pltpu.roll(z, 1, 0)
    z_prev = jnp.concatenate([jnp.where(row8 == 0, prev_row, z_prev[:8]), z_prev[8:]], axis=0)
    z_next = pltpu.roll(z, rows - 1, 0)
    z_next = jnp.concatenate([z_next[:rows - 8], jnp.where(row8 == 7, next_row, z_next[rows - 8:])], axis=0)
    y = z_prev * w_ref[0:1, :] + z * w_ref[1:2, :] + z_next * w_ref[2:3, :]
    y = y * _sigmoid(y)

    is_qk = j < 2
    for h in range(GDN_HEADS):
        cols = slice(h * GDN_DK, (h + 1) * GDN_DK)
        yh = y[:, cols]
        inv = lax.rsqrt(jnp.sum(yh * yh, axis=-1, keepdims=True) + EPS)
        o_ref[0, :, cols] = (yh * jnp.where(is_qk, inv, 1.0)).astype(o_ref.dtype)


def _conv_call(z, conv_w):
    b, t, _ = z.shape
    tt = _pick_tile(t, 512)
    g = GDN_QK
    nbh = tt // HALO
    return pl.pallas_call(
        _conv_kernel,
        grid=(b, t // tt, 3),
        in_specs=[
            pl.BlockSpec((1, tt, g), lambda bi, i, j: (bi, i, j)),
            pl.BlockSpec((1, HALO, g), lambda bi, i, j: (bi, jnp.maximum(i * nbh - 1, 0), j)),
            pl.BlockSpec((1, HALO, g), lambda bi, i, j: (bi, jnp.minimum((i + 1) * nbh, t // HALO - 1), j)),
            pl.BlockSpec((3, g), lambda bi, i, j: (0, j)),
        ],
        out_specs=pl.BlockSpec((1, tt, g), lambda bi, i, j: (bi, i, j)),
        out_shape=jax.ShapeDtypeStruct((b, t, 3 * g), BF16),
        compiler_params=_cparams("parallel", "parallel", "parallel"),
        name="gdn_conv",
    )(z, z, z, conv_w)


QUAD = 4
QW = QUAD * CHUNK
NQUAD = GDN_HEADS // QUAD
CHUNKS_PER_STEP = 4


def _quad_masks(reverse):
    ii = lax.broadcasted_iota(jnp.int32, (QW, QW), 0)
    jj = lax.broadcasted_iota(jnp.int32, (QW, QW), 1)
    same = (ii ^ jj) < CHUNK
    if reverse:
        return same & (jj > ii), same & (jj >= ii), ii == jj
    return same & (jj < ii), same & (jj <= ii), ii == jj


def _gdn_body(refs, with_out):
    (qf, kf, vf, qb, kb, vb, gcf, gcb, grf, grb, s0_ref) = refs[:11]
    if with_out:
        of_ref, ob_ref, sfin_ref = refs[11:]
    else:
        (sfin_ref,) = refs[11:]
        of_ref = ob_ref = None

    @pl.when(pl.program_id(1) == 0)
    def _():
        sfin_ref[...] = s0_ref[...]

    scale = GDN_DK ** -0.5
    groups = []
    cs = grf.shape[1]
    for ci, d, g in [(ci, d, g) for ci in range(cs) for d in range(2) for g in range(NQUAD)]:
        q_ref, k_ref, v_ref, gc_ref, gr_ref, o_ref = (
            (qb, kb, vb, gcb, grb, ob_ref) if d else (qf, kf, vf, gcf, grf, of_ref))
        ch = cs - 1 - ci if d else ci
        rows = slice(ch * CHUNK, (ch + 1) * CHUNK)
        heads = tuple(range(g * QUAD, (g + 1) * QUAD))

        def stack(ref):
            return jnp.concatenate([ref[0, rows, h * GDN_DK:(h + 1) * GDN_DK] for h in heads], axis=0)

        def col(base):
            lanes = [base + d * GDN_HEADS + h for h in heads]
            return jnp.concatenate([gc_ref[0, rows, l:l + 1] for l in lanes], axis=0)

        r = d * NQUAD + g
        groups.append(dict(
            d=d, g=g, heads=heads, rows=rows, o_ref=o_ref, k=stack(k_ref), v=stack(v_ref),
            q=stack(q_ref) if with_out else None, gcol=col(G_A), bcol=col(G_B),
            grow=gr_ref[0, ch, r:r + 1, :]))

    for grp in groups:
        strict, incl, diag = _quad_masks(bool(grp["d"]))
        decay = jnp.exp(jnp.where(incl, grp["gcol"] - grp["grow"], NEG_BIG))
        kk = _dot_nt(grp["k"], grp["k"])
        x = jnp.where(strict, kk * decay * (-grp["bcol"]), 0.0)
        grp["x"] = x
        grp["s"] = jnp.where(diag, 1.0, x)
        if with_out:
            grp["attn"] = (_dot_nt(grp["q"], grp["k"]) * decay * scale).astype(BF16)
        yield
    for grp in groups:
        xb = grp["x"].astype(BF16)
        grp["xm"] = _dot(xb, xb)
        yield
    m = 2
    while 2 * m < CHUNK:
        for grp in groups:
            xb = grp["xm"].astype(BF16)
            both = _dot(jnp.concatenate([grp["s"].astype(BF16), xb], axis=0), xb)
            grp["s"] = grp["s"] + both[:QW]
            grp["xm"] = both[QW:]
            yield
        m *= 2
    for grp in groups:
        t = grp["s"] + _dot(grp["s"].astype(BF16), grp["xm"].astype(BF16))
        bcol, gcol = grp["bcol"], grp["gcol"]
        rhs = jnp.concatenate([(grp["v"].astype(F32) * bcol).astype(BF16),
                               (grp["k"].astype(F32) * (bcol * jnp.exp(gcol))).astype(BF16)], axis=1)
        uw = _dot(t.astype(BF16), rhs)
        grp["u"] = uw[:, :GDN_DV]
        grp["w"] = uw[:, GDN_DV:]
        yield

    row = lax.broadcasted_iota(jnp.int32, (QW, GDN_DV), 0)
    for grp in groups:
        d, g, gcol = grp["d"], grp["g"], grp["gcol"]
        s4 = sfin_ref[0, d, g]
        sb = s4.astype(BF16)
        wb = grp["w"].astype(BF16)
        per_head = []
        for c in range(QUAD):
            lhs = wb[c * CHUNK:(c + 1) * CHUNK]
            if with_out:
                lhs = jnp.concatenate([lhs, grp["q"][c * CHUNK:(c + 1) * CHUNK]], axis=0)
            per_head.append(_dot(lhs, sb[:, c * GDN_DV:(c + 1) * GDN_DV]))

        def diag_blocks(base):
            return jnp.concatenate([r[base:base + CHUNK] for r in per_head], axis=0)

        v_new = grp["u"] - diag_blocks(0)
        end = 0 if d else CHUNK - 1
        g_end = [gcol[c * CHUNK + end:c * CHUNK + end + 1] for c in range(QUAD)]
        g_end_col = jnp.concatenate([jnp.broadcast_to(ge, (CHUNK, 1)) for ge in g_end], axis=0)
        vt = v_new * jnp.exp(g_end_col - gcol)
        vbd = jnp.concatenate(
            [jnp.where((row >= c * CHUNK) & (row < (c + 1) * CHUNK), vt, 0.0).astype(BF16) for c in range(QUAD)],
            axis=1)
        decay_lane = jnp.concatenate([jnp.broadcast_to(jnp.exp(ge), (1, GDN_DV)) for ge in g_end], axis=1)
        sfin_ref[0, d, g] = s4 * decay_lane + _dot_tn(grp["k"], vbd)
        if with_out:
            o = (jnp.exp(gcol) * scale) * diag_blocks(CHUNK) + _dot(grp["attn"], v_new.astype(BF16))
            for c, h in enumerate(grp["heads"]):
                grp["o_ref"][0, grp["rows"], h * GDN_DV:(h + 1) * GDN_DV] = o[c * CHUNK:(c + 1) * CHUNK].astype(BF16)
        yield


def _gdn_operands(qkv, gcol, grow, s0, with_out):
    b, t, _ = qkv.shape
    cs = CHUNKS_PER_STEP
    blk = cs * CHUNK
    nb = t // blk
    w = GDN_QK

    def fwd(*tail):
        return lambda bi, n: (bi, n) + tail

    def bwd(*tail):
        return lambda bi, n: (bi, nb - 1 - n) + tail

    s_spec = pl.BlockSpec((1, 2, NQUAD, GDN_DK, QUAD * GDN_DV), lambda bi, n: (bi, 0, 0, 0, 0))
    in_specs = (
        [pl.BlockSpec((1, blk, w), fwd(o)) for o in range(3)]
        + [pl.BlockSpec((1, blk, w), bwd(o)) for o in range(3)]
        + [pl.BlockSpec((1, blk, LANES), fwd(0)), pl.BlockSpec((1, blk, LANES), bwd(0)),
           pl.BlockSpec((1, cs, 4 * NQUAD, QW), fwd(0, 0)), pl.BlockSpec((1, cs, 4 * NQUAD, QW), bwd(0, 0)),
           s_spec])
    s_shape = jax.ShapeDtypeStruct(s0.shape, F32)
    if with_out:
        out_specs = [pl.BlockSpec((1, blk, w), fwd(0)), pl.BlockSpec((1, blk, w), bwd(0)), s_spec]
        o_shape = jax.ShapeDtypeStruct((b, t, GDN_V), BF16)
        out_shape = [o_shape, o_shape, s_shape]
    else:
        out_specs = [s_spec]
        out_shape = [s_shape]
    return [qkv] * 6 + [gcol, gcol, grow, grow, s0], in_specs, out_specs, out_shape


def _ml_body(refs, with_out):
    (qf, kf, vf, qb, kb, vb, gcf, gcb, grf, grb, c0_ref, n0_ref, m0_ref) = refs[:13]
    if with_out:
        of_ref, ob_ref, cfin_ref, nfin_ref, mfin_ref = refs[13:]
    else:
        cfin_ref, nfin_ref, mfin_ref = refs[13:]
        of_ref = ob_ref = None

    @pl.when(pl.program_id(1) == 0)
    def _():
        cfin_ref[...] = c0_ref[...]
        nfin_ref[...] = n0_ref[...]
        mfin_ref[...] = m0_ref[...]

    scale = ML_DK ** -0.5
    heads = tuple(range(ML_HEADS))

    def rows_of(c):
        return slice(c * CHUNK, (c + 1) * CHUNK)

    def spread(xs):
        return jnp.concatenate([jnp.broadcast_to(x, (CHUNK, 1)) for x in xs], axis=0)

    groups = []
    cs = grf.shape[1]
    for ci, d in [(ci, d) for ci in range(cs) for d in range(2)]:
        q_ref, k_ref, v_ref, gc_ref, gr_ref, o_ref = (
            (qb, kb, vb, gcb, grb, ob_ref) if d else (qf, kf, vf, gcf, grf, of_ref))
        ch = cs - 1 - ci if d else ci
        rows = slice(ch * CHUNK, (ch + 1) * CHUNK)

        def stack(ref, width):
            return jnp.concatenate([ref[0, rows, h * width:(h + 1) * width] for h in heads], axis=0)

        def col(base):
            lanes = [base + G_I + d * ML_HEADS + h for h in heads]
            return jnp.concatenate([gc_ref[0, rows, l:l + 1] for l in lanes], axis=0)

        grp = dict(d=d, rows=rows, o_ref=o_ref, k=stack(k_ref, ML_DK), v=stack(v_ref, ML_DV),
                   bcol=col(LANES), ucol=col(2 * LANES), dmcol=col(3 * LANES), urow=gr_ref[0, ch, d:d + 1, :])
        if with_out:
            grp["q"] = stack(q_ref, ML_DK)
            grp["qk"] = _dot_nt(grp["q"], grp["k"])
        groups.append(grp)
        yield

    row = lax.broadcasted_iota(jnp.int32, (QW, ML_DV), 0)
    for grp in groups:
        d, k4, v4, bcol, ucol, dmcol = grp["d"], grp["k"], grp["v"], grp["bcol"], grp["ucol"], grp["dmcol"]
        end = 0 if d else CHUNK - 1
        m = [mfin_ref[0, d, h, 0:1, 0:1] for h in heads]
        n = [nfin_ref[0, d, h, 0:1, :] for h in heads]
        tot = [bcol[c * CHUNK + end:c * CHUNK + end + 1] for c in heads]
        m_new = [jnp.maximum(tot[c] + m[c], dmcol[c * CHUNK + end:c * CHUNK + end + 1]) for c in heads]
        dec = [jnp.exp(tot[c] + m[c] - m_new[c]) for c in heads]
        wt = jnp.exp(spread(tot) + ucol - spread(m_new))
        vw = v4.astype(F32) * wt
        vbd = jnp.concatenate(
            [jnp.where((row >= c * CHUNK) & (row < (c + 1) * CHUNK), vw, 0.0).astype(BF16) for c in heads], axis=1)
        c4 = cfin_ref[0, d]
        dec_lane = jnp.concatenate([jnp.broadcast_to(dec[c], (1, ML_DV)) for c in heads], axis=1)
        cfin_ref[0, d] = c4 * dec_lane + _dot_tn(k4, vbd)
        kw = k4.astype(F32) * wt
        for c in heads:
            n_next = dec[c] * n[c] + jnp.sum(kw[rows_of(c)], axis=0, keepdims=True)
            nfin_ref[0, d, c] = jnp.broadcast_to(n_next, (8, ML_DK))
            mfin_ref[0, d, c] = jnp.broadcast_to(m_new[c], (8, LANES))
        yield
        if not with_out:
            continue
        q4 = grp["q"]
        _, incl, _ = _quad_masks(bool(d))
        m4 = spread(m)
        m_t = jnp.maximum(bcol + m4, dmcol)
        inter = jnp.exp(bcol + m4 - m_t)
        p = grp["qk"] * scale * jnp.exp(jnp.where(incl, bcol + grp["urow"] - m_t, NEG_BIG))
        c4b = c4.astype(BF16)
        qc = jnp.concatenate([_dot(q4[rows_of(c)], c4b[:, c * ML_DV:(c + 1) * ML_DV]) for c in heads], axis=0)
        num = (inter * scale) * qc + _dot(p.astype(BF16), v4)
        n_rows = jnp.concatenate([jnp.broadcast_to(n[c], (CHUNK, ML_DK)) for c in heads], axis=0)
        qn = jnp.sum(q4.astype(F32) * n_rows, axis=1, keepdims=True) * scale
        den = inter * qn + jnp.sum(p, axis=1, keepdims=True)
        out = num * (1.0 / jnp.maximum(jnp.abs(den), jnp.exp(-m_t)))
        for c in heads:
            grp["o_ref"][0, grp["rows"], c * ML_DV:(c + 1) * ML_DV] = out[rows_of(c)].astype(BF16)
        yield


def _ml_operands(z, col0, gates, urow, c0, n0, m0, with_out):
    b, t, _ = z.shape
    cs = CHUNKS_PER_STEP
    blk = cs * CHUNK
    nb = t // blk
    qoff = col0 * LANES // ML_QK
    koff = (col0 * LANES + ML_QK) // ML_QK
    voff = (col0 * LANES + 2 * ML_QK) // ML_V

    def fwd(*tail):
        return lambda bi, n: (bi, n) + tail

    def bwd(*tail):
        return lambda bi, n: (bi, nb - 1 - n) + tail

    def state_spec(shape):
        return pl.BlockSpec((1,) + shape, lambda bi, n: (bi,) + (0,) * len(shape))

    st = [(2, ML_DK, ML_HEADS * ML_DV), (2, ML_HEADS, 8, ML_DK), (2, ML_HEADS, 8, LANES)]
    st_specs = [state_spec(s) for s in st]
    st_shapes = [jax.ShapeDtypeStruct((b,) + s, F32) for s in st]
    in_specs = (
        [pl.BlockSpec((1, blk, ML_QK), fwd(qoff)), pl.BlockSpec((1, blk, ML_QK), fwd(koff)),
         pl.BlockSpec((1, blk, ML_V), fwd(voff)),
         pl.BlockSpec((1, blk, ML_QK), bwd(qoff)), pl.BlockSpec((1, blk, ML_QK), bwd(koff)),
         pl.BlockSpec((1, blk, ML_V), bwd(voff)),
         pl.BlockSpec((1, blk, 4 * LANES), fwd(0)), pl.BlockSpec((1, blk, 4 * LANES), bwd(0)),
         pl.BlockSpec((1, cs, 8, QW), fwd(0, 0)), pl.BlockSpec((1, cs, 8, QW), bwd(0, 0))]
        + st_specs)
    if with_out:
        out_specs = [pl.BlockSpec((1, blk, ML_V), fwd(0)), pl.BlockSpec((1, blk, ML_V), bwd(0))] + st_specs
        o_shape = jax.ShapeDtypeStruct((b, t, ML_V), BF16)
        out_shape = [o_shape, o_shape] + st_shapes
    else:
        out_specs = st_specs
        out_shape = st_shapes
    return [z] * 6 + [gates, gates, urow, urow, c0, n0, m0], in_specs, out_specs, out_shape


GDN_PIECES_PER_GROUP = 8
ML_PIECES_PER_GROUP = 3


def _scan_kernel(*refs, n_gdn_in, n_ml_in, n_gdn_out, with_out):
    n_in = n_gdn_in + n_ml_in
    gdn_refs = refs[:n_gdn_in] + refs[n_in:n_in + n_gdn_out]
    ml_refs = refs[n_gdn_in:n_in] + refs[n_in + n_gdn_out:]
    gdn = _gdn_body(gdn_refs, with_out)
    ml = _ml_body(ml_refs, with_out)
    n_gdn = GDN_PIECES_PER_GROUP * 2 * NQUAD * CHUNKS_PER_STEP
    n_ml = (ML_PIECES_PER_GROUP if with_out else ML_PIECES_PER_GROUP - 1) * 2 * CHUNKS_PER_STEP
    done = 0
    for k, _ in enumerate(gdn, 1):
        while done < min(k * n_ml // n_gdn, n_ml):
            next(ml, None)
            done += 1
    for _ in ml:
        pass


def _scan_call(gdn_args, ml_args, with_out):
    g_ops, g_in, g_out, g_shape = _gdn_operands(*gdn_args, with_out)
    m_ops, m_in, m_out, m_shape = _ml_operands(*ml_args, with_out)
    b, t, _ = g_ops[0].shape
    res = pl.pallas_call(
        functools.partial(_scan_kernel, n_gdn_in=len(g_ops), n_ml_in=len(m_ops), n_gdn_out=len(g_out),
                          with_out=with_out),
        grid=(b, t // (CHUNKS_PER_STEP * CHUNK)),
        in_specs=g_in + m_in,
        out_specs=g_out + m_out,
        out_shape=g_shape + m_shape,
        compiler_params=pltpu.CompilerParams(
            dimension_semantics=("parallel", "arbitrary"), vmem_limit_bytes=SCAN_VMEM_LIMIT),
        name="scan_out" if with_out else "scan_state",
    )(*g_ops, *m_ops)
    return res[:len(g_out)], res[len(g_out):]


def _head_rms(x, width):
    outs = []
    for h in range(x.shape[1] // width):
        xh = x[:, h * width:(h + 1) * width]
        ms = jnp.sum(xh * xh, axis=-1, keepdims=True) * (1.0 / width)
        outs.append(xh * lax.rsqrt(ms + EPS))
    return jnp.concatenate(outs, axis=-1)


def _merge_kernel(x_ref, of_ref, ob_ref, hf_ref, hb_ref, gz_ref, mo_ref, gg_ref, gm_ref,
                  gnw_ref, mnw_ref, wg_ref, wm_ref, wo_ref, gate_ref, o_ref):
    f32 = lambda ref: ref[0].astype(F32)
    og = _head_rms(f32(of_ref) + f32(ob_ref), GDN_DV) * gnw_ref[...]
    gz = f32(gz_ref)
    og = og * (gz * _sigmoid(gz))
    y_gdn = _dot(og.astype(BF16), wg_ref[...])
    hm = _head_rms(f32(hf_ref) + f32(hb_ref), ML_DV) * mnw_ref[...]
    hm = hm * _sigmoid(f32(mo_ref))
    y_ml = _dot(hm.astype(BF16), wm_ref[...])
    merged = _sigmoid(f32(gg_ref)) * y_gdn + _sigmoid(f32(gm_ref)) * y_ml
    x_mix = _dot(merged.astype(BF16), wo_ref[...])
    o_ref[0] = x_ref[0] + gate_ref[0] * x_mix


def _merge_call(x, o_f, o_b, h_f, h_b, z, zo_col0, gnw, mnw, wg, wm, wo, gate):
    b, t, d = x.shape
    tm = _pick_tile(t, 512)
    nz = zo_col0 * LANES // d

    def tok(bi, i):
        return (bi, i, 0)

    def zcol(k):
        return lambda bi, i: (bi, i, nz + k)

    tile = lambda imap: pl.BlockSpec((1, tm, d), imap)
    full = lambda shape: pl.BlockSpec(shape, lambda bi, i: (0,) * len(shape))
    return pl.pallas_call(
        _merge_kernel,
        grid=(b, t // tm),
        in_specs=[tile(tok), tile(tok), tile(tok), tile(tok), tile(tok),
                  tile(zcol(0)), tile(zcol(1)), tile(zcol(2)), tile(zcol(3)),
                  full((1, d)), full((1, d)), full((d, d)), full((d, d)), full((d, d)),
                  pl.BlockSpec((1, 1, d), lambda bi, i: (bi, 0, 0))],
        out_specs=tile(tok),
        out_shape=jax.ShapeDtypeStruct((b, t, d), F32),
        compiler_params=_cparams("parallel", "parallel"),
        name="merge",
    )(x, o_f, o_b, h_f, h_b, z, z, z, z, gnw, mnw, wg, wm, wo, gate)


def _ffn2_kernel(ug_ref, ugp_ref, ugn_ref, uv_ref, uvp_ref, uvn_ref, cwg_ref, cwv_ref, wd_ref,
                 x_ref, gate_ref, nw_ref, o_ref, acc_ref, act_ref):
    i = pl.program_id(1)
    j = pl.program_id(2)
    first = i == 0
    last = i == pl.num_programs(1) - 1
    n_img_rows = ug_ref.shape[1] // GRID_W
    col = lax.broadcasted_iota(jnp.int32, (GRID_W, LANES), 0)

    def row_conv(u_ref, up_ref, un_ref, cw_ref, cols):
        w = [cw_ref[k:k + 1, cols] for k in range(9)]
        cache = {}

        def strip(r):
            if r not in cache:
                if r < 0:
                    cache[r] = jnp.where(first, 0.0, up_ref[0, :, cols].astype(F32))
                elif r == n_img_rows:
                    cache[r] = jnp.where(last, 0.0, un_ref[0, :, cols].astype(F32))
                else:
                    cache[r] = u_ref[0, r * GRID_W:(r + 1) * GRID_W, cols].astype(F32)
            return cache[r]

        def conv(r):
            taps = [strip(r - 1), strip(r), strip(r + 1)]
            side = [taps[0] * w[dc] + taps[1] * w[3 + dc] + taps[2] * w[6 + dc] for dc in range(3)]
            left = jnp.where(col == 0, 0.0, pltpu.roll(side[0], 1, 0))
            right = jnp.where(col == GRID_W - 1, 0.0, pltpu.roll(side[2], GRID_W - 1, 0))
            return side[1] + left + right

        return conv

    for cb in range(ug_ref.shape[2] // LANES):
        cols = slice(cb * LANES, (cb + 1) * LANES)
        conv_g = row_conv(ug_ref, ugp_ref, ugn_ref, cwg_ref, cols)
        conv_v = row_conv(uv_ref, uvp_ref, uvn_ref, cwv_ref, cols)
        for r in range(n_img_rows):
            g = conv_g(r)
            act_ref[r * GRID_W:(r + 1) * GRID_W, cols] = (g * _sigmoid(g) * conv_v(r)).astype(BF16)
    part = _dot(act_ref[...], wd_ref[j])

    @pl.when(j == 0)
    def _():
        acc_ref[...] = part

    @pl.when(j > 0)
    def _():
        acc_ref[...] += part

    @pl.when(j == pl.num_programs(2) - 1)
    def _():
        x = x_ref[0] + gate_ref[0] * acc_ref[...]
        ms = jnp.mean(x * x, axis=-1, keepdims=True)
        o_ref[0] = x * lax.rsqrt(ms + EPS) * nw_ref[...]


def _ffn2_call(u, conv_w, w_down, x, gate, nw):
    b, t, d = x.shape
    f = w_down.shape[0]
    tm = _pick_tile(t, 1024)
    tc = 256
    nj = f // tc
    w_tiles = w_down.reshape(nj, tc, d)
    rpt = tm // GRID_W
    nrows = t // GRID_W

    def main(off):
        return pl.BlockSpec((1, tm, tc), lambda bi, i, j: (bi, i, off + j))

    def prev(off):
        return pl.BlockSpec((1, GRID_W, tc), lambda bi, i, j: (bi, jnp.maximum(i * rpt - 1, 0), off + j))

    def nxt(off):
        return pl.BlockSpec((1, GRID_W, tc), lambda bi, i, j: (bi, jnp.minimum((i + 1) * rpt, nrows - 1), off + j))

    return pl.pallas_call(
        _ffn2_kernel,
        grid=(b, t // tm, nj),
        in_specs=[main(0), prev(0), nxt(0), main(nj), prev(nj), nxt(nj),
                  pl.BlockSpec((9, tc), lambda bi, i, j: (0, j)),
                  pl.BlockSpec((9, tc), lambda bi, i, j: (0, nj + j)),
                  pl.BlockSpec((nj, tc, d), lambda bi, i, j: (0, 0, 0), pipeline_mode=pl.Buffered(1)),
                  pl.BlockSpec((1, tm, d), lambda bi, i, j: (bi, i, 0)),
                  pl.BlockSpec((1, 1, d), lambda bi, i, j: (bi, 0, 0)),
                  pl.BlockSpec((1, d), lambda bi, i, j: (0, 0))],
        out_specs=pl.BlockSpec((1, tm, d), lambda bi, i, j: (bi, i, 0)),
        out_shape=jax.ShapeDtypeStruct((b, t, d), F32),
        scratch_shapes=[pltpu.VMEM((tm, d), F32), pltpu.VMEM((tm, tc), BF16)],
        compiler_params=_cparams("parallel", "parallel", "arbitrary"),
        name="ffn2",
    )(u, u, u, u, u, u, conv_w, conv_w, w_tiles, x, gate, nw)


def _ffn_kernel(x_ref, xp_ref, xn_ref, nw2_ref, sh_ref, sc_ref, wup_ref, cwg_ref, cwv_ref, wd_ref,
                gate_ref, nwo_ref, o_ref, hn_ref, ua_ref, ub_ref, acta_ref, actb_ref, acc_ref):
    i = pl.program_id(1)
    j = pl.program_id(2)
    nj = wd_ref.shape[0]
    first = i == 0
    last = i == pl.num_programs(1) - 1
    tm = x_ref.shape[1]
    n_img_rows = tm // GRID_W
    col = lax.broadcasted_iota(jnp.int32, (GRID_W, LANES), 0)

    def up_project(u_ref, jt):
        hn = hn_ref[...]
        u_ref[0] = _dot(hn, wup_ref[jt])
        u_ref[1] = _dot(hn, wup_ref[nj + jt])

    @pl.when(j == 0)
    def _():
        def norm_mod(x):
            ms = jnp.mean(x * x, axis=-1, keepdims=True)
            y = x * lax.rsqrt(ms + EPS) * nw2_ref[...]
            return (y * (1.0 + sc_ref[0]) + sh_ref[0]).astype(BF16)

        hn_ref[0:GRID_W] = norm_mod(xp_ref[0])
        hn_ref[GRID_W:GRID_W + tm] = norm_mod(x_ref[0])
        hn_ref[GRID_W + tm:] = norm_mod(xn_ref[0])
        up_project(ua_ref, 0)
        actb_ref[...] = jnp.zeros(actb_ref.shape, BF16)
        acc_ref[...] = jnp.zeros(acc_ref.shape, F32)

    def row_conv(u_ref, cw_ref, cols):
        w = [cw_ref[k:k + 1, cols] for k in range(9)]
        cache = {}

        def strip(r):
            if r not in cache:
                s = u_ref[(r + 1) * GRID_W:(r + 2) * GRID_W, cols]
                if r < 0:
                    s = jnp.where(first, 0.0, s)
                elif r == n_img_rows:
                    s = jnp.where(last, 0.0, s)
                cache[r] = s
            return cache[r]

        def conv(r):
            taps = [strip(r - 1), strip(r), strip(r + 1)]
            side = [taps[0] * w[dc] + taps[1] * w[3 + dc] + taps[2] * w[6 + dc] for dc in range(3)]
            left = jnp.where(col == 0, 0.0, pltpu.roll(side[0], 1, 0))
            right = jnp.where(col == GRID_W - 1, 0.0, pltpu.roll(side[2], GRID_W - 1, 0))
            return side[1] + left + right

        return conv

    def step(u_cur, act_cur, u_next, act_prev):
        jn = jnp.minimum(j + 1, nj - 1)
        jp = jnp.maximum(j - 1, 0)
        ext = tm + 2 * GRID_W
        half_e, half_t = ext // FFN_ROW_SPLIT, tm // FFN_ROW_SPLIT

        def up_piece(which, lo):
            rows = slice(lo, lo + half_e)
            return lambda: u_next.__setitem__((which, rows), _dot(hn_ref[rows], wup_ref[which * nj + jn]))

        def down_piece(lo):
            rows = slice(lo, lo + half_t)

            def run():
                acc_ref[rows] += _dot(act_prev[rows], wd_ref[jp])
            return run

        def conv_piece(cb, conv_g, conv_v, r):
            cols = slice(cb * LANES, (cb + 1) * LANES)

            def run():
                g = conv_g(r)
                act_cur[r * GRID_W:(r + 1) * GRID_W, cols] = (g * _sigmoid(g) * conv_v(r)).astype(BF16)
            return run

        mxu_work = ([up_piece(which, k * half_e) for which in range(2) for k in range(FFN_ROW_SPLIT)]
                    + [down_piece(k * half_t) for k in range(FFN_ROW_SPLIT)])
        vpu_work = []
        for cb in range(act_cur.shape[1] // LANES):
            cols = slice(cb * LANES, (cb + 1) * LANES)
            conv_g = row_conv(u_cur.at[0], cwg_ref, cols)
            conv_v = row_conv(u_cur.at[1], cwv_ref, cols)
            vpu_work += [conv_piece(cb, conv_g, conv_v, r) for r in range(n_img_rows)]
        per = -(-len(vpu_work) // len(mxu_work))
        for k, piece in enumerate(mxu_work):
            piece()
            for run in vpu_work[k * per:(k + 1) * per]:
                run()

    @pl.when((j & 1) == 0)
    def _():
        step(ua_ref, acta_ref, ub_ref, actb_ref)

    @pl.when((j & 1) == 1)
    def _():
        step(ub_ref, actb_ref, ua_ref, acta_ref)

    @pl.when(j == nj - 1)
    def _():
        act_last = acta_ref if (nj - 1) % 2 == 0 else actb_ref
        acc_ref[...] += _dot(act_last[...], wd_ref[nj - 1])
        x = x_ref[0] + gate_ref[0] * acc_ref[...]
        ms = jnp.mean(x * x, axis=-1, keepdims=True)
        o_ref[0] = x * lax.rsqrt(ms + EPS) * nwo_ref[...]


def _ffn_call(x, nw2, shift, scale, w_up, conv_w, w_down, gate, nwo):
    b, t, d = x.shape
    f = w_down.shape[0]
    tm = _pick_tile(t, 1024)
    tc = 256
    nj = f // tc
    rpt = tm // GRID_W
    nrows = t // GRID_W
    wup_tiles = w_up.reshape(d, 2 * nj, tc).transpose(1, 0, 2)
    wd_tiles = w_down.reshape(nj, tc, d)
    const = lambda *shape: pl.BlockSpec(shape, lambda bi, i, j: (0,) * len(shape))
    resident = lambda *shape: pl.BlockSpec(shape, lambda bi, i, j: (0,) * len(shape), pipeline_mode=pl.Buffered(1))
    per_batch = pl.BlockSpec((1, 1, d), lambda bi, i, j: (bi, 0, 0))
    return pl.pallas_call(
        _ffn_kernel,
        grid=(b, t // tm, nj),
        in_specs=[pl.BlockSpec((1, tm, d), lambda bi, i, j: (bi, i, 0)),
                  pl.BlockSpec((1, GRID_W, d), lambda bi, i, j: (bi, jnp.maximum(i * rpt - 1, 0), 0)),
                  pl.BlockSpec((1, GRID_W, d), lambda bi, i, j: (bi, jnp.minimum((i + 1) * rpt, nrows - 1), 0)),
                  const(1, d), per_batch, per_batch,
                  resident(2 * nj, d, tc),
                  pl.BlockSpec((9, tc), lambda bi, i, j: (0, j)),
                  pl.BlockSpec((9, tc), lambda bi, i, j: (0, nj + j)),
                  resident(nj, tc, d),
                  per_batch, const(1, d)],
        out_specs=pl.BlockSpec((1, tm, d), lambda bi, i, j: (bi, i, 0)),
        out_shape=jax.ShapeDtypeStruct((b, t, d), F32),
        scratch_shapes=[pltpu.VMEM((tm + 2 * GRID_W, d), BF16),
                        pltpu.VMEM((2, tm + 2 * GRID_W, tc), F32), pltpu.VMEM((2, tm + 2 * GRID_W, tc), F32),
                        pltpu.VMEM((tm, tc), BF16), pltpu.VMEM((tm, tc), BF16), pltpu.VMEM((tm, d), F32)],
        compiler_params=pltpu.CompilerParams(
            dimension_semantics=("parallel", "parallel", "arbitrary"), vmem_limit_bytes=FFN_VMEM_LIMIT),
        name="ffn",
    )(x, x, x, nw2, shift, scale, wup_tiles, conv_w, conv_w, wd_tiles, gate, nwo)


def _gdn_gate_rows(gates):
    b, t, _ = gates.shape
    nc = t // CHUNK

    def rows(base):
        q = gates[:, :, base:base + 2 * GDN_HEADS].reshape(b, nc, CHUNK, 2, NQUAD, QUAD)
        return q.transpose(0, 1, 3, 4, 5, 2).reshape(b, nc, 2 * NQUAD, QW)

    return jnp.concatenate([rows(G_A), rows(G_B)], axis=2)


def _ml_gate_rows(gates):
    b, t, _ = gates.shape
    nc = t // CHUNK
    base = 2 * LANES + G_I
    u = gates[:, :, base:base + 2 * ML_HEADS].reshape(b, nc, CHUNK, 2, ML_HEADS)
    u = u.transpose(0, 1, 3, 4, 2).reshape(b, nc, 2, QW)
    return jnp.pad(u, ((0, 0), (0, 0), (0, 6), (0, 0)))


def _mixer_states(x_seq, nw, shift, scale, w_state, w_aux, conv_w, alog_row, bias_row, states,
                  with_out, w_full=None):
    w = w_full if with_out else w_state
    z, z_aux = _nmm_call(x_seq, nw, shift, scale, w, w_aux, name="in_proj")
    gates = _gates_call(z_aux, alog_row, bias_row)
    qkv = _conv_call(z, conv_w)
    s_gdn, c_ml, n_ml, m_ml = states
    ml_col0 = (2 * GDN_QK + GDN_V) // LANES
    gdn_res, ml_res = _scan_call(
        (qkv, gates[:, :, :LANES], _gdn_gate_rows(gates), s_gdn),
        (z, ml_col0, gates, _ml_gate_rows(gates), c_ml, n_ml, m_ml), with_out)
    return z, gdn_res, ml_res


def kernel(x, c, ctx, c_ctx, w_ada, b_ada, norm1_w, w_in, gdn_conv, gdn_a_log, gdn_dt_bias, gdn_norm_w,
           ml_igate_b, ml_fgate_b, ml_norm_w, w_branch_gdn, w_branch_ml, w_out, norm2_w, w_up, ffn_conv,
           w_down, norm_out_w):
    bsz, _, d = x.shape
    depth = w_ada.shape[0]
    assert depth == 1, "single-layer problem: the context stream is never updated"
    l = 0

    sizes = (2 * GDN_QK + GDN_V, 2 * GDN_HEADS, 2 * GDN_HEADS, ML_QK, ML_QK, ML_V, 2 * ML_HEADS, 2 * ML_HEADS,
             GDN_V, ML_V, d, d)
    offs = [0]
    for s in sizes:
        offs.append(offs[-1] + s)
    wi = w_in[l]
    seg = lambda k: wi[:, offs[k]:offs[k + 1]]
    w_state = jnp.concatenate([seg(0), seg(3), seg(4), seg(5)], axis=1).astype(BF16)
    w_full = jnp.concatenate([seg(0), seg(3), seg(4), seg(5), seg(8), seg(9), seg(10), seg(11)], axis=1).astype(BF16)
    n_gate = 4 * GDN_HEADS + 4 * ML_HEADS
    w_aux = jnp.concatenate([seg(1), seg(2), seg(6), seg(7), jnp.zeros((d, LANES - n_gate), F32)], axis=1).astype(BF16)
    pad = lambda v, n: jnp.pad(v.reshape(1, -1).astype(F32), ((0, 0), (0, n - v.size)))
    alog_row = pad(gdn_a_log[l], LANES)
    bias_row = pad(jnp.concatenate([gdn_dt_bias[l].reshape(-1), jnp.zeros((2 * GDN_HEADS,), F32),
                                    ml_igate_b[l].reshape(-1), ml_fgate_b[l].reshape(-1)]), LANES)
    row = lambda v: v.reshape(1, -1).astype(F32)

    c_all = jnp.concatenate([c, c_ctx[None], jnp.zeros((8 - bsz - 1, d), F32)], axis=0)
    mods = _mod_call(c_all, w_ada[l].astype(BF16), row(b_ada[l]))
    mod_x = mods[:bsz].reshape(bsz, N_MOD, 1, d)
    mod_c = jnp.broadcast_to(mods[bsz].reshape(1, N_MOD, 1, d), (bsz, N_MOD, 1, d))

    zero_states = (jnp.zeros((bsz, 2, NQUAD, GDN_DK, QUAD * GDN_DV), F32),
                   jnp.zeros((bsz, 2, ML_DK, ML_HEADS * ML_DV), F32),
                   jnp.zeros((bsz, 2, ML_HEADS, 8, ML_DK), F32),
                   jnp.zeros((bsz, 2, ML_HEADS, 8, LANES), F32))
    common = (w_state, w_aux, gdn_conv[l].astype(F32), alog_row, bias_row)

    _, (s_gdn,), (c_ml, n_ml, m_ml) = _mixer_states(
        ctx, row(norm1_w[l]), mod_c[:, 0], mod_c[:, 1], *common, zero_states, False)

    z, (o_f, o_b, _), (h_f, h_b, _, _, _) = _mixer_states(
        x, row(norm1_w[l]), mod_x[:, 0], mod_x[:, 1], *common, (s_gdn, c_ml, n_ml, m_ml), True, w_full)
    x1 = _merge_call(x, o_f, o_b, h_f, h_b, z, STATE_COLS // LANES,
                     row(jnp.tile(gdn_norm_w[l], GDN_HEADS)), row(ml_norm_w[l]),
                     w_branch_gdn[l].astype(BF16), w_branch_ml[l].astype(BF16), w_out[l].astype(BF16),
                     mod_x[:, 2])
    return _ffn_call(x1, row(norm2_w[l]), mod_x[:, 3], mod_x[:, 4], w_up[l].astype(BF16),
                     ffn_conv[l].reshape(9, -1).astype(F32), w_down[l].astype(BF16), mod_x[:, 5], row(norm_out_w))
```

```python
import functools

import jax
import jax.numpy as jnp
from jax import lax
from jax.experimental import pallas as pl
from jax.experimental.pallas import tpu as pltpu

F32 = jnp.float32
BF16 = jnp.bfloat16

GDN_HEADS = 8
GDN_DK = 128
GDN_DV = 128
ML_HEADS = 4
ML_DK = 128
ML_DV = 256
CHUNK = 64
GATE_CAP = 15.0
GRID_W = 64
N_MOD = 6
EPS = 1e-6
LANES = 128
HALO = 16
NEG_BIG = -1e30

GDN_QK = GDN_HEADS * GDN_DK
GDN_V = GDN_HEADS * GDN_DV
ML_QK = ML_HEADS * ML_DK
ML_V = ML_HEADS * ML_DV
STATE_COLS = 2 * GDN_QK + GDN_V + 2 * ML_QK + ML_V
OUT_COLS = GDN_V + ML_V + 2 * 1024

VMEM_LIMIT = 48 * 1024 * 1024
FFN_VMEM_LIMIT = 56 * 1024 * 1024
FFN_ROW_SPLIT = 4
SCAN_VMEM_LIMIT = 58 * 1024 * 1024


def _cparams(*sem):
    return pltpu.CompilerParams(dimension_semantics=sem, vmem_limit_bytes=VMEM_LIMIT)


def _dot(a, b):
    return jnp.dot(a, b, preferred_element_type=F32)


def _dot_nt(a, b):
    return lax.dot_general(a, b, (((1,), (1,)), ((), ())), preferred_element_type=F32)


def _dot_tn(a, b):
    return lax.dot_general(a, b, (((0,), (0,)), ((), ())), preferred_element_type=F32)


def _sigmoid(x):
    return 0.5 * jnp.tanh(0.5 * x) + 0.5


def _softplus(x):
    return jnp.maximum(x, 0.0) + jnp.log1p(jnp.exp(-jnp.abs(x)))


def _pick_tile(n, pref):
    t = min(n, pref)
    while n % t:
        t //= 2
    return t


def _pick_cols(c, cap):
    t = cap - cap % LANES
    while c % t:
        t -= LANES
    return t


def _mod_kernel(c_ref, w_ref, b_ref, o_ref):
    c = c_ref[...]
    s = c * _sigmoid(c)
    o_ref[...] = _dot(s.astype(BF16), w_ref[...]) + b_ref[...]


def _mod_call(c_all, w_ada, b_ada):
    rows, d = c_all.shape
    n = w_ada.shape[1]
    tn = 1024
    return pl.pallas_call(
        _mod_kernel,
        grid=(n // tn,),
        in_specs=[
            pl.BlockSpec((rows, d), lambda j: (0, 0)),
            pl.BlockSpec((d, tn), lambda j: (0, j)),
            pl.BlockSpec((1, tn), lambda j: (0, j)),
        ],
        out_specs=pl.BlockSpec((rows, tn), lambda j: (0, j)),
        out_shape=jax.ShapeDtypeStruct((rows, n), F32),
        compiler_params=_cparams("parallel"),
        name="mod",
    )(c_all, w_ada, b_ada)


def _nmm_kernel(x_ref, nw_ref, sh_ref, sc_ref, w_ref, *rest, has_aux):
    if has_aux:
        wa_ref, o_ref, oa_ref, hn_ref = rest
    else:
        o_ref, hn_ref = rest

    @pl.when(pl.program_id(2) == 0)
    def _():
        x = x_ref[0]
        ms = jnp.mean(x * x, axis=-1, keepdims=True)
        y = x * lax.rsqrt(ms + EPS) * nw_ref[...]
        hb = (y * (1.0 + sc_ref[0]) + sh_ref[0]).astype(BF16)
        hn_ref[...] = hb
        if has_aux:
            oa_ref[0] = _dot(hb, wa_ref[...])

    o_ref[0] = _dot(hn_ref[...], w_ref[pl.program_id(2)]).astype(o_ref.dtype)


def _nmm_call(x, nw, shift, scale, w, w_aux=None, out_dtype=BF16, name="nmm"):
    b, t, d = x.shape
    c = w.shape[1]
    tm = _pick_tile(t, 1024)
    tn = _pick_cols(c, 1536)
    nj = c // tn
    has_aux = w_aux is not None
    w_tiles = w.reshape(d, nj, tn).transpose(1, 0, 2)
    in_specs = [
        pl.BlockSpec((1, tm, d), lambda bi, i, j: (bi, i, 0)),
        pl.BlockSpec((1, d), lambda bi, i, j: (0, 0)),
        pl.BlockSpec((1, 1, d), lambda bi, i, j: (bi, 0, 0)),
        pl.BlockSpec((1, 1, d), lambda bi, i, j: (bi, 0, 0)),
        pl.BlockSpec((nj, d, tn), lambda bi, i, j: (0, 0, 0), pipeline_mode=pl.Buffered(1)),
    ]
    out_specs = [pl.BlockSpec((1, tm, tn), lambda bi, i, j: (bi, i, j))]
    out_shape = [jax.ShapeDtypeStruct((b, t, c), out_dtype)]
    args = [x, nw, shift, scale, w_tiles]
    if has_aux:
        in_specs.append(pl.BlockSpec((d, LANES), lambda bi, i, j: (0, 0)))
        out_specs.append(pl.BlockSpec((1, tm, LANES), lambda bi, i, j: (bi, i, 0)))
        out_shape.append(jax.ShapeDtypeStruct((b, t, LANES), F32))
        args.append(w_aux)
    res = pl.pallas_call(
        functools.partial(_nmm_kernel, has_aux=has_aux),
        grid=(b, t // tm, c // tn),
        in_specs=in_specs,
        out_specs=out_specs,
        out_shape=out_shape,
        scratch_shapes=[pltpu.VMEM((tm, d), BF16)],
        compiler_params=_cparams("parallel", "parallel", "arbitrary"),
        name=name,
    )(*args)
    return res if has_aux else res[0]


G_A, G_B, G_I, G_F = 0, 16, 32, 40


def _chunk_scan(x, pos, backward, op, ident):
    rows = x.shape[0]
    yf, yb = x, x
    s = 1
    while s < CHUNK:
        yf = op(yf, jnp.where(pos >= s, pltpu.roll(yf, s, 0), ident))
        yb = op(yb, jnp.where(pos + s < CHUNK, pltpu.roll(yb, rows - s, 0), ident))
        s *= 2
    return jnp.where(backward, yb, yf)


def _gates_kernel(z_ref, alog_ref, bias_ref, o_ref):
    z = z_ref[0] + bias_ref[...]
    lane = lax.broadcasted_iota(jnp.int32, z.shape, 1)
    pos = lax.broadcasted_iota(jnp.int32, z.shape, 0) & (CHUNK - 1)
    backward = ((lane < G_I) & ((lane & 15) >= 8)) | ((lane >= G_I) & ((lane & 7) >= 4))

    log_decay = -jnp.exp(alog_ref[...]) * _softplus(z)
    beta = _sigmoid(z)
    capped = GATE_CAP * jnp.tanh(z * (1.0 / GATE_CAP))
    log_forget = -_softplus(-capped)

    summand = jnp.where(lane < G_B, log_decay, jnp.where(lane >= G_F, log_forget, 0.0))
    csum = _chunk_scan(summand, pos, backward, jnp.add, 0.0)
    o_ref[0, :, 0:LANES] = jnp.where(lane < G_B, csum, beta)

    bc = pltpu.roll(csum, LANES - (G_F - G_I), 1)
    backward_i = (lane & 7) >= 4
    u = capped - bc
    umax = _chunk_scan(u, pos, backward_i, jnp.maximum, NEG_BIG)
    o_ref[0, :, LANES:2 * LANES] = bc
    o_ref[0, :, 2 * LANES:3 * LANES] = u
    o_ref[0, :, 3 * LANES:4 * LANES] = bc + umax


def _gates_call(z_aux, alog_row, bias_row):
    b, t, _ = z_aux.shape
    tm = _pick_tile(t, 512)
    return pl.pallas_call(
        _gates_kernel,
        grid=(b, t // tm),
        in_specs=[
            pl.BlockSpec((1, tm, LANES), lambda bi, i: (bi, i, 0)),
            pl.BlockSpec((1, LANES), lambda bi, i: (0, 0)),
            pl.BlockSpec((1, LANES), lambda bi, i: (0, 0)),
        ],
        out_specs=pl.BlockSpec((1, tm, 4 * LANES), lambda bi, i: (bi, i, 0)),
        out_shape=jax.ShapeDtypeStruct((b, t, 4 * LANES), F32),
        compiler_params=_cparams("parallel", "parallel"),
        name="gates",
    )(z_aux, alog_row, bias_row)


def _conv_kernel(z_ref, zp_ref, zn_ref, w_ref, o_ref):
    i = pl.program_id(1)
    j = pl.program_id(2)
    z = z_ref[0].astype(F32)
    rows = z.shape[0]
    row8 = lax.broadcasted_iota(jnp.int32, (8, z.shape[1]), 0)
    prev_row = jnp.where(i == 0, 0.0, zp_ref[0, HALO - 1:HALO, :].astype(F32))
    next_row = jnp.where(i == pl.num_programs(1) - 1, 0.0, zn_ref[0, 0:1, :].astype(F32))
    z_prev = pltpu.roll(z, 1, 0)
    z_prev = jnp.concatenate([jnp.where(row8 == 0, prev_row, z_prev[:8]), z_prev[8:]], axis=0)
    z_next = pltpu.roll(z, rows - 1, 0)
    z_next = jnp.concatenate([z_next[:rows - 8], jnp.where(row8 == 7, next_row, z_next[rows - 8:])], axis=0)
    y = z_prev * w_ref[0:1, :] + z * w_ref[1:2, :] + z_next * w_ref[2:3, :]
    y = y * _sigmoid(y)

    is_qk = j < 2
    for h in range(GDN_HEADS):
        cols = slice(h * GDN_DK, (h + 1) * GDN_DK)
        yh = y[:, cols]
        inv = lax.rsqrt(jnp.sum(yh * yh, axis=-1, keepdims=True) + EPS)
        o_ref[0, :, cols] = (yh * jnp.where(is_qk, inv, 1.0)).astype(o_ref.dtype)


def _conv_call(z, conv_w):
    b, t, _ = z.shape
    tt = _pick_tile(t, 512)
    g = GDN_QK
    nbh = tt // HALO
    return pl.pallas_call(
        _conv_kernel,
        grid=(b, t // tt, 3),
        in_specs=[
            pl.BlockSpec((1, tt, g), lambda bi, i, j: (bi, i, j)),
            pl.BlockSpec((1, HALO, g), lambda bi, i, j: (bi, jnp.maximum(i * nbh - 1, 0), j)),
            pl.BlockSpec((1, HALO, g), lambda bi, i, j: (bi, jnp.minimum((i + 1) * nbh, t // HALO - 1), j)),
            pl.BlockSpec((3, g), lambda bi, i, j: (0, j)),
        ],
        out_specs=pl.BlockSpec((1, tt, g), lambda bi, i, j: (bi, i, j)),
        out_shape=jax.ShapeDtypeStruct((b, t, 3 * g), BF16),
        compiler_params=_cparams("parallel", "parallel", "parallel"),
        name="gdn_conv",
    )(z, z, z, conv_w)


QUAD = 4
QW = QUAD * CHUNK
NQUAD = GDN_HEADS // QUAD
INV_BASE = 8
CHUNKS_PER_STEP = 4


def _quad_masks(reverse):
    ii = lax.broadcasted_iota(jnp.int32, (QW, QW), 0)
    jj = lax.broadcasted_iota(jnp.int32, (QW, QW), 1)
    same = (ii ^ jj) < CHUNK
    if reverse:
        return same & (jj > ii), same & (jj >= ii), ii == jj
    return same & (jj < ii), same & (jj <= ii), ii == jj


def _gdn_body(refs, with_out):
    (qf, kf, vf, qb, kb, vb, gcf, gcb, grf, grb, s0_ref) = refs[:11]
    if with_out:
        of_ref, ob_ref, sfin_ref = refs[11:]
    else:
        (sfin_ref,) = refs[11:]
        of_ref = ob_ref = None

    @pl.when(pl.program_id(1) == 0)
    def _():
        sfin_ref[...] = s0_ref[...]

    scale = GDN_DK ** -0.5
    groups = []
    cs = grf.shape[1]
    for ci, d, g in [(ci, d, g) for ci in range(cs) for d in range(2) for g in range(NQUAD)]:
        q_ref, k_ref, v_ref, gc_ref, gr_ref, o_ref = (
            (qb, kb, vb, gcb, grb, ob_ref) if d else (qf, kf, vf, gcf, grf, of_ref))
        ch = cs - 1 - ci if d else ci
        rows = slice(ch * CHUNK, (ch + 1) * CHUNK)
        heads = tuple(range(g * QUAD, (g + 1) * QUAD))

        def stack(ref):
            return jnp.concatenate([ref[0, rows, h * GDN_DK:(h + 1) * GDN_DK] for h in heads], axis=0)

        def col(base):
            lanes = [base + d * GDN_HEADS + h for h in heads]
            return jnp.concatenate([gc_ref[0, rows, l:l + 1] for l in lanes], axis=0)

        r = d * NQUAD + g
        groups.append(dict(
            d=d, g=g, heads=heads, rows=rows, o_ref=o_ref, k=stack(k_ref), v=stack(v_ref),
            q=stack(q_ref) if with_out else None, gcol=col(G_A), bcol=col(G_B),
            grow=gr_ref[0, ch, r:r + 1, :]))

    for grp in groups:
        strict, incl, _ = _quad_masks(bool(grp["d"]))
        decay = jnp.exp(jnp.where(incl, grp["gcol"] - grp["grow"], NEG_BIG))
        kk = _dot_nt(grp["k"], grp["k"])
        x = jnp.where(strict, kk * decay * (-grp["bcol"]), 0.0)
        grp["x"] = x
        if with_out:
            grp["attn"] = (_dot_nt(grp["q"], grp["k"]) * decay * scale).astype(BF16)
        yield
    ii = lax.broadcasted_iota(jnp.int32, (QW, QW), 0)
    jj = lax.broadcasted_iota(jnp.int32, (QW, QW), 1)
    blk_xor = ii ^ jj
    for grp in groups:
        xb = jnp.where(blk_xor < INV_BASE, grp["x"], 0.0)
        grp["s"] = jnp.where(ii == jj, 1.0, xb)
        xbb = xb.astype(BF16)
        grp["xm"] = _dot(xbb, xbb)
        yield
    for grp in groups:
        xb = grp["xm"].astype(BF16)
        both = _dot(jnp.concatenate([grp["s"].astype(BF16), xb], axis=0), xb)
        grp["s"] = grp["s"] + both[:QW]
        grp["xm"] = both[QW:]
        yield
    for grp in groups:
        grp["s"] = grp["s"] + _dot(grp["s"].astype(BF16), grp["xm"].astype(BF16))
        yield
    b = INV_BASE
    while b < CHUNK:
        for grp in groups:
            xoff = jnp.where((blk_xor < 2 * b) & (blk_xor >= b), grp["x"], 0.0)
            tb = grp["s"].astype(BF16)
            y = _dot(xoff.astype(BF16), tb)
            grp["s"] = grp["s"] + _dot(tb, y.astype(BF16))
            yield
        b *= 2
    for grp in groups:
        t = grp["s"]
        bcol, gcol = grp["bcol"], grp["gcol"]
        rhs = jnp.concatenate([(grp["v"].astype(F32) * bcol).astype(BF16),
                               (grp["k"].astype(F32) * (bcol * jnp.exp(gcol))).astype(BF16)], axis=1)
        uw = _dot(t.astype(BF16), rhs)
        grp["u"] = uw[:, :GDN_DV]
        grp["w"] = uw[:, GDN_DV:]
        yield

    row = lax.broadcasted_iota(jnp.int32, (QW, GDN_DV), 0)
    for grp in groups:
        d, g, gcol = grp["d"], grp["g"], grp["gcol"]
        s4 = sfin_ref[0, d, g]
        sb = s4.astype(BF16)
        wb = grp["w"].astype(BF16)
        per_head = []
        for c in range(QUAD):
            lhs = wb[c * CHUNK:(c + 1) * CHUNK]
            if with_out:
                lhs = jnp.concatenate([lhs, grp["q"][c * CHUNK:(c + 1) * CHUNK]], axis=0)
            per_head.append(_dot(lhs, sb[:, c * GDN_DV:(c + 1) * GDN_DV]))

        def diag_blocks(base):
            return jnp.concatenate([r[base:base + CHUNK] for r in per_head], axis=0)

        v_new = grp["u"] - diag_blocks(0)
        end = 0 if d else CHUNK - 1
        g_end = [gcol[c * CHUNK + end:c * CHUNK + end + 1] for c in range(QUAD)]
        g_end_col = jnp.concatenate([jnp.broadcast_to(ge, (CHUNK, 1)) for ge in g_end], axis=0)
        vt = v_new * jnp.exp(g_end_col - gcol)
        vbd = jnp.concatenate(
            [jnp.where((row >= c * CHUNK) & (row < (c + 1) * CHUNK), vt, 0.0).astype(BF16) for c in range(QUAD)],
            axis=1)
        decay_lane = jnp.concatenate([jnp.broadcast_to(jnp.exp(ge), (1, GDN_DV)) for ge in g_end], axis=1)
        sfin_ref[0, d, g] = s4 * decay_lane + _dot_tn(grp["k"], vbd)
        if with_out:
            o = (jnp.exp(gcol) * scale) * diag_blocks(CHUNK) + _dot(grp["attn"], v_new.astype(BF16))
            for c, h in enumerate(grp["heads"]):
                grp["o_ref"][0, grp["rows"], h * GDN_DV:(h + 1) * GDN_DV] = o[c * CHUNK:(c + 1) * CHUNK].astype(BF16)
        yield


def _gdn_operands(qkv, gcol, grow, s0, with_out):
    b, t, _ = qkv.shape
    cs = CHUNKS_PER_STEP
    blk = cs * CHUNK
    nb = t // blk
    w = GDN_QK

    def fwd(*tail):
        return lambda bi, n: (bi, n) + tail

    def bwd(*tail):
        return lambda bi, n: (bi, nb - 1 - n) + tail

    s_spec = pl.BlockSpec((1, 2, NQUAD, GDN_DK, QUAD * GDN_DV), lambda bi, n: (bi, 0, 0, 0, 0))
    in_specs = (
        [pl.BlockSpec((1, blk, w), fwd(o)) for o in range(3)]
        + [pl.BlockSpec((1, blk, w), bwd(o)) for o in range(3)]
        + [pl.BlockSpec((1, blk, LANES), fwd(0)), pl.BlockSpec((1, blk, LANES), bwd(0)),
           pl.BlockSpec((1, cs, 4 * NQUAD, QW), fwd(0, 0)), pl.BlockSpec((1, cs, 4 * NQUAD, QW), bwd(0, 0)),
           s_spec])
    s_shape = jax.ShapeDtypeStruct(s0.shape, F32)
    if with_out:
        out_specs = [pl.BlockSpec((1, blk, w), fwd(0)), pl.BlockSpec((1, blk, w), bwd(0)), s_spec]
        o_shape = jax.ShapeDtypeStruct((b, t, GDN_V), BF16)
        out_shape = [o_shape, o_shape, s_shape]
    else:
        out_specs = [s_spec]
        out_shape = [s_shape]
    return [qkv] * 6 + [gcol, gcol, grow, grow, s0], in_specs, out_specs, out_shape


def _ml_body(refs, with_out):
    (qf, kf, vf, qb, kb, vb, gcf, gcb, grf, grb, c0_ref, n0_ref, m0_ref) = refs[:13]
    if with_out:
        of_ref, ob_ref, cfin_ref, nfin_ref, mfin_ref = refs[13:]
    else:
        cfin_ref, nfin_ref, mfin_ref = refs[13:]
        of_ref = ob_ref = None

    @pl.when(pl.program_id(1) == 0)
    def _():
        cfin_ref[...] = c0_ref[...]
        nfin_ref[...] = n0_ref[...]
        mfin_ref[...] = m0_ref[...]

    scale = ML_DK ** -0.5
    heads = tuple(range(ML_HEADS))

    def rows_of(c):
        return slice(c * CHUNK, (c + 1) * CHUNK)

    def spread(xs):
        return jnp.concatenate([jnp.broadcast_to(x, (CHUNK, 1)) for x in xs], axis=0)

    groups = []
    cs = grf.shape[1]
    for ci, d in [(ci, d) for ci in range(cs) for d in range(2)]:
        q_ref, k_ref, v_ref, gc_ref, gr_ref, o_ref = (
            (qb, kb, vb, gcb, grb, ob_ref) if d else (qf, kf, vf, gcf, grf, of_ref))
        ch = cs - 1 - ci if d else ci
        rows = slice(ch * CHUNK, (ch + 1) * CHUNK)

        def stack(ref, width):
            return jnp.concatenate([ref[0, rows, h * width:(h + 1) * width] for h in heads], axis=0)

        def col(base):
            lanes = [base + G_I + d * ML_HEADS + h for h in heads]
            return jnp.concatenate([gc_ref[0, rows, l:l + 1] for l in lanes], axis=0)

        grp = dict(d=d, rows=rows, o_ref=o_ref, k=stack(k_ref, ML_DK), v=stack(v_ref, ML_DV),
                   bcol=col(LANES), ucol=col(2 * LANES), dmcol=col(3 * LANES), urow=gr_ref[0, ch, d:d + 1, :])
        if with_out:
            grp["q"] = stack(q_ref, ML_DK)
            grp["qk"] = _dot_nt(grp["q"], grp["k"])
        groups.append(grp)
        yield

    row = lax.broadcasted_iota(jnp.int32, (QW, ML_DV), 0)
    for grp in groups:
        d, k4, v4, bcol, ucol, dmcol = grp["d"], grp["k"], grp["v"], grp["bcol"], grp["ucol"], grp["dmcol"]
        end = 0 if d else CHUNK - 1
        m = [mfin_ref[0, d, h, 0:1, 0:1] for h in heads]
        n = [nfin_ref[0, d, h, 0:1, :] for h in heads]
        tot = [bcol[c * CHUNK + end:c * CHUNK + end + 1] for c in heads]
        m_new = [jnp.maximum(tot[c] + m[c], dmcol[c * CHUNK + end:c * CHUNK + end + 1]) for c in heads]
        dec = [jnp.exp(tot[c] + m[c] - m_new[c]) for c in heads]
        wt = jnp.exp(spread(tot) + ucol - spread(m_new))
        vw = v4.astype(F32) * wt
        vbd = jnp.concatenate(
            [jnp.where((row >= c * CHUNK) & (row < (c + 1) * CHUNK), vw, 0.0).astype(BF16) for c in heads], axis=1)
        c4 = cfin_ref[0, d]
        dec_lane = jnp.concatenate([jnp.broadcast_to(dec[c], (1, ML_DV)) for c in heads], axis=1)
        cfin_ref[0, d] = c4 * dec_lane + _dot_tn(k4, vbd)
        kw = k4.astype(F32) * wt
        for c in heads:
            n_next = dec[c] * n[c] + jnp.sum(kw[rows_of(c)], axis=0, keepdims=True)
            nfin_ref[0, d, c] = jnp.broadcast_to(n_next, (8, ML_DK))
            mfin_ref[0, d, c] = jnp.broadcast_to(m_new[c], (8, LANES))
        yield
        if not with_out:
            continue
        q4 = grp["q"]
        _, incl, _ = _quad_masks(bool(d))
        m4 = spread(m)
        m_t = jnp.maximum(bcol + m4, dmcol)
        inter = jnp.exp(bcol + m4 - m_t)
        p = grp["qk"] * scale * jnp.exp(jnp.where(incl, bcol + grp["urow"] - m_t, NEG_BIG))
        c4b = c4.astype(BF16)
        qc = jnp.concatenate([_dot(q4[rows_of(c)], c4b[:, c * ML_DV:(c + 1) * ML_DV]) for c in heads], axis=0)
        num = (inter * scale) * qc + _dot(p.astype(BF16), v4)
        n_rows = jnp.concatenate([jnp.broadcast_to(n[c], (CHUNK, ML_DK)) for c in heads], axis=0)
        qn = jnp.sum(q4.astype(F32) * n_rows, axis=1, keepdims=True) * scale
        den = inter * qn + jnp.sum(p, axis=1, keepdims=True)
        out = num * (1.0 / jnp.maximum(jnp.abs(den), jnp.exp(-m_t)))
        for c in heads:
            grp["o_ref"][0, grp["rows"], c * ML_DV:(c + 1) * ML_DV] = out[rows_of(c)].astype(BF16)
        yield


def _ml_operands(z, col0, gates, urow, c0, n0, m0, with_out):
    b, t, _ = z.shape
    cs = CHUNKS_PER_STEP
    blk = cs * CHUNK
    nb = t // blk
    qoff = col0 * LANES // ML_QK
    koff = (col0 * LANES + ML_QK) // ML_QK
    voff = (col0 * LANES + 2 * ML_QK) // ML_V

    def fwd(*tail):
        return lambda bi, n: (bi, n) + tail

    def bwd(*tail):
        return lambda bi, n: (bi, nb - 1 - n) + tail

    def state_spec(shape):
        return pl.BlockSpec((1,) + shape, lambda bi, n: (bi,) + (0,) * len(shape))

    st = [(2, ML_DK, ML_HEADS * ML_DV), (2, ML_HEADS, 8, ML_DK), (2, ML_HEADS, 8, LANES)]
    st_specs = [state_spec(s) for s in st]
    st_shapes = [jax.ShapeDtypeStruct((b,) + s, F32) for s in st]
    in_specs = (
        [pl.BlockSpec((1, blk, ML_QK), fwd(qoff)), pl.BlockSpec((1, blk, ML_QK), fwd(koff)),
         pl.BlockSpec((1, blk, ML_V), fwd(voff)),
         pl.BlockSpec((1, blk, ML_QK), bwd(qoff)), pl.BlockSpec((1, blk, ML_QK), bwd(koff)),
         pl.BlockSpec((1, blk, ML_V), bwd(voff)),
         pl.BlockSpec((1, blk, 4 * LANES), fwd(0)), pl.BlockSpec((1, blk, 4 * LANES), bwd(0)),
         pl.BlockSpec((1, cs, 8, QW), fwd(0, 0)), pl.BlockSpec((1, cs, 8, QW), bwd(0, 0))]
        + st_specs)
    if with_out:
        out_specs = [pl.BlockSpec((1, blk, ML_V), fwd(0)), pl.BlockSpec((1, blk, ML_V), bwd(0))] + st_specs
        o_shape = jax.ShapeDtypeStruct((b, t, ML_V), BF16)
        out_shape = [o_shape, o_shape] + st_shapes
    else:
        out_specs = st_specs
        out_shape = st_shapes
    return [z] * 6 + [gates, gates, urow, urow, c0, n0, m0], in_specs, out_specs, out_shape


GDN_PIECES_PER_GROUP = 9
ML_PIECES_PER_GROUP = 3


def _scan_kernel(*refs, n_gdn_in, n_ml_in, n_gdn_out, with_out):
    n_in = n_gdn_in + n_ml_in
    gdn_refs = refs[:n_gdn_in] + refs[n_in:n_in + n_gdn_out]
    ml_refs = refs[n_gdn_in:n_in] + refs[n_in + n_gdn_out:]
    gdn = _gdn_body(gdn_refs, with_out)
    ml = _ml_body(ml_refs, with_out)
    n_gdn = GDN_PIECES_PER_GROUP * 2 * NQUAD * CHUNKS_PER_STEP
    n_ml = (ML_PIECES_PER_GROUP if with_out else ML_PIECES_PER_GROUP - 1) * 2 * CHUNKS_PER_STEP
    done = 0
    for k, _ in enumerate(gdn, 1):
        while done < min(k * n_ml // n_gdn, n_ml):
            next(ml, None)
            done += 1
    for _ in ml:
        pass


def _scan_call(gdn_args, ml_args, with_out):
    g_ops, g_in, g_out, g_shape = _gdn_operands(*gdn_args, with_out)
    m_ops, m_in, m_out, m_shape = _ml_operands(*ml_args, with_out)
    b, t, _ = g_ops[0].shape
    res = pl.pallas_call(
        functools.partial(_scan_kernel, n_gdn_in=len(g_ops), n_ml_in=len(m_ops), n_gdn_out=len(g_out),
                          with_out=with_out),
        grid=(b, t // (CHUNKS_PER_STEP * CHUNK)),
        in_specs=g_in + m_in,
        out_specs=g_out + m_out,
        out_shape=g_shape + m_shape,
        compiler_params=pltpu.CompilerParams(
            dimension_semantics=("parallel", "arbitrary"), vmem_limit_bytes=SCAN_VMEM_LIMIT),
        name="scan_out" if with_out else "scan_state",
    )(*g_ops, *m_ops)
    return res[:len(g_out)], res[len(g_out):]


def _head_rms(x, width):
    outs = []
    for h in range(x.shape[1] // width):
        xh = x[:, h * width:(h + 1) * width]
        ms = jnp.sum(xh * xh, axis=-1, keepdims=True) * (1.0 / width)
        outs.append(xh * lax.rsqrt(ms + EPS))
    return jnp.concatenate(outs, axis=-1)


def _merge_kernel(x_ref, of_ref, ob_ref, hf_ref, hb_ref, gz_ref, mo_ref, gg_ref, gm_ref,
                  gnw_ref, mnw_ref, wg_ref, wm_ref, wo_ref, gate_ref, o_ref):
    f32 = lambda ref: ref[0].astype(F32)
    og = _head_rms(f32(of_ref) + f32(ob_ref), GDN_DV) * gnw_ref[...]
    gz = f32(gz_ref)
    og = og * (gz * _sigmoid(gz))
    y_gdn = _dot(og.astype(BF16), wg_ref[...])
    hm = _head_rms(f32(hf_ref) + f32(hb_ref), ML_DV) * mnw_ref[...]
    hm = hm * _sigmoid(f32(mo_ref))
    y_ml = _dot(hm.astype(BF16), wm_ref[...])
    merged = _sigmoid(f32(gg_ref)) * y_gdn + _sigmoid(f32(gm_ref)) * y_ml
    x_mix = _dot(merged.astype(BF16), wo_ref[...])
    o_ref[0] = x_ref[0] + gate_ref[0] * x_mix


def _merge_call(x, o_f, o_b, h_f, h_b, z, zo_col0, gnw, mnw, wg, wm, wo, gate):
    b, t, d = x.shape
    tm = _pick_tile(t, 512)
    nz = zo_col0 * LANES // d

    def tok(bi, i):
        return (bi, i, 0)

    def zcol(k):
        return lambda bi, i: (bi, i, nz + k)

    tile = lambda imap: pl.BlockSpec((1, tm, d), imap)
    full = lambda shape: pl.BlockSpec(shape, lambda bi, i: (0,) * len(shape))
    return pl.pallas_call(
        _merge_kernel,
        grid=(b, t // tm),
        in_specs=[tile(tok), tile(tok), tile(tok), tile(tok), tile(tok),
                  tile(zcol(0)), tile(zcol(1)), tile(zcol(2)), tile(zcol(3)),
                  full((1, d)), full((1, d)), full((d, d)), full((d, d)), full((d, d)),
                  pl.BlockSpec((1, 1, d), lambda bi, i: (bi, 0, 0))],
        out_specs=tile(tok),
        out_shape=jax.ShapeDtypeStruct((b, t, d), F32),
        compiler_params=_cparams("parallel", "parallel"),
        name="merge",
    )(x, o_f, o_b, h_f, h_b, z, z, z, z, gnw, mnw, wg, wm, wo, gate)


def _ffn_kernel(x_ref, xp_ref, xn_ref, nw2_ref, sh_ref, sc_ref, wup_ref, cwg_ref, cwv_ref, wd_ref,
                gate_ref, nwo_ref, o_ref, hn_ref, ua_ref, ub_ref, acta_ref, actb_ref, acc_ref):
    i = pl.program_id(1)
    j = pl.program_id(2)
    nj = wd_ref.shape[0]
    first = i == 0
    last = i == pl.num_programs(1) - 1
    tm = x_ref.shape[1]
    n_img_rows = tm // GRID_W
    col = lax.broadcasted_iota(jnp.int32, (GRID_W, LANES), 0)

    def up_project(u_ref, jt):
        hn = hn_ref[...]
        u_ref[0] = _dot(hn, wup_ref[jt])
        u_ref[1] = _dot(hn, wup_ref[nj + jt])

    @pl.when(j == 0)
    def _():
        def norm_mod(x):
            ms = jnp.mean(x * x, axis=-1, keepdims=True)
            y = x * lax.rsqrt(ms + EPS) * nw2_ref[...]
            return (y * (1.0 + sc_ref[0]) + sh_ref[0]).astype(BF16)

        hn_ref[0:GRID_W] = norm_mod(xp_ref[0])
        hn_ref[GRID_W:GRID_W + tm] = norm_mod(x_ref[0])
        hn_ref[GRID_W + tm:] = norm_mod(xn_ref[0])
        up_project(ua_ref, 0)
        actb_ref[...] = jnp.zeros(actb_ref.shape, BF16)
        acc_ref[...] = jnp.zeros(acc_ref.shape, F32)

    def row_conv(u_ref, cw_ref, cols):
        w = [cw_ref[k:k + 1, cols] for k in range(9)]
        cache = {}

        def strip(r):
            if r not in cache:
                s = u_ref[(r + 1) * GRID_W:(r + 2) * GRID_W, cols]
                if r < 0:
                    s = jnp.where(first, 0.0, s)
                elif r == n_img_rows:
                    s = jnp.where(last, 0.0, s)
                cache[r] = s
            return cache[r]

        def conv(r):
            taps = [strip(r - 1), strip(r), strip(r + 1)]
            side = [taps[0] * w[dc] + taps[1] * w[3 + dc] + taps[2] * w[6 + dc] for dc in range(3)]
            left = jnp.where(col == 0, 0.0, pltpu.roll(side[0], 1, 0))
            right = jnp.where(col == GRID_W - 1, 0.0, pltpu.roll(side[2], GRID_W - 1, 0))
            return side[1] + left + right

        return conv

    def step(u_cur, act_cur, u_next, act_prev):
        jn = jnp.minimum(j + 1, nj - 1)
        jp = jnp.maximum(j - 1, 0)
        ext = tm + 2 * GRID_W
        half_e, half_t = ext // FFN_ROW_SPLIT, tm // FFN_ROW_SPLIT

        def up_piece(which, lo):
            rows = slice(lo, lo + half_e)
            return lambda: u_next.__setitem__((which, rows), _dot(hn_ref[rows], wup_ref[which * nj + jn]))

        def down_piece(lo):
            rows = slice(lo, lo + half_t)

            def run():
                acc_ref[rows] += _dot(act_prev[rows], wd_ref[jp])
            return run

        def conv_piece(cb, conv_g, conv_v, r):
            cols = slice(cb * LANES, (cb + 1) * LANES)

            def run():
                g = conv_g(r)
                act_cur[r * GRID_W:(r + 1) * GRID_W, cols] = (g * _sigmoid(g) * conv_v(r)).astype(BF16)
            return run

        mxu_work = ([up_piece(which, k * half_e) for which in range(2) for k in range(FFN_ROW_SPLIT)]
                    + [down_piece(k * half_t) for k in range(FFN_ROW_SPLIT)])
        vpu_work = []
        for cb in range(act_cur.shape[1] // LANES):
            cols = slice(cb * LANES, (cb + 1) * LANES)
            conv_g = row_conv(u_cur.at[0], cwg_ref, cols)
            conv_v = row_conv(u_cur.at[1], cwv_ref, cols)
            vpu_work += [conv_piece(cb, conv_g, conv_v, r) for r in range(n_img_rows)]
        per = -(-len(vpu_work) // len(mxu_work))
        for k, piece in enumerate(mxu_work):
            piece()
            for run in vpu_work[k * per:(k + 1) * per]:
                run()

    @pl.when((j & 1) == 0)
    def _():
        step(ua_ref, acta_ref, ub_ref, actb_ref)

    @pl.when((j & 1) == 1)
    def _():
        step(ub_ref, actb_ref, ua_ref, acta_ref)

    @pl.when(j == nj - 1)
    def _():
        act_last = acta_ref if (nj - 1) % 2 == 0 else actb_ref
        acc_ref[...] += _dot(act_last[...], wd_ref[nj - 1])
        x = x_ref[0] + gate_ref[0] * acc_ref[...]
        ms = jnp.mean(x * x, axis=-1, keepdims=True)
        o_ref[0] = x * lax.rsqrt(ms + EPS) * nwo_ref[...]


def _ffn_call(x, nw2, shift, scale, w_up, conv_w, w_down, gate, nwo):
    b, t, d = x.shape
    f = w_down.shape[0]
    tm = _pick_tile(t, 1024)
    tc = 256
    nj = f // tc
    rpt = tm // GRID_W
    nrows = t // GRID_W
    wup_tiles = w_up.reshape(d, 2 * nj, tc).transpose(1, 0, 2)
    wd_tiles = w_down.reshape(nj, tc, d)
    const = lambda *shape: pl.BlockSpec(shape, lambda bi, i, j: (0,) * len(shape))
    resident = lambda *shape: pl.BlockSpec(shape, lambda bi, i, j: (0,) * len(shape), pipeline_mode=pl.Buffered(1))
    per_batch = pl.BlockSpec((1, 1, d), lambda bi, i, j: (bi, 0, 0))
    return pl.pallas_call(
        _ffn_kernel,
        grid=(b, t // tm, nj),
        in_specs=[pl.BlockSpec((1, tm, d), lambda bi, i, j: (bi, i, 0)),
                  pl.BlockSpec((1, GRID_W, d), lambda bi, i, j: (bi, jnp.maximum(i * rpt - 1, 0), 0)),
                  pl.BlockSpec((1, GRID_W, d), lambda bi, i, j: (bi, jnp.minimum((i + 1) * rpt, nrows - 1), 0)),
                  const(1, d), per_batch, per_batch,
                  resident(2 * nj, d, tc),
                  pl.BlockSpec((9, tc), lambda bi, i, j: (0, j)),
                  pl.BlockSpec((9, tc), lambda bi, i, j: (0, nj + j)),
                  resident(nj, tc, d),
                  per_batch, const(1, d)],
        out_specs=pl.BlockSpec((1, tm, d), lambda bi, i, j: (bi, i, 0)),
        out_shape=jax.ShapeDtypeStruct((b, t, d), F32),
        scratch_shapes=[pltpu.VMEM((tm + 2 * GRID_W, d), BF16),
                        pltpu.VMEM((2, tm + 2 * GRID_W, tc), F32), pltpu.VMEM((2, tm + 2 * GRID_W, tc), F32),
                        pltpu.VMEM((tm, tc), BF16), pltpu.VMEM((tm, tc), BF16), pltpu.VMEM((tm, d), F32)],
        compiler_params=pltpu.CompilerParams(
            dimension_semantics=("parallel", "parallel", "arbitrary"), vmem_limit_bytes=FFN_VMEM_LIMIT),
        name="ffn",
    )(x, x, x, nw2, shift, scale, wup_tiles, conv_w, conv_w, wd_tiles, gate, nwo)


def _gdn_gate_rows(gates):
    b, t, _ = gates.shape
    nc = t // CHUNK

    def rows(base):
        q = gates[:, :, base:base + 2 * GDN_HEADS].reshape(b, nc, CHUNK, 2, NQUAD, QUAD)
        return q.transpose(0, 1, 3, 4, 5, 2).reshape(b, nc, 2 * NQUAD, QW)

    return jnp.concatenate([rows(G_A), rows(G_B)], axis=2)


def _ml_gate_rows(gates):
    b, t, _ = gates.shape
    nc = t // CHUNK
    base = 2 * LANES + G_I
    u = gates[:, :, base:base + 2 * ML_HEADS].reshape(b, nc, CHUNK, 2, ML_HEADS)
    u = u.transpose(0, 1, 3, 4, 2).reshape(b, nc, 2, QW)
    return jnp.pad(u, ((0, 0), (0, 0), (0, 6), (0, 0)))


def _mixer_states(x_seq, nw, shift, scale, w_state, w_aux, conv_w, alog_row, bias_row, states,
                  with_out, w_full=None):
    w = w_full if with_out else w_state
    z, z_aux = _nmm_call(x_seq, nw, shift, scale, w, w_aux, name="in_proj")
    gates = _gates_call(z_aux, alog_row, bias_row)
    qkv = _conv_call(z, conv_w)
    s_gdn, c_ml, n_ml, m_ml = states
    ml_col0 = (2 * GDN_QK + GDN_V) // LANES
    gdn_res, ml_res = _scan_call(
        (qkv, gates[:, :, :LANES], _gdn_gate_rows(gates), s_gdn),
        (z, ml_col0, gates, _ml_gate_rows(gates), c_ml, n_ml, m_ml), with_out)
    return z, gdn_res, ml_res


def kernel(x, c, ctx, c_ctx, w_ada, b_ada, norm1_w, w_in, gdn_conv, gdn_a_log, gdn_dt_bias, gdn_norm_w,
           ml_igate_b, ml_fgate_b, ml_norm_w, w_branch_gdn, w_branch_ml, w_out, norm2_w, w_up, ffn_conv,
           w_down, norm_out_w):
    bsz, _, d = x.shape
    depth = w_ada.shape[0]
    assert depth == 1, "single-layer problem: the context stream is never updated"
    l = 0

    sizes = (2 * GDN_QK + GDN_V, 2 * GDN_HEADS, 2 * GDN_HEADS, ML_QK, ML_QK, ML_V, 2 * ML_HEADS, 2 * ML_HEADS,
             GDN_V, ML_V, d, d)
    offs = [0]
    for s in sizes:
        offs.append(offs[-1] + s)
    wi = w_in[l]
    seg = lambda k: wi[:, offs[k]:offs[k + 1]]
    w_state = jnp.concatenate([seg(0), seg(3), seg(4), seg(5)], axis=1).astype(BF16)
    w_full = jnp.concatenate([seg(0), seg(3), seg(4), seg(5), seg(8), seg(9), seg(10), seg(11)], axis=1).astype(BF16)
    n_gate = 4 * GDN_HEADS + 4 * ML_HEADS
    w_aux = jnp.concatenate([seg(1), seg(2), seg(6), seg(7), jnp.zeros((d, LANES - n_gate), F32)], axis=1).astype(BF16)
    pad = lambda v, n: jnp.pad(v.reshape(1, -1).astype(F32), ((0, 0), (0, n - v.size)))
    alog_row = pad(gdn_a_log[l], LANES)
    bias_row = pad(jnp.concatenate([gdn_dt_bias[l].reshape(-1), jnp.zeros((2 * GDN_HEADS,), F32),
                                    ml_igate_b[l].reshape(-1), ml_fgate_b[l].reshape(-1)]), LANES)
    row = lambda v: v.reshape(1, -1).astype(F32)

    c_all = jnp.concatenate([c, c_ctx[None], jnp.zeros((8 - bsz - 1, d), F32)], axis=0)
    mods = _mod_call(c_all, w_ada[l].astype(BF16), row(b_ada[l]))
    mod_x = mods[:bsz].reshape(bsz, N_MOD, 1, d)
    mod_c = jnp.broadcast_to(mods[bsz].reshape(1, N_MOD, 1, d), (bsz, N_MOD, 1, d))

    zero_states = (jnp.zeros((bsz, 2, NQUAD, GDN_DK, QUAD * GDN_DV), F32),
                   jnp.zeros((bsz, 2, ML_DK, ML_HEADS * ML_DV), F32),
                   jnp.zeros((bsz, 2, ML_HEADS, 8, ML_DK), F32),
                   jnp.zeros((bsz, 2, ML_HEADS, 8, LANES), F32))
    common = (w_state, w_aux, gdn_conv[l].astype(F32), alog_row, bias_row)

    _, (s_gdn,), (c_ml, n_ml, m_ml) = _mixer_states(
        ctx, row(norm1_w[l]), mod_c[:, 0], mod_c[:, 1], *common, zero_states, False)

    z, (o_f, o_b, _), (h_f, h_b, _, _, _) = _mixer_states(
        x, row(norm1_w[l]), mod_x[:, 0], mod_x[:, 1], *common, (s_gdn, c_ml, n_ml, m_ml), True, w_full)
    x1 = _merge_call(x, o_f, o_b, h_f, h_b, z, STATE_COLS // LANES,
                     row(jnp.tile(gdn_norm_w[l], GDN_HEADS)), row(ml_norm_w[l]),
                     w_branch_gdn[l].astype(BF16), w_branch_ml[l].astype(BF16), w_out[l].astype(BF16),
                     mod_x[:, 2])
    return _ffn_call(x1, row(norm2_w[l]), mod_x[:, 3], mod_x[:, 4], w_up[l].astype(BF16),
                     ffn_conv[l].reshape(9, -1).astype(F32), w_down[l].astype(BF16), mod_x[:, 5], row(norm_out_w))
```

```python
import functools

import jax
import jax.numpy as jnp
from jax import lax
from jax.experimental import pallas as pl
from jax.experimental.pallas import tpu as pltpu

F32 = jnp.float32
BF16 = jnp.bfloat16

GDN_HEADS = 8
GDN_DK = 128
GDN_DV = 128
ML_HEADS = 4
ML_DK = 128
ML_DV = 256
CHUNK = 64
GATE_CAP = 15.0
GRID_W = 64
N_MOD = 6
EPS = 1e-6
LANES = 128
HALO = 16
NEG_BIG = -1e30

GDN_QK = GDN_HEADS * GDN_DK
GDN_V = GDN_HEADS * GDN_DV
ML_QK = ML_HEADS * ML_DK
ML_V = ML_HEADS * ML_DV
STATE_COLS = 2 * GDN_QK + GDN_V + 2 * ML_QK + ML_V
OUT_COLS = GDN_V + ML_V + 2 * 1024

VMEM_LIMIT = 48 * 1024 * 1024
FFN_VMEM_LIMIT = 56 * 1024 * 1024
FFN_ROW_SPLIT = 4
SCAN_VMEM_LIMIT = 58 * 1024 * 1024


def _cparams(*sem):
    return pltpu.CompilerParams(dimension_semantics=sem, vmem_limit_bytes=VMEM_LIMIT)


def _dot(a, b):
    return jnp.dot(a, b, preferred_element_type=F32)


def _dot_nt(a, b):
    return lax.dot_general(a, b, (((1,), (1,)), ((), ())), preferred_element_type=F32)


def _dot_tn(a, b):
    return lax.dot_general(a, b, (((0,), (0,)), ((), ())), preferred_element_type=F32)


def _sigmoid(x):
    return 0.5 * jnp.tanh(0.5 * x) + 0.5


def _softplus(x):
    return jnp.maximum(x, 0.0) + jnp.log1p(jnp.exp(-jnp.abs(x)))


def _pick_tile(n, pref):
    t = min(n, pref)
    while n % t:
        t //= 2
    return t


def _pick_cols(c, cap):
    t = cap - cap % LANES
    while c % t:
        t -= LANES
    return t


def _mod_kernel(c_ref, w_ref, b_ref, o_ref):
    c = c_ref[...]
    s = c * _sigmoid(c)
    o_ref[...] = _dot(s.astype(BF16), w_ref[...]) + b_ref[...]


def _mod_call(c_all, w_ada, b_ada):
    rows, d = c_all.shape
    n = w_ada.shape[1]
    tn = 1024
    return pl.pallas_call(
        _mod_kernel,
        grid=(n // tn,),
        in_specs=[
            pl.BlockSpec((rows, d), lambda j: (0, 0)),
            pl.BlockSpec((d, tn), lambda j: (0, j)),
            pl.BlockSpec((1, tn), lambda j: (0, j)),
        ],
        out_specs=pl.BlockSpec((rows, tn), lambda j: (0, j)),
        out_shape=jax.ShapeDtypeStruct((rows, n), F32),
        compiler_params=_cparams("parallel"),
        name="mod",
    )(c_all, w_ada, b_ada)


def _nmm_kernel(x_ref, nw_ref, sh_ref, sc_ref, w_ref, *rest, has_aux):
    if has_aux:
        wa_ref, o_ref, oa_ref, hn_ref = rest
    else:
        o_ref, hn_ref = rest

    @pl.when(pl.program_id(2) == 0)
    def _():
        x = x_ref[0]
        ms = jnp.mean(x * x, axis=-1, keepdims=True)
        y = x * lax.rsqrt(ms + EPS) * nw_ref[...]
        hb = (y * (1.0 + sc_ref[0]) + sh_ref[0]).astype(BF16)
        hn_ref[...] = hb
        if has_aux:
            oa_ref[0] = _dot(hb, wa_ref[...])

    o_ref[0] = _dot(hn_ref[...], w_ref[pl.program_id(2)]).astype(o_ref.dtype)


def _nmm_call(x, nw, shift, scale, w, w_aux=None, out_dtype=BF16, name="nmm"):
    b, t, d = x.shape
    c = w.shape[1]
    tm = _pick_tile(t, 1024)
    tn = _pick_cols(c, 1536)
    nj = c // tn
    has_aux = w_aux is not None
    w_tiles = w.reshape(d, nj, tn).transpose(1, 0, 2)
    in_specs = [
        pl.BlockSpec((1, tm, d), lambda bi, i, j: (bi, i, 0)),
        pl.BlockSpec((1, d), lambda bi, i, j: (0, 0)),
        pl.BlockSpec((1, 1, d), lambda bi, i, j: (bi, 0, 0)),
        pl.BlockSpec((1, 1, d), lambda bi, i, j: (bi, 0, 0)),
        pl.BlockSpec((nj, d, tn), lambda bi, i, j: (0, 0, 0), pipeline_mode=pl.Buffered(1)),
    ]
    out_specs = [pl.BlockSpec((1, tm, tn), lambda bi, i, j: (bi, i, j))]
    out_shape = [jax.ShapeDtypeStruct((b, t, c), out_dtype)]
    args = [x, nw, shift, scale, w_tiles]
    if has_aux:
        in_specs.append(pl.BlockSpec((d, LANES), lambda bi, i, j: (0, 0)))
        out_specs.append(pl.BlockSpec((1, tm, LANES), lambda bi, i, j: (bi, i, 0)))
        out_shape.append(jax.ShapeDtypeStruct((b, t, LANES), F32))
        args.append(w_aux)
    res = pl.pallas_call(
        functools.partial(_nmm_kernel, has_aux=has_aux),
        grid=(b, t // tm, c // tn),
        in_specs=in_specs,
        out_specs=out_specs,
        out_shape=out_shape,
        scratch_shapes=[pltpu.VMEM((tm, d), BF16)],
        compiler_params=_cparams("parallel", "parallel", "arbitrary"),
        name=name,
    )(*args)
    return res if has_aux else res[0]


G_A, G_B, G_I, G_F = 0, 16, 32, 40


def _chunk_scan(x, pos, backward, op, ident):
    rows = x.shape[0]
    yf, yb = x, x
    s = 1
    while s < CHUNK:
        yf = op(yf, jnp.where(pos >= s, pltpu.roll(yf, s, 0), ident))
        yb = op(yb, jnp.where(pos + s < CHUNK, pltpu.roll(yb, rows - s, 0), ident))
        s *= 2
    return jnp.where(backward, yb, yf)


def _gates_kernel(z_ref, alog_ref, bias_ref, o_ref):
    z = z_ref[0] + bias_ref[...]
    lane = lax.broadcasted_iota(jnp.int32, z.shape, 1)
    pos = lax.broadcasted_iota(jnp.int32, z.shape, 0) & (CHUNK - 1)
    backward = ((lane < G_I) & ((lane & 15) >= 8)) | ((lane >= G_I) & ((lane & 7) >= 4))

    log_decay = -jnp.exp(alog_ref[...]) * _softplus(z)
    beta = _sigmoid(z)
    capped = GATE_CAP * jnp.tanh(z * (1.0 / GATE_CAP))
    log_forget = -_softplus(-capped)

    summand = jnp.where(lane < G_B, log_decay, jnp.where(lane >= G_F, log_forget, 0.0))
    csum = _chunk_scan(summand, pos, backward, jnp.add, 0.0)
    o_ref[0, :, 0:LANES] = jnp.where(lane < G_B, csum, beta)

    bc = pltpu.roll(csum, LANES - (G_F - G_I), 1)
    backward_i = (lane & 7) >= 4
    u = capped - bc
    umax = _chunk_scan(u, pos, backward_i, jnp.maximum, NEG_BIG)
    o_ref[0, :, LANES:2 * LANES] = bc
    o_ref[0, :, 2 * LANES:3 * LANES] = u
    o_ref[0, :, 3 * LANES:4 * LANES] = bc + umax


def _gates_call(z_aux, alog_row, bias_row):
    b, t, _ = z_aux.shape
    tm = _pick_tile(t, 512)
    return pl.pallas_call(
        _gates_kernel,
        grid=(b, t // tm),
        in_specs=[
            pl.BlockSpec((1, tm, LANES), lambda bi, i: (bi, i, 0)),
            pl.BlockSpec((1, LANES), lambda bi, i: (0, 0)),
            pl.BlockSpec((1, LANES), lambda bi, i: (0, 0)),
        ],
        out_specs=pl.BlockSpec((1, tm, 4 * LANES), lambda bi, i: (bi, i, 0)),
        out_shape=jax.ShapeDtypeStruct((b, t, 4 * LANES), F32),
        compiler_params=_cparams("parallel", "parallel"),
        name="gates",
    )(z_aux, alog_row, bias_row)


def _conv_kernel(z_ref, zp_ref, zn_ref, w_ref, o_ref):
    i = pl.program_id(1)
    j = pl.program_id(2)
    z = z_ref[0].astype(F32)
    rows = z.shape[0]
    row8 = lax.broadcasted_iota(jnp.int32, (8, z.shape[1]), 0)
    prev_row = jnp.where(i == 0, 0.0, zp_ref[0, HALO - 1:HALO, :].astype(F32))
    next_row = jnp.where(i == pl.num_programs(1) - 1, 0.0, zn_ref[0, 0:1, :].astype(F32))
    z_prev = pltpu.roll(z, 1, 0)
    z_prev = jnp.concatenate([jnp.where(row8 == 0, prev_row, z_prev[:8]), z_prev[8:]], axis=0)
    z_next = pltpu.roll(z, rows - 1, 0)
    z_next = jnp.concatenate([z_next[:rows - 8], jnp.where(row8 == 7, next_row, z_next[rows - 8:])], axis=0)
    y = z_prev * w_ref[0:1, :] + z * w_ref[1:2, :] + z_next * w_ref[2:3, :]
    y = y * _sigmoid(y)

    is_qk = j < 2
    for h in range(GDN_HEADS):
        cols = slice(h * GDN_DK, (h + 1) * GDN_DK)
        yh = y[:, cols]
        inv = lax.rsqrt(jnp.sum(yh * yh, axis=-1, keepdims=True) + EPS)
        o_ref[0, :, cols] = (yh * jnp.where(is_qk, inv, 1.0)).astype(o_ref.dtype)


def _conv_call(z, conv_w):
    b, t, _ = z.shape
    tt = _pick_tile(t, 512)
    g = GDN_QK
    nbh = tt // HALO
    return pl.pallas_call(
        _conv_kernel,
        grid=(b, t // tt, 3),
        in_specs=[
            pl.BlockSpec((1, tt, g), lambda bi, i, j: (bi, i, j)),
            pl.BlockSpec((1, HALO, g), lambda bi, i, j: (bi, jnp.maximum(i * nbh - 1, 0), j)),
            pl.BlockSpec((1, HALO, g), lambda bi, i, j: (bi, jnp.minimum((i + 1) * nbh, t // HALO - 1), j)),
            pl.BlockSpec((3, g), lambda bi, i, j: (0, j)),
        ],
        out_specs=pl.BlockSpec((1, tt, g), lambda bi, i, j: (bi, i, j)),
        out_shape=jax.ShapeDtypeStruct((b, t, 3 * g), BF16),
        compiler_params=_cparams("parallel", "parallel", "parallel"),
        name="gdn_conv",
    )(z, z, z, conv_w)


QUAD = 4
QW = QUAD * CHUNK
NQUAD = GDN_HEADS // QUAD
INV_BASE = 8
CHUNKS_PER_STEP = 4


def _quad_masks(reverse):
    ii = lax.broadcasted_iota(jnp.int32, (QW, QW), 0)
    jj = lax.broadcasted_iota(jnp.int32, (QW, QW), 1)
    same = (ii ^ jj) < CHUNK
    if reverse:
        return same & (jj > ii), same & (jj >= ii), ii == jj
    return same & (jj < ii), same & (jj <= ii), ii == jj


def _gdn_body(refs, with_out):
    (qf, kf, vf, qb, kb, vb, gcf, gcb, grf, grb, s0_ref) = refs[:11]
    if with_out:
        of_ref, ob_ref, sfin_ref = refs[11:]
    else:
        (sfin_ref,) = refs[11:]
        of_ref = ob_ref = None

    @pl.when(pl.program_id(1) == 0)
    def _():
        sfin_ref[...] = s0_ref[...]

    scale = GDN_DK ** -0.5
    groups = []
    cs = grf.shape[1]
    for ci, d, g in [(ci, d, g) for ci in range(cs) for d in range(2) for g in range(NQUAD)]:
        q_ref, k_ref, v_ref, gc_ref, gr_ref, o_ref = (
            (qb, kb, vb, gcb, grb, ob_ref) if d else (qf, kf, vf, gcf, grf, of_ref))
        ch = cs - 1 - ci if d else ci
        rows = slice(ch * CHUNK, (ch + 1) * CHUNK)
        heads = tuple(range(g * QUAD, (g + 1) * QUAD))

        def stack(ref):
            return jnp.concatenate([ref[0, rows, h * GDN_DK:(h + 1) * GDN_DK] for h in heads], axis=0)

        def col(base):
            lanes = [base + d * GDN_HEADS + h for h in heads]
            return jnp.concatenate([gc_ref[0, rows, l:l + 1] for l in lanes], axis=0)

        r = d * NQUAD + g
        groups.append(dict(
            d=d, g=g, heads=heads, rows=rows, o_ref=o_ref, k=stack(k_ref), v=stack(v_ref),
            q=stack(q_ref) if with_out else None, gcol=col(G_A), bcol=col(G_B),
            grow=gr_ref[0, ch, r:r + 1, :]))

    for grp in groups:
        strict, incl, _ = _quad_masks(bool(grp["d"]))
        decay = jnp.exp(jnp.where(incl, grp["gcol"] - grp["grow"], NEG_BIG))
        kk = _dot_nt(grp["k"], grp["k"])
        x = jnp.where(strict, kk * decay * (-grp["bcol"]), 0.0)
        grp["x"] = x
        if with_out:
            grp["attn"] = (_dot_nt(grp["q"], grp["k"]) * decay * scale).astype(BF16)
        yield
    ii = lax.broadcasted_iota(jnp.int32, (QW, QW), 0)
    jj = lax.broadcasted_iota(jnp.int32, (QW, QW), 1)
    blk_xor = ii ^ jj
    for grp in groups:
        xb = jnp.where(blk_xor < INV_BASE, grp["x"], 0.0)
        grp["s"] = jnp.where(ii == jj, 1.0, xb)
        xbb = xb.astype(BF16)
        grp["xm"] = _dot(xbb, xbb)
        yield
    for grp in groups:
        xb = grp["xm"].astype(BF16)
        both = _dot(jnp.concatenate([grp["s"].astype(BF16), xb], axis=0), xb)
        grp["s"] = grp["s"] + both[:QW]
        grp["xm"] = both[QW:]
        yield
    for grp in groups:
        grp["s"] = grp["s"] + _dot(grp["s"].astype(BF16), grp["xm"].astype(BF16))
        yield
    b = INV_BASE
    while b < CHUNK:
        for grp in groups:
            xoff = jnp.where((blk_xor < 2 * b) & (blk_xor >= b), grp["x"], 0.0)
            tb = grp["s"].astype(BF16)
            y = _dot(xoff.astype(BF16), tb)
            grp["s"] = grp["s"] + _dot(tb, y.astype(BF16))
            yield
        b *= 2
    for grp in groups:
        t = grp["s"]
        bcol, gcol = grp["bcol"], grp["gcol"]
        rhs = jnp.concatenate([(grp["v"].astype(F32) * bcol).astype(BF16),
                               (grp["k"].astype(F32) * (bcol * jnp.exp(gcol))).astype(BF16)], axis=1)
        uw = _dot(t.astype(BF16), rhs)
        grp["u"] = uw[:, :GDN_DV]
        grp["w"] = uw[:, GDN_DV:]
        yield

    row = lax.broadcasted_iota(jnp.int32, (QW, GDN_DV), 0)
    for grp in groups:
        d, g, gcol = grp["d"], grp["g"], grp["gcol"]
        s4 = sfin_ref[0, d, g]
        sb = s4.astype(BF16)
        wb = grp["w"].astype(BF16)
        per_head = []
        for c in range(QUAD):
            lhs = wb[c * CHUNK:(c + 1) * CHUNK]
            if with_out:
                lhs = jnp.concatenate([lhs, grp["q"][c * CHUNK:(c + 1) * CHUNK]], axis=0)
            per_head.append(_dot(lhs, sb[:, c * GDN_DV:(c + 1) * GDN_DV]))

        def diag_blocks(base):
            return jnp.concatenate([r[base:base + CHUNK] for r in per_head], axis=0)

        v_new = grp["u"] - diag_blocks(0)
        end = 0 if d else CHUNK - 1
        g_end = [gcol[c * CHUNK + end:c * CHUNK + end + 1] for c in range(QUAD)]
        g_end_col = jnp.concatenate([jnp.broadcast_to(ge, (CHUNK, 1)) for ge in g_end], axis=0)
        vt = v_new * jnp.exp(g_end_col - gcol)
        vbd = jnp.concatenate(
            [jnp.where((row >= c * CHUNK) & (row < (c + 1) * CHUNK), vt, 0.0).astype(BF16) for c in range(QUAD)],
            axis=1)
        decay_lane = jnp.concatenate([jnp.broadcast_to(jnp.exp(ge), (1, GDN_DV)) for ge in g_end], axis=1)
        sfin_ref[0, d, g] = s4 * decay_lane + _dot_tn(grp["k"], vbd)
        if with_out:
            o = (jnp.exp(gcol) * scale) * diag_blocks(CHUNK) + _dot(grp["attn"], v_new.astype(BF16))
            for c, h in enumerate(grp["heads"]):
                grp["o_ref"][0, grp["rows"], h * GDN_DV:(h + 1) * GDN_DV] = o[c * CHUNK:(c + 1) * CHUNK].astype(BF16)
        yield


def _gdn_operands(qkv, gcol, grow, s0, with_out):
    b, t, _ = qkv.shape
    cs = CHUNKS_PER_STEP
    blk = cs * CHUNK
    nb = t // blk
    w = GDN_QK

    def fwd(*tail):
        return lambda bi, n: (bi, n) + tail

    def bwd(*tail):
        return lambda bi, n: (bi, nb - 1 - n) + tail

    s_spec = pl.BlockSpec((1, 2, NQUAD, GDN_DK, QUAD * GDN_DV), lambda bi, n: (bi, 0, 0, 0, 0))
    in_specs = (
        [pl.BlockSpec((1, blk, w), fwd(o)) for o in range(3)]
        + [pl.BlockSpec((1, blk, w), bwd(o)) for o in range(3)]
        + [pl.BlockSpec((1, blk, LANES), fwd(0)), pl.BlockSpec((1, blk, LANES), bwd(0)),
           pl.BlockSpec((1, cs, 4 * NQUAD, QW), fwd(0, 0)), pl.BlockSpec((1, cs, 4 * NQUAD, QW), bwd(0, 0)),
           s_spec])
    s_shape = jax.ShapeDtypeStruct(s0.shape, F32)
    if with_out:
        out_specs = [pl.BlockSpec((1, blk, w), fwd(0)), pl.BlockSpec((1, blk, w), bwd(0)), s_spec]
        o_shape = jax.ShapeDtypeStruct((b, t, GDN_V), BF16)
        out_shape = [o_shape, o_shape, s_shape]
    else:
        out_specs = [s_spec]
        out_shape = [s_shape]
    return [qkv] * 6 + [gcol, gcol, grow, grow, s0], in_specs, out_specs, out_shape


def _ml_body(refs, with_out):
    (qf, kf, vf, qb, kb, vb, gcf, gcb, grf, grb, c0_ref, n0_ref, m0_ref) = refs[:13]
    if with_out:
        of_ref, ob_ref, cfin_ref, nfin_ref, mfin_ref = refs[13:]
    else:
        cfin_ref, nfin_ref, mfin_ref = refs[13:]
        of_ref = ob_ref = None

    @pl.when(pl.program_id(1) == 0)
    def _():
        cfin_ref[...] = c0_ref[...]
        nfin_ref[...] = n0_ref[...]
        mfin_ref[...] = m0_ref[...]

    scale = ML_DK ** -0.5
    heads = tuple(range(ML_HEADS))

    def rows_of(c):
        return slice(c * CHUNK, (c + 1) * CHUNK)

    def spread(xs):
        return jnp.concatenate([jnp.broadcast_to(x, (CHUNK, 1)) for x in xs], axis=0)

    groups = []
    cs = grf.shape[1]
    for ci, d in [(ci, d) for ci in range(cs) for d in range(2)]:
        q_ref, k_ref, v_ref, gc_ref, gr_ref, o_ref = (
            (qb, kb, vb, gcb, grb, ob_ref) if d else (qf, kf, vf, gcf, grf, of_ref))
        ch = cs - 1 - ci if d else ci
        rows = slice(ch * CHUNK, (ch + 1) * CHUNK)

        def stack(ref, width):
            return jnp.concatenate([ref[0, rows, h * width:(h + 1) * width] for h in heads], axis=0)

        def col(base):
            lanes = [base + G_I + d * ML_HEADS + h for h in heads]
            return jnp.concatenate([gc_ref[0, rows, l:l + 1] for l in lanes], axis=0)

        grp = dict(d=d, rows=rows, o_ref=o_ref, k=stack(k_ref, ML_DK), v=stack(v_ref, ML_DV),
                   bcol=col(LANES), ucol=col(2 * LANES), dmcol=col(3 * LANES), urow=gr_ref[0, ch, d:d + 1, :])
        if with_out:
            grp["q"] = stack(q_ref, ML_DK)
            grp["qk"] = _dot_nt(grp["q"], grp["k"])
        groups.append(grp)
        yield

    row = lax.broadcasted_iota(jnp.int32, (QW, ML_DV), 0)
    for grp in groups:
        d, k4, v4, bcol, ucol, dmcol = grp["d"], grp["k"], grp["v"], grp["bcol"], grp["ucol"], grp["dmcol"]
        end = 0 if d else CHUNK - 1
        m = [mfin_ref[0, d, h, 0:1, 0:1] for h in heads]
        n = [nfin_ref[0, d, h, 0:1, :] for h in heads]
        tot = [bcol[c * CHUNK + end:c * CHUNK + end + 1] for c in heads]
        m_new = [jnp.maximum(tot[c] + m[c], dmcol[c * CHUNK + end:c * CHUNK + end + 1]) for c in heads]
        dec = [jnp.exp(tot[c] + m[c] - m_new[c]) for c in heads]
        wt = jnp.exp(spread(tot) + ucol - spread(m_new))
        vw = v4.astype(F32) * wt
        vbd = jnp.concatenate(
            [jnp.where((row >= c * CHUNK) & (row < (c + 1) * CHUNK), vw, 0.0).astype(BF16) for c in heads], axis=1)
        c4 = cfin_ref[0, d]
        dec_lane = jnp.concatenate([jnp.broadcast_to(dec[c], (1, ML_DV)) for c in heads], axis=1)
        cfin_ref[0, d] = c4 * dec_lane + _dot_tn(k4, vbd)
        kw = k4.astype(F32) * wt
        for c in heads:
            n_next = dec[c] * n[c] + jnp.sum(kw[rows_of(c)], axis=0, keepdims=True)
            nfin_ref[0, d, c] = jnp.broadcast_to(n_next, (8, ML_DK))
            mfin_ref[0, d, c] = jnp.broadcast_to(m_new[c], (8, LANES))
        yield
        if not with_out:
            continue
        q4 = grp["q"]
        _, incl, _ = _quad_masks(bool(d))
        m4 = spread(m)
        m_t = jnp.maximum(bcol + m4, dmcol)
        inter = jnp.exp(bcol + m4 - m_t)
        p = grp["qk"] * scale * jnp.exp(jnp.where(incl, bcol + grp["urow"] - m_t, NEG_BIG))
        c4b = c4.astype(BF16)
        qc = jnp.concatenate([_dot(q4[rows_of(c)], c4b[:, c * ML_DV:(c + 1) * ML_DV]) for c in heads], axis=0)
        num = (inter * scale) * qc + _dot(p.astype(BF16), v4)
        n_rows = jnp.concatenate([jnp.broadcast_to(n[c], (CHUNK, ML_DK)) for c in heads], axis=0)
        qn = jnp.sum(q4.astype(F32) * n_rows, axis=1, keepdims=True) * scale
        den = inter * qn + jnp.sum(p, axis=1, keepdims=True)
        out = num * (1.0 / jnp.maximum(jnp.abs(den), jnp.exp(-m_t)))
        for c in heads:
            grp["o_ref"][0, grp["rows"], c * ML_DV:(c + 1) * ML_DV] = out[rows_of(c)].astype(BF16)
        yield


def _ml_operands(z, col0, gates, urow, c0, n0, m0, with_out):
    b, t, _ = z.shape
    cs = CHUNKS_PER_STEP
    blk = cs * CHUNK
    nb = t // blk
    qoff = col0 * LANES // ML_QK
    koff = (col0 * LANES + ML_QK) // ML_QK
    voff = (col0 * LANES + 2 * ML_QK) // ML_V

    def fwd(*tail):
        return lambda bi, n: (bi, n) + tail

    def bwd(*tail):
        return lambda bi, n: (bi, nb - 1 - n) + tail

    def state_spec(shape):
        return pl.BlockSpec((1,) + shape, lambda bi, n: (bi,) + (0,) * len(shape))

    st = [(2, ML_DK, ML_HEADS * ML_DV), (2, ML_HEADS, 8, ML_DK), (2, ML_HEADS, 8, LANES)]
    st_specs = [state_spec(s) for s in st]
    st_shapes = [jax.ShapeDtypeStruct((b,) + s, F32) for s in st]
    in_specs = (
        [pl.BlockSpec((1, blk, ML_QK), fwd(qoff)), pl.BlockSpec((1, blk, ML_QK), fwd(koff)),
         pl.BlockSpec((1, blk, ML_V), fwd(voff)),
         pl.BlockSpec((1, blk, ML_QK), bwd(qoff)), pl.BlockSpec((1, blk, ML_QK), bwd(koff)),
         pl.BlockSpec((1, blk, ML_V), bwd(voff)),
         pl.BlockSpec((1, blk, 4 * LANES), fwd(0)), pl.BlockSpec((1, blk, 4 * LANES), bwd(0)),
         pl.BlockSpec((1, cs, 8, QW), fwd(0, 0)), pl.BlockSpec((1, cs, 8, QW), bwd(0, 0))]
        + st_specs)
    if with_out:
        out_specs = [pl.BlockSpec((1, blk, ML_V), fwd(0)), pl.BlockSpec((1, blk, ML_V), bwd(0))] + st_specs
        o_shape = jax.ShapeDtypeStruct((b, t, ML_V), BF16)
        out_shape = [o_shape, o_shape] + st_shapes
    else:
        out_specs = st_specs
        out_shape = st_shapes
    return [z] * 6 + [gates, gates, urow, urow, c0, n0, m0], in_specs, out_specs, out_shape


GDN_PIECES_PER_GROUP = 9
ML_PIECES_PER_GROUP = 3


def _scan_kernel(*refs, n_gdn_in, n_ml_in, n_gdn_out, with_out):
    n_in = n_gdn_in + n_ml_in
    gdn_refs = refs[:n_gdn_in] + refs[n_in:n_in + n_gdn_out]
    ml_refs = refs[n_gdn_in:n_in] + refs[n_in + n_gdn_out:]
    gdn = _gdn_body(gdn_refs, with_out)
    ml = _ml_body(ml_refs, with_out)
    n_gdn = GDN_PIECES_PER_GROUP * 2 * NQUAD * CHUNKS_PER_STEP
    n_ml = (ML_PIECES_PER_GROUP if with_out else ML_PIECES_PER_GROUP - 1) * 2 * CHUNKS_PER_STEP
    done = 0
    for k, _ in enumerate(gdn, 1):
        while done < min(k * n_ml // n_gdn, n_ml):
            next(ml, None)
            done += 1
    for _ in ml:
        pass


def _scan_call(gdn_args, ml_args, with_out):
    g_ops, g_in, g_out, g_shape = _gdn_operands(*gdn_args, with_out)
    m_ops, m_in, m_out, m_shape = _ml_operands(*ml_args, with_out)
    b, t, _ = g_ops[0].shape
    res = pl.pallas_call(
        functools.partial(_scan_kernel, n_gdn_in=len(g_ops), n_ml_in=len(m_ops), n_gdn_out=len(g_out),
                          with_out=with_out),
        grid=(b, t // (CHUNKS_PER_STEP * CHUNK)),
        in_specs=g_in + m_in,
        out_specs=g_out + m_out,
        out_shape=g_shape + m_shape,
        compiler_params=pltpu.CompilerParams(
            dimension_semantics=("parallel", "arbitrary"), vmem_limit_bytes=SCAN_VMEM_LIMIT),
        name="scan_out" if with_out else "scan_state",
    )(*g_ops, *m_ops)
    return res[:len(g_out)], res[len(g_out):]


def _head_rms(x, width):
    outs = []
    for h in range(x.shape[1] // width):
        xh = x[:, h * width:(h + 1) * width]
        ms = jnp.sum(xh * xh, axis=-1, keepdims=True) * (1.0 / width)
        outs.append(xh * lax.rsqrt(ms + EPS))
    return jnp.concatenate(outs, axis=-1)


def _merge_kernel(x_ref, of_ref, ob_ref, hf_ref, hb_ref, gz_ref, mo_ref, gg_ref, gm_ref,
                  gnw_ref, mnw_ref, wg_ref, wm_ref, wo_ref, gate_ref, o_ref):
    f32 = lambda ref: ref[0].astype(F32)
    og = _head_rms(f32(of_ref) + f32(ob_ref), GDN_DV) * gnw_ref[...]
    gz = f32(gz_ref)
    og = og * (gz * _sigmoid(gz))
    y_gdn = _dot(og.astype(BF16), wg_ref[...])
    hm = _head_rms(f32(hf_ref) + f32(hb_ref), ML_DV) * mnw_ref[...]
    hm = hm * _sigmoid(f32(mo_ref))
    y_ml = _dot(hm.astype(BF16), wm_ref[...])
    merged = _sigmoid(f32(gg_ref)) * y_gdn + _sigmoid(f32(gm_ref)) * y_ml
    x_mix = _dot(merged.astype(BF16), wo_ref[...])
    o_ref[0] = x_ref[0] + gate_ref[0] * x_mix


def _merge_call(x, o_f, o_b, h_f, h_b, z, zo_col0, gnw, mnw, wg, wm, wo, gate):
    b, t, d = x.shape
    tm = _pick_tile(t, 512)
    nz = zo_col0 * LANES // d

    def tok(bi, i):
        return (bi, i, 0)

    def zcol(k):
        return lambda bi, i: (bi, i, nz + k)

    tile = lambda imap: pl.BlockSpec((1, tm, d), imap)
    full = lambda shape: pl.BlockSpec(shape, lambda bi, i: (0,) * len(shape))
    return pl.pallas_call(
        _merge_kernel,
        grid=(b, t // tm),
        in_specs=[tile(tok), tile(tok), tile(tok), tile(tok), tile(tok),
                  tile(zcol(0)), tile(zcol(1)), tile(zcol(2)), tile(zcol(3)),
                  full((1, d)), full((1, d)), full((d, d)), full((d, d)), full((d, d)),
                  pl.BlockSpec((1, 1, d), lambda bi, i: (bi, 0, 0))],
        out_specs=tile(tok),
        out_shape=jax.ShapeDtypeStruct((b, t, d), F32),
        compiler_params=_cparams("parallel", "parallel"),
        name="merge",
    )(x, o_f, o_b, h_f, h_b, z, z, z, z, gnw, mnw, wg, wm, wo, gate)


def _ffn_kernel(x_ref, xp_ref, xn_ref, nw2_ref, sh_ref, sc_ref, wup_ref, cwg_ref, cwv_ref, wd_ref,
                gate_ref, nwo_ref, o_ref, hn_ref, ua_ref, ub_ref, acta_ref, actb_ref, acc_ref):
    i = pl.program_id(1)
    j = pl.program_id(2)
    nj = wd_ref.shape[0]
    first = i == 0
    last = i == pl.num_programs(1) - 1
    tm = x_ref.shape[1]
    n_img_rows = tm // GRID_W
    col = lax.broadcasted_iota(jnp.int32, (GRID_W, LANES), 0)

    def up_project(u_ref, jt):
        hn = hn_ref[...]
        u_ref[0] = _dot(hn, wup_ref[jt])
        u_ref[1] = _dot(hn, wup_ref[nj + jt])

    @pl.when(j == 0)
    def _():
        def norm_mod(x):
            ms = jnp.mean(x * x, axis=-1, keepdims=True)
            y = x * lax.rsqrt(ms + EPS) * nw2_ref[...]
            return (y * (1.0 + sc_ref[0]) + sh_ref[0]).astype(BF16)

        hn_ref[0:GRID_W] = norm_mod(xp_ref[0])
        hn_ref[GRID_W:GRID_W + tm] = norm_mod(x_ref[0])
        hn_ref[GRID_W + tm:] = norm_mod(xn_ref[0])
        up_project(ua_ref, 0)
        actb_ref[...] = jnp.zeros(actb_ref.shape, BF16)
        acc_ref[...] = jnp.zeros(acc_ref.shape, F32)

    def row_conv(u_ref, cw_ref, cols):
        w = [cw_ref[k:k + 1, cols] for k in range(9)]
        cache = {}

        def strip(r):
            if r not in cache:
                s = u_ref[(r + 1) * GRID_W:(r + 2) * GRID_W, cols]
                if r < 0:
                    s = jnp.where(first, 0.0, s)
                elif r == n_img_rows:
                    s = jnp.where(last, 0.0, s)
                cache[r] = s
            return cache[r]

        def conv(r):
            taps = [strip(r - 1), strip(r), strip(r + 1)]
            side = [taps[0] * w[dc] + taps[1] * w[3 + dc] + taps[2] * w[6 + dc] for dc in range(3)]
            left = jnp.where(col == 0, 0.0, pltpu.roll(side[0], 1, 0))
            right = jnp.where(col == GRID_W - 1, 0.0, pltpu.roll(side[2], GRID_W - 1, 0))
            return side[1] + left + right

        return conv

    def step(u_cur, act_cur, u_next, act_prev):
        jn = jnp.minimum(j + 1, nj - 1)
        jp = jnp.maximum(j - 1, 0)
        ext = tm + 2 * GRID_W
        half_e, half_t = ext // FFN_ROW_SPLIT, tm // FFN_ROW_SPLIT

        def up_piece(which, lo):
            rows = slice(lo, lo + half_e)
            return lambda: u_next.__setitem__((which, rows), _dot(hn_ref[rows], wup_ref[which * nj + jn]))

        def down_piece(lo):
            rows = slice(lo, lo + half_t)

            def run():
                acc_ref[rows] += _dot(act_prev[rows], wd_ref[jp])
            return run

        def conv_piece(cb, conv_g, conv_v, r):
            cols = slice(cb * LANES, (cb + 1) * LANES)

            def run():
                g = conv_g(r)
                act_cur[r * GRID_W:(r + 1) * GRID_W, cols] = (g * _sigmoid(g) * conv_v(r)).astype(BF16)
            return run

        mxu_work = ([up_piece(which, k * half_e) for which in range(2) for k in range(FFN_ROW_SPLIT)]
                    + [down_piece(k * half_t) for k in range(FFN_ROW_SPLIT)])
        vpu_work = []
        for cb in range(act_cur.shape[1] // LANES):
            cols = slice(cb * LANES, (cb + 1) * LANES)
            conv_g = row_conv(u_cur.at[0], cwg_ref, cols)
            conv_v = row_conv(u_cur.at[1], cwv_ref, cols)
            vpu_work += [conv_piece(cb, conv_g, conv_v, r) for r in range(n_img_rows)]
        for piece in mxu_work:
            piece()
        for run in vpu_work:
            run()

    @pl.when((j & 1) == 0)
    def _():
        step(ua_ref, acta_ref, ub_ref, actb_ref)

    @pl.when((j & 1) == 1)
    def _():
        step(ub_ref, actb_ref, ua_ref, acta_ref)

    @pl.when(j == nj - 1)
    def _():
        act_last = acta_ref if (nj - 1) % 2 == 0 else actb_ref
        acc_ref[...] += _dot(act_last[...], wd_ref[nj - 1])
        x = x_ref[0] + gate_ref[0] * acc_ref[...]
        ms = jnp.mean(x * x, axis=-1, keepdims=True)
        o_ref[0] = x * lax.rsqrt(ms + EPS) * nwo_ref[...]


def _ffn_call(x, nw2, shift, scale, w_up, conv_w, w_down, gate, nwo):
    b, t, d = x.shape
    f = w_down.shape[0]
    tm = _pick_tile(t, 1024)
    tc = 256
    nj = f // tc
    rpt = tm // GRID_W
    nrows = t // GRID_W
    wup_tiles = w_up.reshape(d, 2 * nj, tc).transpose(1, 0, 2)
    wd_tiles = w_down.reshape(nj, tc, d)
    const = lambda *shape: pl.BlockSpec(shape, lambda bi, i, j: (0,) * len(shape))
    resident = lambda *shape: pl.BlockSpec(shape, lambda bi, i, j: (0,) * len(shape), pipeline_mode=pl.Buffered(1))
    per_batch = pl.BlockSpec((1, 1, d), lambda bi, i, j: (bi, 0, 0))
    return pl.pallas_call(
        _ffn_kernel,
        grid=(b, t // tm, nj),
        in_specs=[pl.BlockSpec((1, tm, d), lambda bi, i, j: (bi, i, 0)),
                  pl.BlockSpec((1, GRID_W, d), lambda bi, i, j: (bi, jnp.maximum(i * rpt - 1, 0), 0)),
                  pl.BlockSpec((1, GRID_W, d), lambda bi, i, j: (bi, jnp.minimum((i + 1) * rpt, nrows - 1), 0)),
                  const(1, d), per_batch, per_batch,
                  resident(2 * nj, d, tc),
                  pl.BlockSpec((9, tc), lambda bi, i, j: (0, j)),
                  pl.BlockSpec((9, tc), lambda bi, i, j: (0, nj + j)),
                  resident(nj, tc, d),
                  per_batch, const(1, d)],
        out_specs=pl.BlockSpec((1, tm, d), lambda bi, i, j: (bi, i, 0)),
        out_shape=jax.ShapeDtypeStruct((b, t, d), F32),
        scratch_shapes=[pltpu.VMEM((tm + 2 * GRID_W, d), BF16),
                        pltpu.VMEM((2, tm + 2 * GRID_W, tc), F32), pltpu.VMEM((2, tm + 2 * GRID_W, tc), F32),
                        pltpu.VMEM((tm, tc), BF16), pltpu.VMEM((tm, tc), BF16), pltpu.VMEM((tm, d), F32)],
        compiler_params=pltpu.CompilerParams(
            dimension_semantics=("parallel", "parallel", "arbitrary"), vmem_limit_bytes=FFN_VMEM_LIMIT),
        name="ffn",
    )(x, x, x, nw2, shift, scale, wup_tiles, conv_w, conv_w, wd_tiles, gate, nwo)


def _gdn_gate_rows(gates):
    b, t, _ = gates.shape
    nc = t // CHUNK

    def rows(base):
        q = gates[:, :, base:base + 2 * GDN_HEADS].reshape(b, nc, CHUNK, 2, NQUAD, QUAD)
        return q.transpose(0, 1, 3, 4, 5, 2).reshape(b, nc, 2 * NQUAD, QW)

    return jnp.concatenate([rows(G_A), rows(G_B)], axis=2)


def _ml_gate_rows(gates):
    b, t, _ = gates.shape
    nc = t // CHUNK
    base = 2 * LANES + G_I
    u = gates[:, :, base:base + 2 * ML_HEADS].reshape(b, nc, CHUNK, 2, ML_HEADS)
    u = u.transpose(0, 1, 3, 4, 2).reshape(b, nc, 2, QW)
    return jnp.pad(u, ((0, 0), (0, 0), (0, 6), (0, 0)))


def _mixer_states(x_seq, nw, shift, scale, w_state, w_aux, conv_w, alog_row, bias_row, states,
                  with_out, w_full=None):
    w = w_full if with_out else w_state
    z, z_aux = _nmm_call(x_seq, nw, shift, scale, w, w_aux, name="in_proj")
    gates = _gates_call(z_aux, alog_row, bias_row)
    qkv = _conv_call(z, conv_w)
    s_gdn, c_ml, n_ml, m_ml = states
    ml_col0 = (2 * GDN_QK + GDN_V) // LANES
    gdn_res, ml_res = _scan_call(
        (qkv, gates[:, :, :LANES], _gdn_gate_rows(gates), s_gdn),
        (z, ml_col0, gates, _ml_gate_rows(gates), c_ml, n_ml, m_ml), with_out)
    return z, gdn_res, ml_res


def kernel(x, c, ctx, c_ctx, w_ada, b_ada, norm1_w, w_in, gdn_conv, gdn_a_log, gdn_dt_bias, gdn_norm_w,
           ml_igate_b, ml_fgate_b, ml_norm_w, w_branch_gdn, w_branch_ml, w_out, norm2_w, w_up, ffn_conv,
           w_down, norm_out_w):
    bsz, _, d = x.shape
    depth = w_ada.shape[0]
    assert depth == 1, "single-layer problem: the context stream is never updated"
    l = 0

    sizes = (2 * GDN_QK + GDN_V, 2 * GDN_HEADS, 2 * GDN_HEADS, ML_QK, ML_QK, ML_V, 2 * ML_HEADS, 2 * ML_HEADS,
             GDN_V, ML_V, d, d)
    offs = [0]
    for s in sizes:
        offs.append(offs[-1] + s)
    wi = w_in[l]
    seg = lambda k: wi[:, offs[k]:offs[k + 1]]
    w_state = jnp.concatenate([seg(0), seg(3), seg(4), seg(5)], axis=1).astype(BF16)
    w_full = jnp.concatenate([seg(0), seg(3), seg(4), seg(5), seg(8), seg(9), seg(10), seg(11)], axis=1).astype(BF16)
    n_gate = 4 * GDN_HEADS + 4 * ML_HEADS
    w_aux = jnp.concatenate([seg(1), seg(2), seg(6), seg(7), jnp.zeros((d, LANES - n_gate), F32)], axis=1).astype(BF16)
    pad = lambda v, n: jnp.pad(v.reshape(1, -1).astype(F32), ((0, 0), (0, n - v.size)))
    alog_row = pad(gdn_a_log[l], LANES)
    bias_row = pad(jnp.concatenate([gdn_dt_bias[l].reshape(-1), jnp.zeros((2 * GDN_HEADS,), F32),
                                    ml_igate_b[l].reshape(-1), ml_fgate_b[l].reshape(-1)]), LANES)
    row = lambda v: v.reshape(1, -1).astype(F32)

    c_all = jnp.concatenate([c, c_ctx[None], jnp.zeros((8 - bsz - 1, d), F32)], axis=0)
    mods = _mod_call(c_all, w_ada[l].astype(BF16), row(b_ada[l]))
    mod_x = mods[:bsz].reshape(bsz, N_MOD, 1, d)
    mod_c = jnp.broadcast_to(mods[bsz].reshape(1, N_MOD, 1, d), (bsz, N_MOD, 1, d))

    zero_states = (jnp.zeros((bsz, 2, NQUAD, GDN_DK, QUAD * GDN_DV), F32),
                   jnp.zeros((bsz, 2, ML_DK, ML_HEADS * ML_DV), F32),
                   jnp.zeros((bsz, 2, ML_HEADS, 8, ML_DK), F32),
                   jnp.zeros((bsz, 2, ML_HEADS, 8, LANES), F32))
    common = (w_state, w_aux, gdn_conv[l].astype(F32), alog_row, bias_row)

    _, (s_gdn,), (c_ml, n_ml, m_ml) = _mixer_states(
        ctx, row(norm1_w[l]), mod_c[:, 0], mod_c[:, 1], *common, zero_states, False)

    z, (o_f, o_b, _), (h_f, h_b, _, _, _) = _mixer_states(
        x, row(norm1_w[l]), mod_x[:, 0], mod_x[:, 1], *common, (s_gdn, c_ml, n_ml, m_ml), True, w_full)
    x1 = _merge_call(x, o_f, o_b, h_f, h_b, z, STATE_COLS // LANES,
                     row(jnp.tile(gdn_norm_w[l], GDN_HEADS)), row(ml_norm_w[l]),
                     w_branch_gdn[l].astype(BF16), w_branch_ml[l].astype(BF16), w_out[l].astype(BF16),
                     mod_x[:, 2])
    return _ffn_call(x1, row(norm2_w[l]), mod_x[:, 3], mod_x[:, 4], w_up[l].astype(BF16),
                     ffn_conv[l].reshape(9, -1).astype(F32), w_down[l].astype(BF16), mod_x[:, 5], row(norm_out_w))
```
